```python
import jax, jax.numpy as jnp
from jax import lax
import numpy as np

D_MODEL = 2048
BATCH = 1
SEQ = 8192
DEPTH = 2
DEC_BATCH = 16
DEC_SEQ = 2048
PAST_LEN = 128

GRID_W = 64
N_MIXERS = 2
NA_HEADS = 16
NA_HEAD_DIM = 128
NA_WIN_H_MAX = 8
NA_WIN_W = 16
NA_QBLK_W = 16
NA_KBLK_W = 32
MLA_HEADS = 16
MLA_Q_LORA = 512
MLA_KV_LORA = 512
MLA_NOPE = 128
MLA_ROPE = 64
MLA_V = 128
MLA_QBLK = 128
ROPE_THETA = 10000.0
D_FF = 5632
N_EXPERTS = 8
TOP_K = 2
D_FF_EXPERT = 5632
EPS = 1e-6
NEG_INF = -1e30
N_EVEN = (DEPTH + 1) // 2
N_ODD = DEPTH // 2

kernel_name = "hybrid_natten_mla_encoder"


def rms_norm(x, g):
    xf = x.astype(jnp.float32)
    y = xf * lax.rsqrt(jnp.mean(xf * xf, axis=-1, keepdims=True) + EPS)
    return (y * g.astype(jnp.float32)).astype(x.dtype)


def swiglu(h, w_gate, w_up, w_down):
    a = jnp.einsum('btd,df->btf', h, w_gate)
    b = jnp.einsum('btd,df->btf', h, w_up)
    return jnp.einsum('btf,fd->btd', jax.nn.silu(a) * b, w_down)


def neighborhood_attention(h, w_qkv, q_gain, k_gain, rpb, w_o):
    B, T, _ = h.shape
    rows = T // GRID_W
    kh = min(NA_WIN_H_MAX, rows)
    qkv = jnp.einsum('btd,de->bte', h, w_qkv).reshape(B, rows, GRID_W, 3, NA_HEADS, NA_HEAD_DIM)
    q = rms_norm(qkv[:, :, :, 0], q_gain)
    k = rms_norm(qkv[:, :, :, 1], k_gain)
    v = qkv[:, :, :, 2]
    n_cb = GRID_W // NA_QBLK_W
    q_cols = jnp.arange(GRID_W).reshape(n_cb, NA_QBLK_W)
    band_start = jnp.clip(jnp.arange(n_cb) * NA_QBLK_W - NA_WIN_W // 2, 0, GRID_W - NA_KBLK_W)
    k_cols = band_start[:, None] + jnp.arange(NA_KBLK_W)
    win_start = jnp.clip(q_cols - NA_WIN_W // 2, 0, GRID_W - NA_WIN_W)
    kc = k_cols[:, None, :]
    col_valid = (kc >= win_start[..., None]) & (kc < win_start[..., None] + NA_WIN_W)
    dc_idx = jnp.clip(kc - q_cols[..., None] + NA_WIN_W - 1, 0, 2 * NA_WIN_W - 2)
    rpb_c = rpb[:, :, dc_idx].astype(jnp.float32)
    row_start = jnp.clip(jnp.arange(rows) - kh // 2, 0, rows - kh)
    scale = NA_HEAD_DIM ** -0.5

    def row_block(args):
        q_r, r, rs = args
        k_b = lax.dynamic_slice_in_dim(k, rs, kh, axis=1)[:, :, k_cols]
        v_b = lax.dynamic_slice_in_dim(v, rs, kh, axis=1)[:, :, k_cols]
        q_b = q_r.reshape(B, n_cb, NA_QBLK_W, NA_HEADS, NA_HEAD_DIM)
        s = jnp.einsum('bjqhd,bkjwhd->bhjqkw', q_b, k_b).astype(jnp.float32) * scale
        dr_idx = rs + jnp.arange(kh) - r + NA_WIN_H_MAX - 1
        bias = jnp.take(rpb_c, dr_idx, axis=1).transpose(0, 2, 3, 1, 4)
        s = jnp.where(col_valid[:, :, None, :], s + bias[None], NEG_INF)
        p = jax.nn.softmax(s, axis=(-2, -1))
        o = jnp.einsum('bhjqkw,bkjwhd->bjqhd', p.astype(v.dtype), v_b)
        return o.reshape(B, GRID_W, NA_HEADS * NA_HEAD_DIM)

    out = lax.map(row_block, (jnp.moveaxis(q, 1, 0), jnp.arange(rows), row_start))
    out = jnp.moveaxis(out, 0, 1).reshape(B, T, NA_HEADS * NA_HEAD_DIM)
    return jnp.einsum('bte,ed->btd', out, w_o)


def axial_rope_tables(T):
    t = jnp.arange(T)
    row = (t // GRID_W).astype(jnp.float32)
    col = (t % GRID_W).astype(jnp.float32)
    n_pairs = MLA_ROPE // 4
    inv = ROPE_THETA ** (-jnp.arange(n_pairs, dtype=jnp.float32) / n_pairs)
    ang = jnp.concatenate([row[:, None] * inv, col[:, None] * inv], axis=-1)
    return jnp.cos(ang), jnp.sin(ang)


def apply_rope(x, cos, sin):
    half = x.shape[-1] // 2
    x1 = x[..., :half].astype(jnp.float32)
    x2 = x[..., half:].astype(jnp.float32)
    return jnp.concatenate([x1 * cos - x2 * sin, x1 * sin + x2 * cos], axis=-1).astype(x.dtype)


def mla(h, w_dqkv, q_lora_gain, kv_lora_gain, w_uq, w_ukv, qn_gain, qr_gain, kn_gain, kr_gain, w_o):
    B, T, _ = h.shape
    lat = jnp.einsum('btd,de->bte', h, w_dqkv)
    c_q = rms_norm(lat[..., :MLA_Q_LORA], q_lora_gain)
    c_kv = rms_norm(lat[..., MLA_Q_LORA:MLA_Q_LORA + MLA_KV_LORA], kv_lora_gain)
    k_rope_raw = lat[..., MLA_Q_LORA + MLA_KV_LORA:]
    cos, sin = axial_rope_tables(T)
    q = jnp.einsum('btc,ce->bte', c_q, w_uq).reshape(B, T, MLA_HEADS, MLA_NOPE + MLA_ROPE)
    q_nope = rms_norm(q[..., :MLA_NOPE], qn_gain)
    q_rope = apply_rope(rms_norm(q[..., MLA_NOPE:], qr_gain), cos[:, None, :], sin[:, None, :])
    kv = jnp.einsum('btc,ce->bte', c_kv, w_ukv).reshape(B, T, MLA_HEADS, MLA_NOPE + MLA_V)
    k_nope = rms_norm(kv[..., :MLA_NOPE], kn_gain)
    v = kv[..., MLA_NOPE:]
    k_rope = apply_rope(rms_norm(k_rope_raw, kr_gain), cos, sin)
    scale = (MLA_NOPE + MLA_ROPE) ** -0.5
    nb = T // MLA_QBLK
    qn_b = jnp.moveaxis(q_nope.reshape(B, nb, MLA_QBLK, MLA_HEADS, MLA_NOPE), 1, 0)
    qr_b = jnp.moveaxis(q_rope.reshape(B, nb, MLA_QBLK, MLA_HEADS, MLA_ROPE), 1, 0)

    def q_block(args):
        qn, qr = args
        s = (jnp.einsum('bqhd,bkhd->bhqk', qn, k_nope)
             + jnp.einsum('bqhr,bkr->bhqk', qr, k_rope)).astype(jnp.float32) * scale
        p = jax.nn.softmax(s, axis=-1)
        return jnp.einsum('bhqk,bkhd->bqhd', p.astype(v.dtype), v)

    o = lax.map(q_block, (qn_b, qr_b))
    o = jnp.moveaxis(o, 0, 1).reshape(B, T, MLA_HEADS * MLA_V)
    return jnp.einsum('bte,ed->btd', o, w_o)


def moe_swiglu(h, w_router, w_gate, w_up, w_down):
    logits = jnp.einsum('btd,de->bte', h, w_router).astype(jnp.float32)
    top_val, top_idx = lax.top_k(logits, TOP_K)
    gates = jax.nn.softmax(top_val, axis=-1)
    combine = jnp.sum(jax.nn.one_hot(top_idx, N_EXPERTS, dtype=jnp.float32) * gates[..., None], axis=-2)
    combine = combine.astype(h.dtype)
    out = jnp.zeros_like(h)
    for e in range(N_EXPERTS):
        out = out + combine[..., e:e + 1] * swiglu(h, w_gate[e], w_up[e], w_down[e])
    return out


def trunk(x, mix_norm, ffn_norm,
          na_w_qkv, na_q_gain, na_k_gain, na_rpb, na_w_o,
          mla_w_dqkv, mla_q_lora_gain, mla_kv_lora_gain, mla_w_uq, mla_w_ukv,
          mla_qn_gain, mla_qr_gain, mla_kn_gain, mla_kr_gain, mla_w_o,
          ffn_w_gate, ffn_w_up, ffn_w_down,
          moe_w_router, moe_w_gate, moe_w_up, moe_w_down):
    for i in range(DEPTH):
        j = i // N_MIXERS
        h = rms_norm(x, mix_norm[i])
        if i % N_MIXERS == 0:
            x = x + neighborhood_attention(h, na_w_qkv[j], na_q_gain[j], na_k_gain[j], na_rpb[j], na_w_o[j])
        else:
            x = x + mla(h, mla_w_dqkv[j], mla_q_lora_gain[j], mla_kv_lora_gain[j], mla_w_uq[j], mla_w_ukv[j],
                        mla_qn_gain[j], mla_qr_gain[j], mla_kn_gain[j], mla_kr_gain[j], mla_w_o[j])
        h = rms_norm(x, ffn_norm[i])
        if i % 2 == 0:
            x = x + swiglu(h, ffn_w_gate[j], ffn_w_up[j], ffn_w_down[j])
        else:
            x = x + moe_swiglu(h, moe_w_router[j], moe_w_gate[j], moe_w_up[j], moe_w_down[j])
    return x


def setup_inputs(seed: int = 0) -> dict:
    key = jax.random.key(seed)
    ks = jax.random.split(key, 32)
    f32 = jnp.float32

    def w(k, shape, fan_in):
        return jax.random.normal(k, shape, f32) * (fan_in ** -0.5)

    def gain(k, shape):
        return 1.0 + 0.02 * jax.random.normal(k, shape, f32)

    na_dim = NA_HEADS * NA_HEAD_DIM
    return {
        "x_prompt": jax.random.normal(ks[0], (BATCH, SEQ, D_MODEL), f32),
        "x_sample": jax.random.normal(ks[1], (DEC_BATCH, DEC_SEQ, D_MODEL), f32),
        "mix_norm": gain(ks[2], (DEPTH, D_MODEL)),
        "ffn_norm": gain(ks[3], (DEPTH, D_MODEL)),
        "na_w_qkv": w(ks[4], (N_EVEN, D_MODEL, 3 * na_dim), D_MODEL),
        "na_q_gain": gain(ks[5], (N_EVEN, NA_HEAD_DIM)),
        "na_k_gain": gain(ks[6], (N_EVEN, NA_HEAD_DIM)),
        "na_rpb": 0.1 * jax.random.normal(ks[7], (N_EVEN, NA_HEADS, 2 * NA_WIN_H_MAX - 1, 2 * NA_WIN_W - 1), f32),
        "na_w_o": w(ks[8], (N_EVEN, na_dim, D_MODEL), na_dim),
        "mla_w_dqkv": w(ks[9], (N_ODD, D_MODEL, MLA_Q_LORA + MLA_KV_LORA + MLA_ROPE), D_MODEL),
        "mla_q_lora_gain": gain(ks[10], (N_ODD, MLA_Q_LORA)),
        "mla_kv_lora_gain": gain(ks[11], (N_ODD, MLA_KV_LORA)),
        "mla_w_uq": w(ks[12], (N_ODD, MLA_Q_LORA, MLA_HEADS * (MLA_NOPE + MLA_ROPE)), MLA_Q_LORA),
        "mla_w_ukv": w(ks[13], (N_ODD, MLA_KV_LORA, MLA_HEADS * (MLA_NOPE + MLA_V)), MLA_KV_LORA),
        "mla_qn_gain": gain(ks[14], (N_ODD, MLA_NOPE)),
        "mla_qr_gain": gain(ks[15], (N_ODD, MLA_ROPE)),
        "mla_kn_gain": gain(ks[16], (N_ODD, MLA_NOPE)),
        "mla_kr_gain": gain(ks[17], (N_ODD, MLA_ROPE)),
        "mla_w_o": w(ks[18], (N_ODD, MLA_HEADS * MLA_V, D_MODEL), MLA_HEADS * MLA_V),
        "ffn_w_gate": w(ks[19], (N_EVEN, D_MODEL, D_FF), D_MODEL),
        "ffn_w_up": w(ks[20], (N_EVEN, D_MODEL, D_FF), D_MODEL),
        "ffn_w_down": w(ks[21], (N_EVEN, D_FF, D_MODEL), D_FF),
        "moe_w_router": w(ks[22], (N_ODD, D_MODEL, N_EXPERTS), D_MODEL),
        "moe_w_gate": w(ks[23], (N_ODD, N_EXPERTS, D_MODEL, D_FF_EXPERT), D_MODEL),
        "moe_w_up": w(ks[24], (N_ODD, N_EXPERTS, D_MODEL, D_FF_EXPERT), D_MODEL),
        "moe_w_down": w(ks[25], (N_ODD, N_EXPERTS, D_FF_EXPERT, D_MODEL), D_FF_EXPERT),
    }


def reference(x_prompt, x_sample, mix_norm, ffn_norm,
              na_w_qkv, na_q_gain, na_k_gain, na_rpb, na_w_o,
              mla_w_dqkv, mla_q_lora_gain, mla_kv_lora_gain, mla_w_uq, mla_w_ukv,
              mla_qn_gain, mla_qr_gain, mla_kn_gain, mla_kr_gain, mla_w_o,
              ffn_w_gate, ffn_w_up, ffn_w_down,
              moe_w_router, moe_w_gate, moe_w_up, moe_w_down):
    params = (mix_norm, ffn_norm,
              na_w_qkv, na_q_gain, na_k_gain, na_rpb, na_w_o,
              mla_w_dqkv, mla_q_lora_gain, mla_kv_lora_gain, mla_w_uq, mla_w_ukv,
              mla_qn_gain, mla_qr_gain, mla_kn_gain, mla_kr_gain, mla_w_o,
              ffn_w_gate, ffn_w_up, ffn_w_down,
              moe_w_router, moe_w_gate, moe_w_up, moe_w_down)
    y_prompt = trunk(x_prompt, *params)
    y_sample = trunk(x_sample, *params)
    return (y_prompt, y_sample)
```

```python
import functools

import jax
import jax.numpy as jnp
from jax import lax
from jax.experimental import pallas as pl
from jax.experimental.pallas import tpu as pltpu

F32 = jnp.float32
BF16 = jnp.bfloat16

D_MODEL = 2048
GRID_W = 64
NA_HEADS = 16
NA_HEAD_DIM = 128
NA_WIN_H = 8
NA_WIN_W = 16
MLA_HEADS = 16
MLA_Q_LORA = 512
MLA_KV_LORA = 512
MLA_NOPE = 128
MLA_ROPE = 64
MLA_V = 128
ROPE_THETA = 10000.0
D_FF = 5632
N_EXPERTS = 8
TOP_K = 2
EPS = 1e-6
NEG_INF = -1e30

LANES = 128
VMEM_LIMIT = 56 * 1024 * 1024


def _params(sem):
    return pltpu.CompilerParams(dimension_semantics=sem, vmem_limit_bytes=VMEM_LIMIT)


def _rms(x, g):
    return x * lax.rsqrt(jnp.mean(x * x, axis=-1, keepdims=True) + EPS) * g


def _norm_matmul_kernel(x_ref, g_ref, w_ref, hg_ref, o_ref, h_scr, *, n_norm_tiles):
    j = pl.program_id(1)

    @pl.when(j == 0)
    def _():
        h_scr[...] = _rms(x_ref[...], g_ref[...]).astype(BF16)

    acc = jnp.dot(h_scr[...], w_ref[...], preferred_element_type=F32)
    tn = acc.shape[1]

    @pl.when(j < n_norm_tiles)
    def _():
        for c in range(tn // LANES):
            sl = slice(c * LANES, (c + 1) * LANES)
            o_ref[:, sl] = _rms(acc[:, sl], hg_ref[:, sl]).astype(o_ref.dtype)

    @pl.when(j >= n_norm_tiles)
    def _():
        o_ref[...] = acc.astype(o_ref.dtype)


def _norm_matmul(x, g, w, head_gain, n_norm_cols, *, tm, tn):
    m, d = x.shape
    n = w.shape[1]
    return pl.pallas_call(
        functools.partial(_norm_matmul_kernel, n_norm_tiles=n_norm_cols // tn),
        grid=(m // tm, n // tn),
        in_specs=[
            pl.BlockSpec((tm, d), lambda i, j: (i, 0)),
            pl.BlockSpec((1, d), lambda i, j: (0, 0)),
            pl.BlockSpec((d, tn), lambda i, j: (0, j)),
            pl.BlockSpec((1, tn), lambda i, j: (0, j)),
        ],
        out_specs=pl.BlockSpec((tm, tn), lambda i, j: (i, j)),
        out_shape=jax.ShapeDtypeStruct((m, n), BF16),
        scratch_shapes=[pltpu.VMEM((tm, d), BF16)],
        compiler_params=_params(("parallel", "arbitrary")),
    )(x, g, w, head_gain)


def _na_kernel(q_ref, k_ref, v_ref, b_ref, *rest, rows):
    o_ref = rest[-1]
    kw = NA_WIN_H * GRID_W

    def body(r, carry):
        rs = jnp.clip(r - NA_WIN_H // 2, 0, rows - NA_WIN_H)
        q = q_ref[pl.ds(pl.multiple_of(r * GRID_W, GRID_W), GRID_W), :]
        k = k_ref[pl.ds(pl.multiple_of(rs * GRID_W, GRID_W), kw), :]
        v = v_ref[pl.ds(pl.multiple_of(rs * GRID_W, GRID_W), kw), :]
        s = lax.dot_general(q, k, (((1,), (1,)), ((), ())), preferred_element_type=F32)
        s = s + b_ref[0, r - rs]
        m = jnp.max(s, axis=-1, keepdims=True)
        p = jnp.exp(s - m)
        l = jnp.sum(p, axis=-1, keepdims=True)
        o = jnp.dot(p.astype(BF16), v, preferred_element_type=F32) / l
        o_ref[pl.ds(pl.multiple_of(r * GRID_W, GRID_W), GRID_W), :] = o.astype(o_ref.dtype)
        return carry

    lax.fori_loop(0, rows, body, 0)


def _na_bias_table(rpb):
    qc = jnp.arange(GRID_W)[:, None]
    kc = jnp.arange(GRID_W)[None, :]
    ws = jnp.clip(qc - NA_WIN_W // 2, 0, GRID_W - NA_WIN_W)
    valid = (kc >= ws) & (kc < ws + NA_WIN_W)
    dc = jnp.clip(kc - qc + NA_WIN_W - 1, 0, 2 * NA_WIN_W - 2)
    c = jnp.arange(NA_WIN_H)[:, None]
    j = jnp.arange(NA_WIN_H)[None, :]
    dr = j - c + NA_WIN_H - 1
    t = rpb.astype(F32)[:, dr][:, :, :, dc]
    t = jnp.where(valid[None, None, None], t, NEG_INF)
    t = t.transpose(0, 1, 3, 2, 4)
    return t.reshape(rpb.shape[0], NA_WIN_H, GRID_W, NA_WIN_H * GRID_W)


def _na_attention(qkv, bias, out_prev, *, seq, n_batch, row_block_off):
    m = qkv.shape[0]
    rows = seq // GRID_W
    assert rows >= NA_WIN_H
    h = NA_HEADS
    kw = NA_WIN_H * GRID_W

    def spec(sec):
        return pl.BlockSpec((seq, NA_HEAD_DIM), lambda hh, b: (row_block_off + b, sec * h + hh))

    in_specs = [spec(0), spec(1), spec(2),
                pl.BlockSpec((1, NA_WIN_H, GRID_W, kw), lambda hh, b: (hh, 0, 0, 0))]
    args = [qkv, qkv, qkv, bias]
    aliases = {}
    if out_prev is not None:
        in_specs.append(pl.BlockSpec(memory_space=pl.ANY))
        args.append(out_prev)
        aliases = {4: 0}
    return pl.pallas_call(
        functools.partial(_na_kernel, rows=rows),
        grid=(h, n_batch),
        in_specs=in_specs,
        out_specs=pl.BlockSpec((seq, NA_HEAD_DIM), lambda hh, b: (row_block_off + b, hh)),
        out_shape=jax.ShapeDtypeStruct((m, h * NA_HEAD_DIM), BF16),
        input_output_aliases=aliases,
        compiler_params=_params(("parallel", "parallel")),
    )(*args)


def _matmul_res_kernel(a_ref, w_ref, r_ref, o_ref):
    o_ref[...] = r_ref[...] + jnp.dot(a_ref[...], w_ref[...], preferred_element_type=F32)


def _matmul_res(a, w, res, *, tm):
    m, k = a.shape
    n = w.shape[1]
    return pl.pallas_call(
        _matmul_res_kernel,
        grid=(m // tm,),
        in_specs=[
            pl.BlockSpec((tm, k), lambda i: (i, 0)),
            pl.BlockSpec((k, n), lambda i: (0, 0)),
            pl.BlockSpec((tm, n), lambda i: (i, 0)),
        ],
        out_specs=pl.BlockSpec((tm, n), lambda i: (i, 0)),
        out_shape=jax.ShapeDtypeStruct((m, n), F32),
        compiler_params=_params(("parallel",)),
    )(a, w, res)


def _swiglu_step(h, wg, wu, wd):
    a = jnp.dot(h, wg, preferred_element_type=F32)
    b = jnp.dot(h, wu, preferred_element_type=F32)
    t = (a / (1.0 + jnp.exp(-a)) * b).astype(BF16)
    return jnp.dot(t, wd, preferred_element_type=F32)


def _ffn_kernel(x_ref, g_ref, wg_ref, wu_ref, wd_ref, o_ref, h_scr):
    f = pl.program_id(1)

    @pl.when(f == 0)
    def _():
        x = x_ref[...]
        h_scr[...] = _rms(x, g_ref[...]).astype(BF16)
        o_ref[...] = x

    o_ref[...] += _swiglu_step(h_scr[...], wg_ref[...], wu_ref[...], wd_ref[...])


def _ffn(x, g, wg, wu, wd, *, tm, tf):
    m, d = x.shape
    f = wg.shape[1]
    return pl.pallas_call(
        _ffn_kernel,
        grid=(m // tm, f // tf),
        in_specs=[
            pl.BlockSpec((tm, d), lambda i, j: (i, 0)),
            pl.BlockSpec((1, d), lambda i, j: (0, 0)),
            pl.BlockSpec((d, tf), lambda i, j: (0, j)),
            pl.BlockSpec((d, tf), lambda i, j: (0, j)),
            pl.BlockSpec((tf, d), lambda i, j: (j, 0)),
        ],
        out_specs=pl.BlockSpec((tm, d), lambda i, j: (i, 0)),
        out_shape=jax.ShapeDtypeStruct((m, d), F32),
        scratch_shapes=[pltpu.VMEM((tm, d), BF16)],
        compiler_params=_params(("parallel", "arbitrary")),
    )(x, g, wg, wu, wd)


def _expert_ffn_kernel(te_ref, nu_ref, x_ref, g_ref, wg_ref, wu_ref, wd_ref, o_ref, h_scr):
    i = pl.program_id(0)
    f = pl.program_id(1)

    @pl.when(i < nu_ref[0])
    def _():
        @pl.when(f == 0)
        def _():
            h_scr[...] = _rms(x_ref[...], g_ref[...]).astype(BF16)
            o_ref[...] = jnp.zeros_like(o_ref)

        o_ref[...] += _swiglu_step(h_scr[...], wg_ref[0], wu_ref[0], wd_ref[0])


def _expert_ffn(tile_expert, n_used, xs, g, wg, wu, wd, *, tm, tf):
    p, d = xs.shape
    nf = wg.shape[2] // tf

    def row_map(i, j, te, nu):
        return (jnp.minimum(i, nu[0] - 1), 0)

    def wcol_map(i, j, te, nu):
        used = i < nu[0]
        return (te[jnp.minimum(i, nu[0] - 1)], 0, jnp.where(used, j, nf - 1))

    def wrow_map(i, j, te, nu):
        used = i < nu[0]
        return (te[jnp.minimum(i, nu[0] - 1)], jnp.where(used, j, nf - 1), 0)

    grid_spec = pltpu.PrefetchScalarGridSpec(
        num_scalar_prefetch=2,
        grid=(p // tm, nf),
        in_specs=[
            pl.BlockSpec((tm, d), row_map),
            pl.BlockSpec((1, d), lambda i, j, te, nu: (0, 0)),
            pl.BlockSpec((1, d, tf), wcol_map),
            pl.BlockSpec((1, d, tf), wcol_map),
            pl.BlockSpec((1, tf, d), wrow_map),
        ],
        out_specs=pl.BlockSpec((tm, d), row_map),
        scratch_shapes=[pltpu.VMEM((tm, d), BF16)],
    )
    return pl.pallas_call(
        _expert_ffn_kernel,
        grid_spec=grid_spec,
        out_shape=jax.ShapeDtypeStruct((p, d), F32),
        compiler_params=_params(("arbitrary", "arbitrary")),
    )(tile_expert, n_used, xs, g, wg, wu, wd)


ROPE_PAD = LANES


def _mla_proj_kernel(x_ref, g_ref, wd_ref, gq_ref, gkv_ref, gkr_ref, wuq_ref, gqn_ref, gqr_ref,
                     wukv_ref, gkn_ref, cos_ref, sinn_ref, sinp_ref,
                     qn_ref, qr_ref, kn_ref, kr_ref, v_ref):
    half = MLA_ROPE // 2
    cos = cos_ref[...]
    sinn = sinn_ref[...]
    sinp = sinp_ref[...]

    def rope(piece, gain):
        ms = jnp.sum(piece * piece, axis=-1, keepdims=True) * (1.0 / MLA_ROPE)
        y = piece * lax.rsqrt(ms + EPS) * gain
        return (y * cos + pltpu.roll(y, ROPE_PAD - half, 1) * sinn + pltpu.roll(y, half, 1) * sinp)

    h = _rms(x_ref[...], g_ref[...]).astype(BF16)
    lat = jnp.dot(h, wd_ref[...], preferred_element_type=F32)
    cq = _rms(lat[:, :MLA_Q_LORA], gq_ref[...]).astype(BF16)
    ckv = _rms(lat[:, MLA_Q_LORA:MLA_Q_LORA + MLA_KV_LORA], gkv_ref[...]).astype(BF16)
    kr_ref[...] = rope(lat[:, MLA_Q_LORA + MLA_KV_LORA:], gkr_ref[...]).astype(BF16)

    hd = MLA_HEADS * LANES
    chunk = 4 * LANES
    for c in range(hd // chunk):
        base = c * chunk
        qn = jnp.dot(cq, wuq_ref[:, base:base + chunk], preferred_element_type=F32)
        qr = jnp.dot(cq, wuq_ref[:, hd + base:hd + base + chunk], preferred_element_type=F32)
        kn = jnp.dot(ckv, wukv_ref[:, base:base + chunk], preferred_element_type=F32)
        v = jnp.dot(ckv, wukv_ref[:, hd + base:hd + base + chunk], preferred_element_type=F32)
        v_ref[:, base:base + chunk] = v.astype(BF16)
        for hh in range(chunk // LANES):
            sl = slice(hh * LANES, (hh + 1) * LANES)
            osl = slice(base + hh * LANES, base + (hh + 1) * LANES)
            qn_ref[:, osl] = _rms(qn[:, sl], gqn_ref[...]).astype(BF16)
            kn_ref[:, osl] = _rms(kn[:, sl], gkn_ref[...]).astype(BF16)
            qr_ref[:, osl] = rope(qr[:, sl], gqr_ref[...]).astype(BF16)


def _mla_proj(x, g, wd, gq, gkv, gkr, wuq, gqn, gqr, wukv, gkn, cos, sinn, sinp, table_block, *, tm):
    m, d = x.shape
    hd = MLA_HEADS * LANES
    const = lambda i: (0, 0)
    row = lambda i: (i, 0)
    tab = lambda i: (table_block(i), 0)
    return pl.pallas_call(
        _mla_proj_kernel,
        grid=(m // tm,),
        in_specs=[
            pl.BlockSpec((tm, d), row),
            pl.BlockSpec((1, d), const),
            pl.BlockSpec(wd.shape, const),
            pl.BlockSpec((1, MLA_Q_LORA), const),
            pl.BlockSpec((1, MLA_KV_LORA), const),
            pl.BlockSpec((1, LANES), const),
            pl.BlockSpec(wuq.shape, const),
            pl.BlockSpec((1, LANES), const),
            pl.BlockSpec((1, LANES), const),
            pl.BlockSpec(wukv.shape, const),
            pl.BlockSpec((1, LANES), const),
            pl.BlockSpec((tm, LANES), tab),
            pl.BlockSpec((tm, LANES), tab),
            pl.BlockSpec((tm, LANES), tab),
        ],
        out_specs=[
            pl.BlockSpec((tm, hd), row),
            pl.BlockSpec((tm, hd), row),
            pl.BlockSpec((tm, hd), row),
            pl.BlockSpec((tm, LANES), row),
            pl.BlockSpec((tm, hd), row),
        ],
        out_shape=[
            jax.ShapeDtypeStruct((m, hd), BF16),
            jax.ShapeDtypeStruct((m, hd), BF16),
            jax.ShapeDtypeStruct((m, hd), BF16),
            jax.ShapeDtypeStruct((m, LANES), BF16),
            jax.ShapeDtypeStruct((m, hd), BF16),
        ],
        compiler_params=_params(("parallel",)),
    )(x, g, wd, gq, gkv, gkr, wuq, gqn, gqr, wukv, gkn, cos, sinn, sinp)


def _rope_tables(seq):
    t = jnp.arange(seq)
    row = (t // GRID_W).astype(F32)
    col = (t % GRID_W).astype(F32)
    n_pairs = MLA_ROPE // 4
    inv = ROPE_THETA ** (-jnp.arange(n_pairs, dtype=F32) / n_pairs)
    ang = jnp.concatenate([row[:, None] * inv, col[:, None] * inv], axis=-1)
    c, s = jnp.cos(ang), jnp.sin(ang)
    z = jnp.zeros_like(c)
    cos = jnp.concatenate([c, c, z, z], axis=-1)
    sinn = jnp.concatenate([-s, z, z, z], axis=-1)
    sinp = jnp.concatenate([z, s, z, z], axis=-1)
    return cos, sinn, sinp


def _flash_kernel(qn_ref, qr_ref, kn_ref, kr_ref, v_ref, *rest):
    o_ref, m_scr, l_scr, acc_scr = rest[-4:]
    kk = pl.program_id(3)

    @pl.when(kk == 0)
    def _():
        m_scr[...] = jnp.full_like(m_scr, -jnp.inf)
        l_scr[...] = jnp.zeros_like(l_scr)
        acc_scr[...] = jnp.zeros_like(acc_scr)

    q = jnp.concatenate([qn_ref[...], qr_ref[...]], axis=-1)
    k = jnp.concatenate([kn_ref[...], kr_ref[...]], axis=-1)
    s = lax.dot_general(q, k, (((1,), (1,)), ((), ())), preferred_element_type=F32)
    m_prev = m_scr[...]
    m_new = jnp.maximum(m_prev, jnp.max(s, axis=-1, keepdims=True))
    alpha = jnp.exp(m_prev - m_new)
    p = jnp.exp(s - m_new)
    l_scr[...] = alpha * l_scr[...] + jnp.sum(p, axis=-1, keepdims=True)
    acc_scr[...] = alpha * acc_scr[...] + jnp.dot(p.astype(BF16), v_ref[...], preferred_element_type=F32)
    m_scr[...] = m_new

    @pl.when(kk == pl.num_programs(3) - 1)
    def _():
        o_ref[...] = (acc_scr[...] / l_scr[...]).astype(o_ref.dtype)


def _flash(qn, qr, kn, kr, v, out_prev, *, seq, n_batch, row_off, tq, tk):
    m = qn.shape[0]
    h = MLA_HEADS
    nq, nk = seq // tq, seq // tk
    qoff, koff = row_off // tq, row_off // tk
    qmap = lambda b, hh, i, j: (qoff + b * nq + i, hh)
    kmap = lambda b, hh, i, j: (koff + b * nk + j, hh)
    in_specs = [
        pl.BlockSpec((tq, LANES), qmap),
        pl.BlockSpec((tq, LANES), qmap),
        pl.BlockSpec((tk, LANES), kmap),
        pl.BlockSpec((tk, LANES), lambda b, hh, i, j: (koff + b * nk + j, 0)),
        pl.BlockSpec((tk, LANES), kmap),
    ]
    args = [qn, qr, kn, kr, v]
    aliases = {}
    if out_prev is not None:
        in_specs.append(pl.BlockSpec(memory_space=pl.ANY))
        args.append(out_prev)
        aliases = {5: 0}
    return pl.pallas_call(
        _flash_kernel,
        grid=(n_batch, h, nq, nk),
        in_specs=in_specs,
        out_specs=pl.BlockSpec((tq, LANES), qmap),
        out_shape=jax.ShapeDtypeStruct((m, h * MLA_V), BF16),
        scratch_shapes=[pltpu.VMEM((tq, 1), F32), pltpu.VMEM((tq, 1), F32), pltpu.VMEM((tq, MLA_V), F32)],
        input_output_aliases=aliases,
        compiler_params=_params(("parallel", "parallel", "parallel", "arbitrary")),
    )(*args)


def _router_kernel(x_ref, g_ref, wr_ref, idx_ref, gate_ref):
    h = _rms(x_ref[...], g_ref[...])
    logits = jnp.dot(h, wr_ref[...], precision=lax.Precision.HIGHEST, preferred_element_type=F32)
    lane = lax.broadcasted_iota(jnp.int32, logits.shape, 1)
    logits = jnp.where(lane < N_EXPERTS, logits, -jnp.inf)
    m1 = jnp.max(logits, axis=-1, keepdims=True)
    i1 = jnp.min(jnp.where(logits == m1, lane, LANES), axis=-1, keepdims=True)
    rest = jnp.where(lane == i1, -jnp.inf, logits)
    m2 = jnp.max(rest, axis=-1, keepdims=True)
    i2 = jnp.min(jnp.where(rest == m2, lane, LANES), axis=-1, keepdims=True)
    e = jnp.exp(m2 - m1)
    g1 = 1.0 / (1.0 + e)
    g2 = e / (1.0 + e)
    idx_ref[...] = jnp.where(lane == 0, i1, jnp.where(lane == 1, i2, 0))
    gate_ref[...] = jnp.where(lane == 0, g1, jnp.where(lane == 1, g2, 0.0))


def _router(x, g, wr, *, tm):
    m, d = x.shape
    return pl.pallas_call(
        _router_kernel,
        grid=(m // tm,),
        in_specs=[
            pl.BlockSpec((tm, d), lambda i: (i, 0)),
            pl.BlockSpec((1, d), lambda i: (0, 0)),
            pl.BlockSpec((d, LANES), lambda i: (0, 0)),
        ],
        out_specs=[pl.BlockSpec((tm, LANES), lambda i: (i, 0)), pl.BlockSpec((tm, LANES), lambda i: (i, 0))],
        out_shape=[jax.ShapeDtypeStruct((m, LANES), jnp.int32), jax.ShapeDtypeStruct((m, LANES), F32)],
        compiler_params=_params(("parallel",)),
    )(x, g, wr)


def _row_copy(src_hbm, row, dst_ref, j, sem):
    return pltpu.make_async_copy(src_hbm.at[pl.ds(row, 1)], dst_ref.at[pl.ds(j, 1)], sem)


def _gather_kernel(src_ref, x_hbm, o_ref, sem):
    r = o_ref.shape[0]

    def start(j, c):
        _row_copy(x_hbm, src_ref[0, 0, j], o_ref, j, sem).start()
        return c

    def wait(j, c):
        _row_copy(x_hbm, 0, o_ref, j, sem).wait()
        return c

    lax.fori_loop(0, r, start, 0)
    lax.fori_loop(0, r, wait, 0)


def _gather_rows(src, x, *, r):
    p = src.shape[0]
    d = x.shape[1]
    return pl.pallas_call(
        _gather_kernel,
        grid=(p // r,),
        in_specs=[
            pl.BlockSpec((1, 1, r), lambda i: (i, 0, 0), memory_space=pltpu.SMEM),
            pl.BlockSpec(memory_space=pl.ANY),
        ],
        out_specs=pl.BlockSpec((r, d), lambda i: (i, 0)),
        out_shape=jax.ShapeDtypeStruct((p, d), x.dtype),
        scratch_shapes=[pltpu.SemaphoreType.DMA(())],
        compiler_params=_params(("arbitrary",)),
    )(src.reshape(p // r, 1, r), x)


def _combine_kernel(pos_ref, x_ref, gate_ref, y_hbm, o_ref, buf, sem):
    r = o_ref.shape[0]

    def start(j, c):
        for k in range(TOP_K):
            _row_copy(y_hbm, pos_ref[0, 0, TOP_K * j + k], buf.at[k], j, sem).start()
        return c

    def wait(j, c):
        for k in range(TOP_K):
            _row_copy(y_hbm, 0, buf.at[k], j, sem).wait()
        return c

    lax.fori_loop(0, r, start, 0)
    lax.fori_loop(0, r, wait, 0)
    gate = gate_ref[...]
    moe = gate[:, 0:1] * buf[0] + gate[:, 1:2] * buf[1]
    o_ref[...] = x_ref[...] + moe


def _combine(pos, x, gates, y, *, r):
    m, d = x.shape
    return pl.pallas_call(
        _combine_kernel,
        grid=(m // r,),
        in_specs=[
            pl.BlockSpec((1, 1, TOP_K * r), lambda i: (i, 0, 0), memory_space=pltpu.SMEM),
            pl.BlockSpec((r, d), lambda i: (i, 0)),
            pl.BlockSpec((r, LANES), lambda i: (i, 0)),
            pl.BlockSpec(memory_space=pl.ANY),
        ],
        out_specs=pl.BlockSpec((r, d), lambda i: (i, 0)),
        out_shape=jax.ShapeDtypeStruct((m, d), F32),
        scratch_shapes=[pltpu.VMEM((TOP_K, r, d), F32), pltpu.SemaphoreType.DMA(())],
        compiler_params=_params(("arbitrary",)),
    )(pos.reshape(m // r, 1, TOP_K * r), x, gates, y)


def _route_plan(idx, n_tokens, tm):
    n_pairs = n_tokens * TOP_K
    p_rows = n_pairs + N_EXPERTS * tm
    e_flat = idx[:, :TOP_K].reshape(n_pairs)
    onehot = (e_flat[:, None] == jnp.arange(N_EXPERTS)[None, :]).astype(jnp.int32)
    csum = jnp.cumsum(onehot, axis=0)
    counts = csum[-1]
    rank = jnp.sum((csum - onehot) * onehot, axis=1)
    padded = ((counts + tm - 1) // tm) * tm
    pad_end = jnp.cumsum(padded)
    pad_start = pad_end - padded
    cnt_start = jnp.cumsum(counts) - counts
    pos = pad_start[e_flat] + rank
    order = jnp.argsort(e_flat, stable=True)
    rows = jnp.arange(p_rows)
    row_e = jnp.minimum(jnp.searchsorted(pad_end, rows, side="right"), N_EXPERTS - 1)
    within = rows - pad_start[row_e]
    valid = within < counts[row_e]
    sorted_i = jnp.clip(cnt_start[row_e] + within, 0, n_pairs - 1)
    src = jnp.where(valid, order[sorted_i] // TOP_K, 0).astype(jnp.int32)
    tile_expert = row_e[::tm].astype(jnp.int32)
    n_used = (pad_end[-1:] // tm).astype(jnp.int32)
    return pos.astype(jnp.int32), src, tile_expert, n_used


def _trunk(x, seqs, mix_norm, ffn_norm, na_w_qkv, na_q_gain, na_k_gain, na_rpb, na_w_o,
           mla_w_dqkv, mla_q_lora_gain, mla_kv_lora_gain, mla_w_uq, mla_w_ukv,
           mla_qn_gain, mla_qr_gain, mla_kn_gain, mla_kr_gain, mla_w_o,
           ffn_w_gate, ffn_w_up, ffn_w_down, moe_w_router, moe_w_gate, moe_w_up, moe_w_down):
    m = x.shape[0]
    row2 = lambda a: a.reshape(1, -1).astype(F32)
    na_dim = NA_HEADS * NA_HEAD_DIM

    head_gain = jnp.concatenate([
        jnp.tile(na_q_gain[0] * NA_HEAD_DIM ** -0.5, NA_HEADS),
        jnp.tile(na_k_gain[0], NA_HEADS),
        jnp.ones((na_dim,), F32)]).reshape(1, -1)
    qkv = _norm_matmul(x, row2(mix_norm[0]), na_w_qkv[0].astype(BF16), head_gain, 2 * na_dim,
                       tm=min(1024, m), tn=512)
    bias = _na_bias_table(na_rpb[0])
    attn = None
    off = 0
    for seq, nb in seqs:
        attn = _na_attention(qkv, bias, attn, seq=seq, n_batch=nb, row_block_off=off // seq)
        off += seq * nb
    x = _matmul_res(attn, na_w_o[0].astype(BF16), x, tm=min(512, m))

    x = _ffn(x, row2(ffn_norm[0]), ffn_w_gate[0].astype(BF16), ffn_w_up[0].astype(BF16),
             ffn_w_down[0].astype(BF16), tm=min(512, m), tf=512)

    scale = (MLA_NOPE + MLA_ROPE) ** -0.5
    half = MLA_ROPE // 2
    lat_w = MLA_Q_LORA + MLA_KV_LORA
    wd = jnp.pad(mla_w_dqkv[0], ((0, 0), (0, ROPE_PAD - MLA_ROPE))).astype(BF16)
    wuq = mla_w_uq[0].reshape(MLA_Q_LORA, MLA_HEADS, MLA_NOPE + MLA_ROPE)
    wuq = jnp.concatenate([
        wuq[:, :, :MLA_NOPE].reshape(MLA_Q_LORA, -1),
        jnp.pad(wuq[:, :, MLA_NOPE:], ((0, 0), (0, 0), (0, ROPE_PAD - MLA_ROPE))).reshape(MLA_Q_LORA, -1),
    ], axis=1).astype(BF16)
    wukv = mla_w_ukv[0].reshape(MLA_KV_LORA, MLA_HEADS, MLA_NOPE + MLA_V)
    wukv = jnp.concatenate([wukv[:, :, :MLA_NOPE].reshape(MLA_KV_LORA, -1),
                            wukv[:, :, MLA_NOPE:].reshape(MLA_KV_LORA, -1)], axis=1).astype(BF16)
    pad_gain = lambda gg: jnp.pad(gg, (0, ROPE_PAD - MLA_ROPE)).reshape(1, -1).astype(F32)
    max_seq = max(s for s, _ in seqs)
    cos, sinn, sinp = _rope_tables(max_seq)
    tm_p = min(256, m)
    bounds = []
    off = 0
    for seq, nb in seqs:
        bounds.append((off // tm_p, seq // tm_p))
        off += seq * nb

    def table_block(i):
        blk = i
        for start, per in bounds:
            blk = jnp.where(i >= start, (i - start) % per, blk)
        return blk

    qn, qr, kn, kr, v = _mla_proj(
        x, row2(mix_norm[1]), wd, row2(mla_q_lora_gain[0]), row2(mla_kv_lora_gain[0]), pad_gain(mla_kr_gain[0]),
        wuq, row2(mla_qn_gain[0] * scale), pad_gain(mla_qr_gain[0] * scale), wukv, row2(mla_kn_gain[0]),
        cos, sinn, sinp, table_block, tm=tm_p)
    attn = None
    off = 0
    for seq, nb in seqs:
        attn = _flash(qn, qr, kn, kr, v, attn, seq=seq, n_batch=nb, row_off=off,
                      tq=min(512, seq), tk=min(1024, seq))
        off += seq * nb
    x = _matmul_res(attn, mla_w_o[0].astype(BF16), x, tm=min(512, m))

    wr = jnp.pad(moe_w_router[0], ((0, 0), (0, LANES - N_EXPERTS))).astype(F32)
    idx, gates = _router(x, row2(ffn_norm[1]), wr, tm=min(512, m))
    tm_e = min(512, m)
    pos, src, tile_expert, n_used = _route_plan(idx, m, tm_e)
    xs = _gather_rows(src, x, r=min(256, m))
    ys = _expert_ffn(tile_expert, n_used, xs, row2(ffn_norm[1]), moe_w_gate[0].astype(BF16),
                     moe_w_up[0].astype(BF16), moe_w_down[0].astype(BF16), tm=tm_e, tf=512)
    return _combine(pos, x, gates, ys, r=min(256, m))


def kernel(x_prompt, x_sample, mix_norm, ffn_norm, na_w_qkv, na_q_gain, na_k_gain, na_rpb, na_w_o, mla_w_dqkv, mla_q_lora_gain, mla_kv_lora_gain, mla_w_uq, mla_w_ukv, mla_qn_gain, mla_qr_gain, mla_kn_gain, mla_kr_gain, mla_w_o, ffn_w_gate, ffn_w_up, ffn_w_down, moe_w_router, moe_w_gate, moe_w_up, moe_w_down):
    bp, sp, d = x_prompt.shape
    bs, ss, _ = x_sample.shape
    x = jnp.concatenate([x_prompt.reshape(bp * sp, d), x_sample.reshape(bs * ss, d)], axis=0)
    y = _trunk(x, [(sp, bp), (ss, bs)], mix_norm, ffn_norm, na_w_qkv, na_q_gain, na_k_gain, na_rpb, na_w_o,
               mla_w_dqkv, mla_q_lora_gain, mla_kv_lora_gain, mla_w_uq, mla_w_ukv,
               mla_qn_gain, mla_qr_gain, mla_kn_gain, mla_kr_gain, mla_w_o,
               ffn_w_gate, ffn_w_up, ffn_w_down, moe_w_router, moe_w_gate, moe_w_up, moe_w_down)
    return (y[:bp * sp].reshape(bp, sp, d), y[bp * sp:].reshape(bs, ss, d))
```

```python
import functools

import jax
import jax.numpy as jnp
from jax import lax
from jax.experimental import pallas as pl
from jax.experimental.pallas import tpu as pltpu

F32 = jnp.float32
BF16 = jnp.bfloat16

D_MODEL = 2048
GRID_W = 64
NA_HEADS = 16
NA_HEAD_DIM = 128
NA_WIN_H = 8
NA_WIN_W = 16
MLA_HEADS = 16
MLA_Q_LORA = 512
MLA_KV_LORA = 512
MLA_NOPE = 128
MLA_ROPE = 64
MLA_V = 128
ROPE_THETA = 10000.0
D_FF = 5632
N_EXPERTS = 8
TOP_K = 2
EPS = 1e-6
NEG_INF = -1e30

LANES = 128
VMEM_LIMIT = 56 * 1024 * 1024


def _params(sem):
    return pltpu.CompilerParams(dimension_semantics=sem, vmem_limit_bytes=VMEM_LIMIT)


def _rms(x, g):
    return x * lax.rsqrt(jnp.mean(x * x, axis=-1, keepdims=True) + EPS) * g


def _select_rows(i, refs, bounds, fn):
    for ref, (lo, hi) in zip(refs, bounds):
        @pl.when((i >= lo) & (i < hi))
        def _(ref=ref):
            fn(ref)


def _split_specs(xs, tm, ncols):
    specs, bounds, lo = [], [], 0
    for x in xs:
        n = x.shape[0] // tm
        assert n * tm == x.shape[0]
        specs.append(pl.BlockSpec(
            (tm, ncols), functools.partial(lambda i, *_, lo, n: (jnp.clip(i - lo, 0, n - 1), 0), lo=lo, n=n)))
        bounds.append((lo, lo + n))
        lo += n
    return specs, tuple(bounds), lo


def _norm_matmul_kernel(*refs, n_norm_tiles, bounds):
    nx = len(bounds)
    x_refs = refs[:nx]
    g_ref, w_ref, hg_ref, o_ref, h_scr = refs[nx:]
    j = pl.program_id(1)

    @pl.when(j == 0)
    def _():
        def fill(x_ref):
            h_scr[...] = _rms(x_ref[...], g_ref[...]).astype(BF16)
        _select_rows(pl.program_id(0), x_refs, bounds, fill)

    acc = jnp.dot(h_scr[...], w_ref[...], preferred_element_type=F32)
    tn = acc.shape[1]

    @pl.when(j < n_norm_tiles)
    def _():
        for c in range(tn // LANES):
            sl = slice(c * LANES, (c + 1) * LANES)
            o_ref[:, sl] = _rms(acc[:, sl], hg_ref[:, sl]).astype(o_ref.dtype)

    @pl.when(j >= n_norm_tiles)
    def _():
        o_ref[...] = acc.astype(o_ref.dtype)


def _norm_matmul(xs, g, w, head_gain, n_norm_cols, *, tm, tn):
    d = xs[0].shape[1]
    n = w.shape[1]
    x_specs, bounds, n_row_tiles = _split_specs(xs, tm, d)
    return pl.pallas_call(
        functools.partial(_norm_matmul_kernel, n_norm_tiles=n_norm_cols // tn, bounds=bounds),
        grid=(n_row_tiles, n // tn),
        in_specs=x_specs + [
            pl.BlockSpec((1, d), lambda i, j: (0, 0)),
            pl.BlockSpec((d, tn), lambda i, j: (0, j)),
            pl.BlockSpec((1, tn), lambda i, j: (0, j)),
        ],
        out_specs=pl.BlockSpec((tm, tn), lambda i, j: (i, j)),
        out_shape=jax.ShapeDtypeStruct((n_row_tiles * tm, n), BF16),
        scratch_shapes=[pltpu.VMEM((tm, d), BF16)],
        compiler_params=_params(("parallel", "arbitrary")),
    )(*xs, g, w, head_gain)


NA_QROWS = 4
NA_SPAN = NA_WIN_H + NA_QROWS


def _na_kernel(q_ref, k_ref, v_ref, b_ref, *rest, rows):
    o_ref = rest[-1]
    nq = NA_QROWS * GRID_W
    nk = NA_SPAN * GRID_W
    n_blk = rows // NA_QROWS

    def body(blk, carry):
        r0 = blk * NA_QROWS
        span0 = jnp.clip(r0 - NA_WIN_H // 2, 0, rows - NA_SPAN)
        case = jnp.where(blk == 0, 0, jnp.where(blk == n_blk - 1, 2, 1))
        qoff = pl.multiple_of(r0 * GRID_W, nq)
        koff = pl.multiple_of(span0 * GRID_W, GRID_W)
        q = q_ref[pl.ds(qoff, nq), :]
        k = k_ref[pl.ds(koff, nk), :]
        v = v_ref[pl.ds(koff, nk), :]
        s = lax.dot_general(q, k, (((1,), (1,)), ((), ())), preferred_element_type=F32)
        s = s + b_ref[0, case]
        m = jnp.max(s, axis=-1, keepdims=True)
        p = jnp.exp(s - m)
        l = jnp.sum(p, axis=-1, keepdims=True)
        o = jnp.dot(p.astype(BF16), v, preferred_element_type=F32) / l
        o_ref[pl.ds(qoff, nq), :] = o.astype(o_ref.dtype)
        return carry

    lax.fori_loop(0, n_blk, body, 0, unroll=2)


def _na_bias_table(rpb):
    qc = jnp.arange(GRID_W)[:, None]
    kc = jnp.arange(GRID_W)[None, :]
    ws = jnp.clip(qc - NA_WIN_W // 2, 0, GRID_W - NA_WIN_W)
    col_ok = (kc >= ws) & (kc < ws + NA_WIN_W)
    dc = jnp.clip(kc - qc + NA_WIN_W - 1, 0, 2 * NA_WIN_W - 2)
    i = jnp.arange(NA_QROWS)
    half = NA_WIN_H // 2
    w_off = jnp.stack([jnp.zeros_like(i), i, jnp.full_like(i, NA_SPAN - NA_WIN_H)])
    c_off = jnp.stack([i, jnp.full_like(i, half), half + i])
    j = jnp.arange(NA_SPAN)[None, None, :] - w_off[:, :, None]
    row_ok = (j >= 0) & (j < NA_WIN_H)
    dr = jnp.clip(j - c_off[:, :, None] + NA_WIN_H - 1, 0, 2 * NA_WIN_H - 2)
    t = rpb.astype(F32)[:, dr][..., dc]
    ok = row_ok[None, :, :, :, None, None] & col_ok[None, None, None, None]
    t = jnp.where(ok, t, NEG_INF)
    t = t.transpose(0, 1, 2, 4, 3, 5)
    return t.reshape(rpb.shape[0], 3, NA_QROWS * GRID_W, NA_SPAN * GRID_W)


def _na_attention(qkv, bias, out_prev, *, seq, n_batch, row_block_off):
    m = qkv.shape[0]
    rows = seq // GRID_W
    assert rows >= NA_SPAN + NA_QROWS and rows % NA_QROWS == 0
    h = NA_HEADS

    def spec(sec):
        return pl.BlockSpec((seq, NA_HEAD_DIM), lambda hh, b: (row_block_off + b, sec * h + hh))

    in_specs = [spec(0), spec(1), spec(2),
                pl.BlockSpec((1,) + bias.shape[1:], lambda hh, b: (hh, 0, 0, 0))]
    args = [qkv, qkv, qkv, bias]
    aliases = {}
    if out_prev is not None:
        in_specs.append(pl.BlockSpec(memory_space=pl.ANY))
        args.append(out_prev)
        aliases = {4: 0}
    return pl.pallas_call(
        functools.partial(_na_kernel, rows=rows),
        grid=(h, n_batch),
        in_specs=in_specs,
        out_specs=pl.BlockSpec((seq, NA_HEAD_DIM), lambda hh, b: (row_block_off + b, hh)),
        out_shape=jax.ShapeDtypeStruct((m, h * NA_HEAD_DIM), BF16),
        input_output_aliases=aliases,
        compiler_params=_params(("parallel", "parallel")),
    )(*args)


def _matmul_res_kernel(a_ref, w_ref, *refs, bounds):
    r_refs, o_ref = refs[:-1], refs[-1]
    y = jnp.dot(a_ref[...], w_ref[...], preferred_element_type=F32)

    def add(r_ref):
        o_ref[...] = r_ref[...] + y
    _select_rows(pl.program_id(0), r_refs, bounds, add)


def _matmul_res(a, w, res_list, *, tm):
    m, k = a.shape
    n = w.shape[1]
    r_specs, bounds, n_row_tiles = _split_specs(res_list, tm, n)
    assert n_row_tiles * tm == m
    return pl.pallas_call(
        functools.partial(_matmul_res_kernel, bounds=bounds),
        grid=(n_row_tiles,),
        in_specs=[
            pl.BlockSpec((tm, k), lambda i: (i, 0)),
            pl.BlockSpec((k, n), lambda i: (0, 0)),
        ] + r_specs,
        out_specs=pl.BlockSpec((tm, n), lambda i: (i, 0)),
        out_shape=jax.ShapeDtypeStruct((m, n), F32),
        compiler_params=_params(("parallel",)),
    )(a, w, *res_list)


def _swiglu_step(h, wg, wu, wd):
    a = jnp.dot(h, wg, preferred_element_type=F32)
    b = jnp.dot(h, wu, preferred_element_type=F32)
    t = (a / (1.0 + jnp.exp(-a)) * b).astype(BF16)
    return jnp.dot(t, wd, preferred_element_type=F32)


def _ffn_kernel(x_ref, g_ref, wg_ref, wu_ref, wd_ref, o_ref, h_scr):
    f = pl.program_id(1)

    @pl.when(f == 0)
    def _():
        x = x_ref[...]
        h_scr[...] = _rms(x, g_ref[...]).astype(BF16)
        o_ref[...] = x

    o_ref[...] += _swiglu_step(h_scr[...], wg_ref[...], wu_ref[...], wd_ref[...])


def _ffn(x, g, wg, wu, wd, *, tm, tf):
    m, d = x.shape
    f = wg.shape[1]
    return pl.pallas_call(
        _ffn_kernel,
        grid=(m // tm, f // tf),
        in_specs=[
            pl.BlockSpec((tm, d), lambda i, j: (i, 0)),
            pl.BlockSpec((1, d), lambda i, j: (0, 0)),
            pl.BlockSpec((d, tf), lambda i, j: (0, j)),
            pl.BlockSpec((d, tf), lambda i, j: (0, j)),
            pl.BlockSpec((tf, d), lambda i, j: (j, 0)),
        ],
        out_specs=pl.BlockSpec((tm, d), lambda i, j: (i, 0)),
        out_shape=jax.ShapeDtypeStruct((m, d), F32),
        scratch_shapes=[pltpu.VMEM((tm, d), BF16)],
        compiler_params=_params(("parallel", "arbitrary")),
    )(x, g, wg, wu, wd)


def _expert_ffn_kernel(te_ref, nu_ref, x_ref, g_ref, wg_ref, wu_ref, wd_ref, o_ref, h_scr):
    i = pl.program_id(0)
    f = pl.program_id(1)

    @pl.when(i < nu_ref[0])
    def _():
        @pl.when(f == 0)
        def _():
            h_scr[...] = _rms(x_ref[...], g_ref[...]).astype(BF16)
            o_ref[...] = jnp.zeros_like(o_ref)

        o_ref[...] += _swiglu_step(h_scr[...], wg_ref[0], wu_ref[0], wd_ref[0])


def _expert_ffn(tile_expert, n_used, xs, g, wg, wu, wd, *, tm, tf):
    p, d = xs.shape
    nf = wg.shape[2] // tf

    def row_map(i, j, te, nu):
        return (jnp.minimum(i, nu[0] - 1), 0)

    def wcol_map(i, j, te, nu):
        used = i < nu[0]
        return (te[jnp.minimum(i, nu[0] - 1)], 0, jnp.where(used, j, nf - 1))

    def wrow_map(i, j, te, nu):
        used = i < nu[0]
        return (te[jnp.minimum(i, nu[0] - 1)], jnp.where(used, j, nf - 1), 0)

    grid_spec = pltpu.PrefetchScalarGridSpec(
        num_scalar_prefetch=2,
        grid=(p // tm, nf),
        in_specs=[
            pl.BlockSpec((tm, d), row_map),
            pl.BlockSpec((1, d), lambda i, j, te, nu: (0, 0)),
            pl.BlockSpec((1, d, tf), wcol_map),
            pl.BlockSpec((1, d, tf), wcol_map),
            pl.BlockSpec((1, tf, d), wrow_map),
        ],
        out_specs=pl.BlockSpec((tm, d), row_map),
        scratch_shapes=[pltpu.VMEM((tm, d), BF16)],
    )
    return pl.pallas_call(
        _expert_ffn_kernel,
        grid_spec=grid_spec,
        out_shape=jax.ShapeDtypeStruct((p, d), F32),
        compiler_params=_params(("arbitrary", "arbitrary")),
    )(tile_expert, n_used, xs, g, wg, wu, wd)


ROPE_PAD = LANES


def _mla_proj_kernel(x_ref, g_ref, wd_ref, gq_ref, gkv_ref, gkr_ref, wuq_ref, gqn_ref, gqr_ref,
                     wukv_ref, gkn_ref, cos_ref, sinn_ref, sinp_ref,
                     qn_ref, qr_ref, kn_ref, kr_ref, v_ref):
    half = MLA_ROPE // 2
    cos = cos_ref[...]
    sinn = sinn_ref[...]
    sinp = sinp_ref[...]

    def rope(piece, gain):
        ms = jnp.sum(piece * piece, axis=-1, keepdims=True) * (1.0 / MLA_ROPE)
        y = piece * lax.rsqrt(ms + EPS) * gain
        return (y * cos + pltpu.roll(y, ROPE_PAD - half, 1) * sinn + pltpu.roll(y, half, 1) * sinp)

    h = _rms(x_ref[...], g_ref[...]).astype(BF16)
    lat = jnp.dot(h, wd_ref[...], preferred_element_type=F32)
    cq = _rms(lat[:, :MLA_Q_LORA], gq_ref[...]).astype(BF16)
    ckv = _rms(lat[:, MLA_Q_LORA:MLA_Q_LORA + MLA_KV_LORA], gkv_ref[...]).astype(BF16)
    kr_ref[...] = rope(lat[:, MLA_Q_LORA + MLA_KV_LORA:], gkr_ref[...]).astype(BF16)

    hd = MLA_HEADS * LANES
    chunk = 4 * LANES
    for c in range(hd // chunk):
        base = c * chunk
        qn = jnp.dot(cq, wuq_ref[:, base:base + chunk], preferred_element_type=F32)
        qr = jnp.dot(cq, wuq_ref[:, hd + base:hd + base + chunk], preferred_element_type=F32)
        kn = jnp.dot(ckv, wukv_ref[:, base:base + chunk], preferred_element_type=F32)
        v = jnp.dot(ckv, wukv_ref[:, hd + base:hd + base + chunk], preferred_element_type=F32)
        v_ref[:, base:base + chunk] = v.astype(BF16)
        for hh in range(chunk // LANES):
            sl = slice(hh * LANES, (hh + 1) * LANES)
            osl = slice(base + hh * LANES, base + (hh + 1) * LANES)
            qn_ref[:, osl] = _rms(qn[:, sl], gqn_ref[...]).astype(BF16)
            kn_ref[:, osl] = _rms(kn[:, sl], gkn_ref[...]).astype(BF16)
            qr_ref[:, osl] = rope(qr[:, sl], gqr_ref[...]).astype(BF16)


def _mla_proj(x, g, wd, gq, gkv, gkr, wuq, gqn, gqr, wukv, gkn, cos, sinn, sinp, table_block, *, tm):
    m, d = x.shape
    hd = MLA_HEADS * LANES
    const = lambda i: (0, 0)
    row = lambda i: (i, 0)
    tab = lambda i: (table_block(i), 0)
    return pl.pallas_call(
        _mla_proj_kernel,
        grid=(m // tm,),
        in_specs=[
            pl.BlockSpec((tm, d), row),
            pl.BlockSpec((1, d), const),
            pl.BlockSpec(wd.shape, const),
            pl.BlockSpec((1, MLA_Q_LORA), const),
            pl.BlockSpec((1, MLA_KV_LORA), const),
            pl.BlockSpec((1, LANES), const),
            pl.BlockSpec(wuq.shape, const),
            pl.BlockSpec((1, LANES), const),
            pl.BlockSpec((1, LANES), const),
            pl.BlockSpec(wukv.shape, const),
            pl.BlockSpec((1, LANES), const),
            pl.BlockSpec((tm, LANES), tab),
            pl.BlockSpec((tm, LANES), tab),
            pl.BlockSpec((tm, LANES), tab),
        ],
        out_specs=[
            pl.BlockSpec((tm, hd), row),
            pl.BlockSpec((tm, hd), row),
            pl.BlockSpec((tm, hd), row),
            pl.BlockSpec((tm, LANES), row),
            pl.BlockSpec((tm, hd), row),
        ],
        out_shape=[
            jax.ShapeDtypeStruct((m, hd), BF16),
            jax.ShapeDtypeStruct((m, hd), BF16),
            jax.ShapeDtypeStruct((m, hd), BF16),
            jax.ShapeDtypeStruct((m, LANES), BF16),
            jax.ShapeDtypeStruct((m, hd), BF16),
        ],
        compiler_params=_params(("parallel",)),
    )(x, g, wd, gq, gkv, gkr, wuq, gqn, gqr, wukv, gkn, cos, sinn, sinp)


def _rope_tables(seq):
    t = jnp.arange(seq)
    row = (t // GRID_W).astype(F32)
    col = (t % GRID_W).astype(F32)
    n_pairs = MLA_ROPE // 4
    inv = ROPE_THETA ** (-jnp.arange(n_pairs, dtype=F32) / n_pairs)
    ang = jnp.concatenate([row[:, None] * inv, col[:, None] * inv], axis=-1)
    c, s = jnp.cos(ang), jnp.sin(ang)
    z = jnp.zeros_like(c)
    cos = jnp.concatenate([c, c, z, z], axis=-1)
    sinn = jnp.concatenate([-s, z, z, z], axis=-1)
    sinp = jnp.concatenate([z, s, z, z], axis=-1)
    return cos, sinn, sinp


FLASH_TQ = 512
FLASH_TK = 2048
FLASH_HEADS = 2
FLASH_SUB_Q = 256


def _flash_kernel(qn_ref, qr_ref, kn_ref, kr_ref, v_ref, *rest, hb, sub_q, nk):
    if nk == 1:
        o_ref = rest[-1]
    else:
        o_ref, m_scr, l_scr, acc_scr = rest[-4:]
        kk = pl.program_id(3)

        @pl.when(kk == 0)
        def _():
            m_scr[...] = jnp.full_like(m_scr, -jnp.inf)
            l_scr[...] = jnp.zeros_like(l_scr)
            acc_scr[...] = jnp.zeros_like(acc_scr)

    tq = qn_ref.shape[0]
    kr = kr_ref[...]
    for h in range(hb):
        hs = slice(h * LANES, (h + 1) * LANES)
        k = jnp.concatenate([kn_ref[:, hs], kr], axis=-1)
        v = v_ref[:, hs]
        for qi in range(tq // sub_q):
            rs = slice(qi * sub_q, (qi + 1) * sub_q)
            q = jnp.concatenate([qn_ref[rs, hs], qr_ref[rs, hs]], axis=-1)
            s = lax.dot_general(q, k, (((1,), (1,)), ((), ())), preferred_element_type=F32)
            if nk == 1:
                m = jnp.max(s, axis=-1, keepdims=True)
                p = jnp.exp(s - m)
                l = jnp.sum(p, axis=-1, keepdims=True)
                o = jnp.dot(p.astype(BF16), v, preferred_element_type=F32) / l
                o_ref[rs, hs] = o.astype(o_ref.dtype)
            else:
                m_prev = m_scr[h, rs]
                m_new = jnp.maximum(m_prev, jnp.max(s, axis=-1, keepdims=True))
                alpha = jnp.exp(m_prev - m_new)
                p = jnp.exp(s - m_new)
                l_scr[h, rs] = alpha * l_scr[h, rs] + jnp.sum(p, axis=-1, keepdims=True)
                acc_scr[rs, hs] = alpha * acc_scr[rs, hs] + jnp.dot(p.astype(BF16), v, preferred_element_type=F32)
                m_scr[h, rs] = m_new

    if nk > 1:
        @pl.when(kk == nk - 1)
        def _():
            for h in range(hb):
                hs = slice(h * LANES, (h + 1) * LANES)
                o_ref[:, hs] = (acc_scr[:, hs] / l_scr[h]).astype(o_ref.dtype)


def _flash(qn, qr, kn, kr, v, out_prev, *, seq, n_batch, row_off, tq, tk, hb, sub_q):
    m = qn.shape[0]
    h = MLA_HEADS
    nq, nk = seq // tq, seq // tk
    qoff, koff = row_off // tq, row_off // tk
    qmap = lambda b, hh, i, j: (qoff + b * nq + i, hh)
    kmap = lambda b, hh, i, j: (koff + b * nk + j, hh)
    in_specs = [
        pl.BlockSpec((tq, hb * LANES), qmap),
        pl.BlockSpec((tq, hb * LANES), qmap),
        pl.BlockSpec((tk, hb * LANES), kmap),
        pl.BlockSpec((tk, LANES), lambda b, hh, i, j: (koff + b * nk + j, 0)),
        pl.BlockSpec((tk, hb * LANES), kmap),
    ]
    args = [qn, qr, kn, kr, v]
    aliases = {}
    if out_prev is not None:
        in_specs.append(pl.BlockSpec(memory_space=pl.ANY))
        args.append(out_prev)
        aliases = {5: 0}
    scratch = []
    if nk > 1:
        scratch = [pltpu.VMEM((hb, tq, 1), F32), pltpu.VMEM((hb, tq, 1), F32), pltpu.VMEM((tq, hb * MLA_V), F32)]
    return pl.pallas_call(
        functools.partial(_flash_kernel, hb=hb, sub_q=sub_q, nk=nk),
        grid=(n_batch, h // hb, nq, nk),
        in_specs=in_specs,
        out_specs=pl.BlockSpec((tq, hb * LANES), qmap),
        out_shape=jax.ShapeDtypeStruct((m, h * MLA_V), BF16),
        scratch_shapes=scratch,
        input_output_aliases=aliases,
        compiler_params=_params(("parallel", "parallel", "parallel", "arbitrary")),
    )(*args)


def _router_kernel(x_ref, g_ref, wr_ref, idx_ref, gate_ref):
    h = _rms(x_ref[...], g_ref[...])
    logits = jnp.dot(h, wr_ref[...], precision=lax.Precision.HIGHEST, preferred_element_type=F32)
    lane = lax.broadcasted_iota(jnp.int32, logits.shape, 1)
    logits = jnp.where(lane < N_EXPERTS, logits, -jnp.inf)
    m1 = jnp.max(logits, axis=-1, keepdims=True)
    i1 = jnp.min(jnp.where(logits == m1, lane, LANES), axis=-1, keepdims=True)
    rest = jnp.where(lane == i1, -jnp.inf, logits)
    m2 = jnp.max(rest, axis=-1, keepdims=True)
    i2 = jnp.min(jnp.where(rest == m2, lane, LANES), axis=-1, keepdims=True)
    e = jnp.exp(m2 - m1)
    g1 = 1.0 / (1.0 + e)
    g2 = e / (1.0 + e)
    idx_ref[...] = jnp.where(lane == 0, i1, jnp.where(lane == 1, i2, 0))
    gate_ref[...] = jnp.where(lane == 0, g1, jnp.where(lane == 1, g2, 0.0))


def _router(x, g, wr, *, tm):
    m, d = x.shape
    return pl.pallas_call(
        _router_kernel,
        grid=(m // tm,),
        in_specs=[
            pl.BlockSpec((tm, d), lambda i: (i, 0)),
            pl.BlockSpec((1, d), lambda i: (0, 0)),
            pl.BlockSpec((d, LANES), lambda i: (0, 0)),
        ],
        out_specs=[pl.BlockSpec((tm, LANES), lambda i: (i, 0)), pl.BlockSpec((tm, LANES), lambda i: (i, 0))],
        out_shape=[jax.ShapeDtypeStruct((m, LANES), jnp.int32), jax.ShapeDtypeStruct((m, LANES), F32)],
        compiler_params=_params(("parallel",)),
    )(x, g, wr)


def _scatter_kernel(pe_ref, pos_ref, x_ref, o_hbm, zero_scr, sem, zsem, *, tm):
    r = x_ref.shape[0]

    def zero_copy(e):
        end = pe_ref[e]
        return pltpu.make_async_copy(zero_scr, o_hbm.at[pl.ds(pl.multiple_of(end - tm, tm), tm)], zsem)

    def nonempty(e):
        return pe_ref[e] > (pe_ref[e - 1] if e else 0)

    @pl.when(pl.program_id(0) == 0)
    def _():
        zero_scr[...] = jnp.zeros_like(zero_scr)
        for e in range(N_EXPERTS):
            @pl.when(nonempty(e))
            def _(e=e):
                zero_copy(e).start()
        for e in range(N_EXPERTS):
            @pl.when(nonempty(e))
            def _(e=e):
                zero_copy(e).wait()

    def row_copy(j, row):
        return pltpu.make_async_copy(x_ref.at[pl.ds(j, 1)], o_hbm.at[pl.ds(row, 1)], sem)

    def start(j, c):
        for k in range(TOP_K):
            row_copy(j, pos_ref[0, 0, TOP_K * j + k]).start()
        return c

    def wait(j, c):
        for k in range(TOP_K):
            row_copy(j, 0).wait()
        return c

    lax.fori_loop(0, r, start, 0)
    lax.fori_loop(0, r, wait, 0)


def _scatter_rows(pad_end, pos, x, p_rows, *, r, tm):
    m, d = x.shape
    grid_spec = pltpu.PrefetchScalarGridSpec(
        num_scalar_prefetch=1,
        grid=(m // r,),
        in_specs=[
            pl.BlockSpec((1, 1, TOP_K * r), lambda i, pe: (i, 0, 0), memory_space=pltpu.SMEM),
            pl.BlockSpec((r, d), lambda i, pe: (i, 0)),
        ],
        out_specs=pl.BlockSpec(memory_space=pl.ANY),
        scratch_shapes=[pltpu.VMEM((tm, d), F32), pltpu.SemaphoreType.DMA(()), pltpu.SemaphoreType.DMA(())],
    )
    return pl.pallas_call(
        functools.partial(_scatter_kernel, tm=tm),
        grid_spec=grid_spec,
        out_shape=jax.ShapeDtypeStruct((p_rows, d), F32),
        compiler_params=_params(("arbitrary",)),
    )(pad_end, pos.reshape(m // r, 1, TOP_K * r), x)


def _combine_kernel(pos_ref, x_ref, gate_ref, y_hbm, *rest, bounds):
    o_refs = rest[:len(bounds)]
    buf, sem = rest[len(bounds):]
    r = x_ref.shape[0]

    def row_copy(j, k, row):
        return pltpu.make_async_copy(y_hbm.at[pl.ds(row, 1)], buf.at[k, pl.ds(j, 1)], sem)

    def start(j, c):
        for k in range(TOP_K):
            row_copy(j, k, pos_ref[0, 0, TOP_K * j + k]).start()
        return c

    def wait(j, c):
        for k in range(TOP_K):
            row_copy(j, k, 0).wait()
        return c

    lax.fori_loop(0, r, start, 0)
    lax.fori_loop(0, r, wait, 0)
    gate = gate_ref[...]
    out = x_ref[...] + (gate[:, 0:1] * buf[0] + gate[:, 1:2] * buf[1])

    def write(o_ref):
        o_ref[...] = out
    _select_rows(pl.program_id(0), o_refs, bounds, write)


def _combine(pos, x, gates, y, out_rows, *, r):
    m, d = x.shape
    out_shape = [jax.ShapeDtypeStruct((n, d), F32) for n in out_rows]
    o_specs, bounds, n_row_tiles = _split_specs(out_shape, r, d)
    assert n_row_tiles * r == m
    return pl.pallas_call(
        functools.partial(_combine_kernel, bounds=bounds),
        grid=(n_row_tiles,),
        in_specs=[
            pl.BlockSpec((1, 1, TOP_K * r), lambda i: (i, 0, 0), memory_space=pltpu.SMEM),
            pl.BlockSpec((r, d), lambda i: (i, 0)),
            pl.BlockSpec((r, LANES), lambda i: (i, 0)),
            pl.BlockSpec(memory_space=pl.ANY),
        ],
        out_specs=o_specs,
        out_shape=out_shape,
        scratch_shapes=[pltpu.VMEM((TOP_K, r, d), F32), pltpu.SemaphoreType.DMA(())],
        compiler_params=_params(("arbitrary",)),
    )(pos.reshape(m // r, 1, TOP_K * r), x, gates, y)


def _route_plan(idx, n_tokens, tm):
    n_pairs = n_tokens * TOP_K
    n_tiles = n_pairs // tm + N_EXPERTS
    e_flat = idx[:, :TOP_K].reshape(n_pairs)
    onehot = (e_flat[:, None] == jnp.arange(N_EXPERTS)[None, :]).astype(jnp.int32)
    csum = jnp.cumsum(onehot, axis=0)
    counts = csum[-1]
    padded = ((counts + tm - 1) // tm) * tm
    pad_end = jnp.cumsum(padded)
    pad_start = pad_end - padded
    pos = jnp.sum((csum - onehot + pad_start[None, :]) * onehot, axis=1)
    tile_start = jnp.arange(n_tiles) * tm
    tile_expert = jnp.minimum(jnp.sum(tile_start[:, None] >= pad_end[None, :], axis=1), N_EXPERTS - 1)
    n_used = pad_end[-1:] // tm
    return (pos.astype(jnp.int32), pad_end.astype(jnp.int32), tile_expert.astype(jnp.int32),
            n_used.astype(jnp.int32))


def _trunk(xs, seqs, mix_norm, ffn_norm, na_w_qkv, na_q_gain, na_k_gain, na_rpb, na_w_o,
           mla_w_dqkv, mla_q_lora_gain, mla_kv_lora_gain, mla_w_uq, mla_w_ukv,
           mla_qn_gain, mla_qr_gain, mla_kn_gain, mla_kr_gain, mla_w_o,
           ffn_w_gate, ffn_w_up, ffn_w_down, moe_w_router, moe_w_gate, moe_w_up, moe_w_down):
    m = sum(x.shape[0] for x in xs)
    row2 = lambda a: a.reshape(1, -1).astype(F32)
    na_dim = NA_HEADS * NA_HEAD_DIM

    head_gain = jnp.concatenate([
        jnp.tile(na_q_gain[0] * NA_HEAD_DIM ** -0.5, NA_HEADS),
        jnp.tile(na_k_gain[0], NA_HEADS),
        jnp.ones((na_dim,), F32)]).reshape(1, -1)
    qkv = _norm_matmul(xs, row2(mix_norm[0]), na_w_qkv[0].astype(BF16), head_gain, 2 * na_dim,
                       tm=min(1024, min(x.shape[0] for x in xs)), tn=512)
    bias = _na_bias_table(na_rpb[0])
    attn = None
    off = 0
    for seq, nb in seqs:
        attn = _na_attention(qkv, bias, attn, seq=seq, n_batch=nb, row_block_off=off // seq)
        off += seq * nb
    x = _matmul_res(attn, na_w_o[0].astype(BF16), xs, tm=min(512, m))

    x = _ffn(x, row2(ffn_norm[0]), ffn_w_gate[0].astype(BF16), ffn_w_up[0].astype(BF16),
             ffn_w_down[0].astype(BF16), tm=min(512, m), tf=512)

    scale = (MLA_NOPE + MLA_ROPE) ** -0.5
    half = MLA_ROPE // 2
    lat_w = MLA_Q_LORA + MLA_KV_LORA
    wd = jnp.pad(mla_w_dqkv[0], ((0, 0), (0, ROPE_PAD - MLA_ROPE))).astype(BF16)
    wuq = mla_w_uq[0].reshape(MLA_Q_LORA, MLA_HEADS, MLA_NOPE + MLA_ROPE)
    wuq = jnp.concatenate([
        wuq[:, :, :MLA_NOPE].reshape(MLA_Q_LORA, -1),
        jnp.pad(wuq[:, :, MLA_NOPE:], ((0, 0), (0, 0), (0, ROPE_PAD - MLA_ROPE))).reshape(MLA_Q_LORA, -1),
    ], axis=1).astype(BF16)
    wukv = mla_w_ukv[0].reshape(MLA_KV_LORA, MLA_HEADS, MLA_NOPE + MLA_V)
    wukv = jnp.concatenate([wukv[:, :, :MLA_NOPE].reshape(MLA_KV_LORA, -1),
                            wukv[:, :, MLA_NOPE:].reshape(MLA_KV_LORA, -1)], axis=1).astype(BF16)
    pad_gain = lambda gg: jnp.pad(gg, (0, ROPE_PAD - MLA_ROPE)).reshape(1, -1).astype(F32)
    max_seq = max(s for s, _ in seqs)
    cos, sinn, sinp = _rope_tables(max_seq)
    tm_p = min(256, m)
    bounds = []
    off = 0
    for seq, nb in seqs:
        bounds.append((off // tm_p, seq // tm_p))
        off += seq * nb

    def table_block(i):
        blk = i
        for start, per in bounds:
            blk = jnp.where(i >= start, (i - start) % per, blk)
        return blk

    qn, qr, kn, kr, v = _mla_proj(
        x, row2(mix_norm[1]), wd, row2(mla_q_lora_gain[0]), row2(mla_kv_lora_gain[0]), pad_gain(mla_kr_gain[0]),
        wuq, row2(mla_qn_gain[0] * scale), pad_gain(mla_qr_gain[0] * scale), wukv, row2(mla_kn_gain[0]),
        cos, sinn, sinp, table_block, tm=tm_p)
    attn = None
    off = 0
    for seq, nb in seqs:
        tq = min(FLASH_TQ, seq)
        attn = _flash(qn, qr, kn, kr, v, attn, seq=seq, n_batch=nb, row_off=off,
                      tq=tq, tk=min(FLASH_TK, seq), hb=FLASH_HEADS, sub_q=min(FLASH_SUB_Q, tq))
        off += seq * nb
    x = _matmul_res(attn, mla_w_o[0].astype(BF16), [x], tm=min(512, m))

    wr = jnp.pad(moe_w_router[0], ((0, 0), (0, LANES - N_EXPERTS))).astype(F32)
    idx, gates = _router(x, row2(ffn_norm[1]), wr, tm=min(512, m))
    tm_e = min(512, m)
    pos, pad_end, tile_expert, n_used = _route_plan(idx, m, tm_e)
    r = min(256, min(xx.shape[0] for xx in xs))
    x_sorted = _scatter_rows(pad_end, pos, x, tile_expert.shape[0] * tm_e, r=r, tm=tm_e)
    y_sorted = _expert_ffn(tile_expert, n_used, x_sorted, row2(ffn_norm[1]), moe_w_gate[0].astype(BF16),
                           moe_w_up[0].astype(BF16), moe_w_down[0].astype(BF16), tm=tm_e, tf=512)
    return _combine(pos, x, gates, y_sorted, [xx.shape[0] for xx in xs], r=r)


def kernel(x_prompt, x_sample, mix_norm, ffn_norm, na_w_qkv, na_q_gain, na_k_gain, na_rpb, na_w_o, mla_w_dqkv, mla_q_lora_gain, mla_kv_lora_gain, mla_w_uq, mla_w_ukv, mla_qn_gain, mla_qr_gain, mla_kn_gain, mla_kr_gain, mla_w_o, ffn_w_gate, ffn_w_up, ffn_w_down, moe_w_router, moe_w_gate, moe_w_up, moe_w_down):
    bp, sp, d = x_prompt.shape
    bs, ss, _ = x_sample.shape
    yp, ys = _trunk([x_prompt.reshape(bp * sp, d), x_sample.reshape(bs * ss, d)], [(sp, bp), (ss, bs)],
                    mix_norm, ffn_norm, na_w_qkv, na_q_gain, na_k_gain, na_rpb, na_w_o,
                    mla_w_dqkv, mla_q_lora_gain, mla_kv_lora_gain, mla_w_uq, mla_w_ukv,
                    mla_qn_gain, mla_qr_gain, mla_kn_gain, mla_kr_gain, mla_w_o,
                    ffn_w_gate, ffn_w_up, ffn_w_down, moe_w_router, moe_w_gate, moe_w_up, moe_w_down)
    return (yp.reshape(bp, sp, d), ys.reshape(bs, ss, d))
```

```python
import functools

import jax
import jax.numpy as jnp
from jax import lax
from jax.experimental import pallas as pl
from jax.experimental.pallas import tpu as pltpu

F32 = jnp.float32
BF16 = jnp.bfloat16

D_MODEL = 2048
GRID_W = 64
NA_HEADS = 16
NA_HEAD_DIM = 128
NA_WIN_H = 8
NA_WIN_W = 16
MLA_HEADS = 16
MLA_Q_LORA = 512
MLA_KV_LORA = 512
MLA_NOPE = 128
MLA_ROPE = 64
MLA_V = 128
ROPE_THETA = 10000.0
D_FF = 5632
N_EXPERTS = 8
TOP_K = 2
EPS = 1e-6
NEG_INF = -1e30

LANES = 128
VMEM_LIMIT = 56 * 1024 * 1024


def _params(sem):
    return pltpu.CompilerParams(dimension_semantics=sem, vmem_limit_bytes=VMEM_LIMIT)


def _rms(x, g):
    return x * lax.rsqrt(jnp.mean(x * x, axis=-1, keepdims=True) + EPS) * g


def _select_rows(i, refs, bounds, fn):
    for ref, (lo, hi) in zip(refs, bounds):
        @pl.when((i >= lo) & (i < hi))
        def _(ref=ref):
            fn(ref)


def _split_specs(xs, tm, ncols):
    specs, bounds, lo = [], [], 0
    for x in xs:
        n = x.shape[0] // tm
        assert n * tm == x.shape[0]
        specs.append(pl.BlockSpec(
            (tm, ncols), functools.partial(lambda i, *_, lo, n: (jnp.clip(i - lo, 0, n - 1), 0), lo=lo, n=n)))
        bounds.append((lo, lo + n))
        lo += n
    return specs, tuple(bounds), lo


def _norm_matmul_kernel(*refs, n_norm_tiles, bounds):
    nx = len(bounds)
    x_refs = refs[:nx]
    g_ref, w_ref, hg_ref, o_ref, h_scr = refs[nx:]
    j = pl.program_id(1)

    @pl.when(j == 0)
    def _():
        def fill(x_ref):
            h_scr[...] = _rms(x_ref[...], g_ref[...]).astype(BF16)
        _select_rows(pl.program_id(0), x_refs, bounds, fill)

    tn = o_ref.shape[1]
    chunk = min(tn, 2 * LANES)

    def emit(normed):
        for c in range(tn // chunk):
            acc = jnp.dot(h_scr[...], w_ref[:, c * chunk:(c + 1) * chunk], preferred_element_type=F32)
            for g in range(chunk // LANES):
                sl = slice(c * chunk + g * LANES, c * chunk + (g + 1) * LANES)
                a = acc[:, g * LANES:(g + 1) * LANES]
                o_ref[:, sl] = (_rms(a, hg_ref[:, sl]) if normed else a).astype(o_ref.dtype)

    @pl.when(j < n_norm_tiles)
    def _():
        emit(True)

    @pl.when(j >= n_norm_tiles)
    def _():
        emit(False)


def _norm_matmul(xs, g, w, head_gain, n_norm_cols, *, tm, tn):
    d = xs[0].shape[1]
    n = w.shape[1]
    x_specs, bounds, n_row_tiles = _split_specs(xs, tm, d)
    return pl.pallas_call(
        functools.partial(_norm_matmul_kernel, n_norm_tiles=n_norm_cols // tn, bounds=bounds),
        grid=(n_row_tiles, n // tn),
        in_specs=x_specs + [
            pl.BlockSpec((1, d), lambda i, j: (0, 0)),
            pl.BlockSpec((d, tn), lambda i, j: (0, j)),
            pl.BlockSpec((1, tn), lambda i, j: (0, j)),
        ],
        out_specs=pl.BlockSpec((tm, tn), lambda i, j: (i, j)),
        out_shape=jax.ShapeDtypeStruct((n_row_tiles * tm, n), BF16),
        scratch_shapes=[pltpu.VMEM((tm, d), BF16)],
        compiler_params=_params(("parallel", "arbitrary")),
    )(*xs, g, w, head_gain)


NA_QROWS = 4
NA_SPAN = NA_WIN_H + NA_QROWS


UNSHIFTED_SOFTMAX_BOUND = 60.0


def _safe_flag(bound):
    return (bound * 1.02 <= UNSHIFTED_SOFTMAX_BOUND).astype(jnp.int32).reshape(1)


def _na_kernel(safe_ref, q_ref, k_ref, v_ref, b_ref, *rest, rows):
    o_ref = rest[-1]
    nq = NA_QROWS * GRID_W
    nk = NA_SPAN * GRID_W
    n_blk = rows // NA_QROWS

    def run(shift):
        def body(blk, carry):
            r0 = blk * NA_QROWS
            span0 = jnp.clip(r0 - NA_WIN_H // 2, 0, rows - NA_SPAN)
            case = jnp.where(blk == 0, 0, jnp.where(blk == n_blk - 1, 2, 1))
            qoff = pl.multiple_of(r0 * GRID_W, nq)
            koff = pl.multiple_of(span0 * GRID_W, GRID_W)
            q = q_ref[pl.ds(qoff, nq), :]
            k = k_ref[pl.ds(koff, nk), :]
            v = v_ref[pl.ds(koff, nk), :]
            s = lax.dot_general(q, k, (((1,), (1,)), ((), ())), preferred_element_type=F32)
            s = s + b_ref[0, case]
            if shift:
                s = s - jnp.max(s, axis=-1, keepdims=True)
            p = jnp.exp(s)
            l = jnp.sum(p, axis=-1, keepdims=True)
            o = jnp.dot(p.astype(BF16), v, preferred_element_type=F32) / l
            o_ref[pl.ds(qoff, nq), :] = o.astype(o_ref.dtype)
            return carry

        lax.fori_loop(0, n_blk, body, 0, unroll=2)

    @pl.when(safe_ref[0] == 1)
    def _():
        run(False)

    @pl.when(safe_ref[0] != 1)
    def _():
        run(True)


def _na_bias_table(rpb):
    qc = jnp.arange(GRID_W)[:, None]
    kc = jnp.arange(GRID_W)[None, :]
    ws = jnp.clip(qc - NA_WIN_W // 2, 0, GRID_W - NA_WIN_W)
    col_ok = (kc >= ws) & (kc < ws + NA_WIN_W)
    dc = jnp.clip(kc - qc + NA_WIN_W - 1, 0, 2 * NA_WIN_W - 2)
    i = jnp.arange(NA_QROWS)
    half = NA_WIN_H // 2
    w_off = jnp.stack([jnp.zeros_like(i), i, jnp.full_like(i, NA_SPAN - NA_WIN_H)])
    c_off = jnp.stack([i, jnp.full_like(i, half), half + i])
    j = jnp.arange(NA_SPAN)[None, None, :] - w_off[:, :, None]
    row_ok = (j >= 0) & (j < NA_WIN_H)
    dr = jnp.clip(j - c_off[:, :, None] + NA_WIN_H - 1, 0, 2 * NA_WIN_H - 2)
    oh_r = jax.nn.one_hot(dr, 2 * NA_WIN_H - 1, dtype=F32)
    oh_c = jax.nn.one_hot(dc, 2 * NA_WIN_W - 1, dtype=F32)
    t = jnp.einsum('hab,xija,qkb->hxiqjk', rpb.astype(F32), oh_r, oh_c, precision=lax.Precision.HIGHEST)
    ok = row_ok[None, :, :, None, :, None] & col_ok[None, None, None, :, None, :]
    t = jnp.where(ok, t, NEG_INF)
    return t.reshape(rpb.shape[0], 3, NA_QROWS * GRID_W, NA_SPAN * GRID_W)


def _na_attention(safe, qkv, bias, out_prev, *, seq, n_batch, row_block_off):
    m = qkv.shape[0]
    rows = seq // GRID_W
    assert rows >= NA_SPAN + NA_QROWS and rows % NA_QROWS == 0
    h = NA_HEADS

    def spec(sec):
        return pl.BlockSpec((seq, NA_HEAD_DIM), lambda hh, b: (row_block_off + b, sec * h + hh))

    in_specs = [pl.BlockSpec(memory_space=pltpu.SMEM), spec(0), spec(1), spec(2),
                pl.BlockSpec((1,) + bias.shape[1:], lambda hh, b: (hh, 0, 0, 0))]
    args = [safe, qkv, qkv, qkv, bias]
    aliases = {}
    if out_prev is not None:
        in_specs.append(pl.BlockSpec(memory_space=pl.ANY))
        args.append(out_prev)
        aliases = {5: 0}
    return pl.pallas_call(
        functools.partial(_na_kernel, rows=rows),
        grid=(h, n_batch),
        in_specs=in_specs,
        out_specs=pl.BlockSpec((seq, NA_HEAD_DIM), lambda hh, b: (row_block_off + b, hh)),
        out_shape=jax.ShapeDtypeStruct((m, h * NA_HEAD_DIM), BF16),
        input_output_aliases=aliases,
        compiler_params=_params(("parallel", "parallel")),
    )(*args)


def _matmul_res_kernel(a_ref, w_ref, *refs, bounds):
    r_refs, o_ref = refs[:-1], refs[-1]
    y = jnp.dot(a_ref[...], w_ref[...], preferred_element_type=F32)

    def add(r_ref):
        o_ref[...] = r_ref[...] + y
    _select_rows(pl.program_id(0), r_refs, bounds, add)


def _matmul_res(a, w, res_list, *, tm):
    m, k = a.shape
    n = w.shape[1]
    r_specs, bounds, n_row_tiles = _split_specs(res_list, tm, n)
    assert n_row_tiles * tm == m
    return pl.pallas_call(
        functools.partial(_matmul_res_kernel, bounds=bounds),
        grid=(n_row_tiles,),
        in_specs=[
            pl.BlockSpec((tm, k), lambda i: (i, 0)),
            pl.BlockSpec((k, n), lambda i: (0, 0)),
        ] + r_specs,
        out_specs=pl.BlockSpec((tm, n), lambda i: (i, 0)),
        out_shape=jax.ShapeDtypeStruct((m, n), F32),
        compiler_params=_params(("parallel",)),
    )(a, w, *res_list)


def _swiglu_step(h, wg, wu, wd):
    a = jnp.dot(h, wg, preferred_element_type=F32)
    b = jnp.dot(h, wu, preferred_element_type=F32)
    t = (a / (1.0 + jnp.exp(-a)) * b).astype(BF16)
    return jnp.dot(t, wd, preferred_element_type=F32)


def _ffn_kernel(x_ref, g_ref, wg_ref, wu_ref, wd_ref, o_ref, h_scr):
    f = pl.program_id(1)

    @pl.when(f == 0)
    def _():
        x = x_ref[...]
        h_scr[...] = _rms(x, g_ref[...]).astype(BF16)
        o_ref[...] = x

    o_ref[...] += _swiglu_step(h_scr[...], wg_ref[...], wu_ref[...], wd_ref[...])


def _ffn(x, g, wg, wu, wd, *, tm, tf):
    m, d = x.shape
    f = wg.shape[1]
    return pl.pallas_call(
        _ffn_kernel,
        grid=(m // tm, f // tf),
        in_specs=[
            pl.BlockSpec((tm, d), lambda i, j: (i, 0)),
            pl.BlockSpec((1, d), lambda i, j: (0, 0)),
            pl.BlockSpec((d, tf), lambda i, j: (0, j)),
            pl.BlockSpec((d, tf), lambda i, j: (0, j)),
            pl.BlockSpec((tf, d), lambda i, j: (j, 0)),
        ],
        out_specs=pl.BlockSpec((tm, d), lambda i, j: (i, 0)),
        out_shape=jax.ShapeDtypeStruct((m, d), F32),
        scratch_shapes=[pltpu.VMEM((tm, d), BF16)],
        compiler_params=_params(("parallel", "arbitrary")),
    )(x, g, wg, wu, wd)


def _expert_ffn_kernel(te_ref, nu_ref, x_ref, g_ref, wg_ref, wu_ref, wd_ref, o_ref, h_scr):
    i = pl.program_id(0)
    f = pl.program_id(1)

    @pl.when(i < nu_ref[0])
    def _():
        @pl.when(f == 0)
        def _():
            h_scr[...] = _rms(x_ref[...], g_ref[...]).astype(BF16)
            o_ref[...] = jnp.zeros_like(o_ref)

        o_ref[...] += _swiglu_step(h_scr[...], wg_ref[0], wu_ref[0], wd_ref[0])


def _expert_ffn(tile_expert, n_used, xs, g, wg, wu, wd, *, tm, tf):
    p, d = xs.shape
    nf = wg.shape[2] // tf

    def row_map(i, j, te, nu):
        return (jnp.minimum(i, nu[0] - 1), 0)

    def wcol_map(i, j, te, nu):
        used = i < nu[0]
        return (te[jnp.minimum(i, nu[0] - 1)], 0, jnp.where(used, j, nf - 1))

    def wrow_map(i, j, te, nu):
        used = i < nu[0]
        return (te[jnp.minimum(i, nu[0] - 1)], jnp.where(used, j, nf - 1), 0)

    grid_spec = pltpu.PrefetchScalarGridSpec(
        num_scalar_prefetch=2,
        grid=(p // tm, nf),
        in_specs=[
            pl.BlockSpec((tm, d), row_map),
            pl.BlockSpec((1, d), lambda i, j, te, nu: (0, 0)),
            pl.BlockSpec((1, d, tf), wcol_map),
            pl.BlockSpec((1, d, tf), wcol_map),
            pl.BlockSpec((1, tf, d), wrow_map),
        ],
        out_specs=pl.BlockSpec((tm, d), row_map),
        scratch_shapes=[pltpu.VMEM((tm, d), BF16)],
    )
    return pl.pallas_call(
        _expert_ffn_kernel,
        grid_spec=grid_spec,
        out_shape=jax.ShapeDtypeStruct((p, d), F32),
        compiler_params=_params(("arbitrary", "arbitrary")),
    )(tile_expert, n_used, xs, g, wg, wu, wd)


ROPE_PAD = LANES


def _mla_proj_kernel(x_ref, g_ref, wd_ref, gq_ref, gkv_ref, gkr_ref, wuq_ref, gqn_ref, gqr_ref,
                     wukv_ref, gkn_ref, cos_ref, sinn_ref, sinp_ref,
                     qn_ref, qr_ref, kn_ref, kr_ref, v_ref):
    half = MLA_ROPE // 2
    cos = cos_ref[...]
    sinn = sinn_ref[...]
    sinp = sinp_ref[...]

    def rope(piece, gain):
        ms = jnp.sum(piece * piece, axis=-1, keepdims=True) * (1.0 / MLA_ROPE)
        y = piece * lax.rsqrt(ms + EPS) * gain
        return (y * cos + pltpu.roll(y, ROPE_PAD - half, 1) * sinn + pltpu.roll(y, half, 1) * sinp)

    h = _rms(x_ref[...], g_ref[...]).astype(BF16)
    lat = jnp.dot(h, wd_ref[...], preferred_element_type=F32)
    cq = _rms(lat[:, :MLA_Q_LORA], gq_ref[...]).astype(BF16)
    ckv = _rms(lat[:, MLA_Q_LORA:MLA_Q_LORA + MLA_KV_LORA], gkv_ref[...]).astype(BF16)
    kr_ref[...] = rope(lat[:, MLA_Q_LORA + MLA_KV_LORA:], gkr_ref[...]).astype(BF16)

    hd = MLA_HEADS * LANES
    chunk = 4 * LANES
    for c in range(hd // chunk):
        base = c * chunk
        qn = jnp.dot(cq, wuq_ref[:, base:base + chunk], preferred_element_type=F32)
        qr = jnp.dot(cq, wuq_ref[:, hd + base:hd + base + chunk], preferred_element_type=F32)
        kn = jnp.dot(ckv, wukv_ref[:, base:base + chunk], preferred_element_type=F32)
        v = jnp.dot(ckv, wukv_ref[:, hd + base:hd + base + chunk], preferred_element_type=F32)
        v_ref[:, base:base + chunk] = v.astype(BF16)
        for hh in range(chunk // LANES):
            sl = slice(hh * LANES, (hh + 1) * LANES)
            osl = slice(base + hh * LANES, base + (hh + 1) * LANES)
            qn_ref[:, osl] = _rms(qn[:, sl], gqn_ref[...]).astype(BF16)
            kn_ref[:, osl] = _rms(kn[:, sl], gkn_ref[...]).astype(BF16)
            qr_ref[:, osl] = rope(qr[:, sl], gqr_ref[...]).astype(BF16)


def _mla_proj(x, g, wd, gq, gkv, gkr, wuq, gqn, gqr, wukv, gkn, cos, sinn, sinp, table_block, *, tm):
    m, d = x.shape
    hd = MLA_HEADS * LANES
    const = lambda i: (0, 0)
    row = lambda i: (i, 0)
    tab = lambda i: (table_block(i), 0)
    return pl.pallas_call(
        _mla_proj_kernel,
        grid=(m // tm,),
        in_specs=[
            pl.BlockSpec((tm, d), row),
            pl.BlockSpec((1, d), const),
            pl.BlockSpec(wd.shape, const, pipeline_mode=pl.Buffered(1)),
            pl.BlockSpec((1, MLA_Q_LORA), const),
            pl.BlockSpec((1, MLA_KV_LORA), const),
            pl.BlockSpec((1, LANES), const),
            pl.BlockSpec(wuq.shape, const, pipeline_mode=pl.Buffered(1)),
            pl.BlockSpec((1, LANES), const),
            pl.BlockSpec((1, LANES), const),
            pl.BlockSpec(wukv.shape, const, pipeline_mode=pl.Buffered(1)),
            pl.BlockSpec((1, LANES), const),
            pl.BlockSpec((tm, LANES), tab),
            pl.BlockSpec((tm, LANES), tab),
            pl.BlockSpec((tm, LANES), tab),
        ],
        out_specs=[
            pl.BlockSpec((tm, hd), row),
            pl.BlockSpec((tm, hd), row),
            pl.BlockSpec((tm, hd), row),
            pl.BlockSpec((tm, LANES), row),
            pl.BlockSpec((tm, hd), row),
        ],
        out_shape=[
            jax.ShapeDtypeStruct((m, hd), BF16),
            jax.ShapeDtypeStruct((m, hd), BF16),
            jax.ShapeDtypeStruct((m, hd), BF16),
            jax.ShapeDtypeStruct((m, LANES), BF16),
            jax.ShapeDtypeStruct((m, hd), BF16),
        ],
        compiler_params=_params(("parallel",)),
    )(x, g, wd, gq, gkv, gkr, wuq, gqn, gqr, wukv, gkn, cos, sinn, sinp)


def _rope_tables(seq):
    t = jnp.arange(seq)
    row = (t // GRID_W).astype(F32)
    col = (t % GRID_W).astype(F32)
    n_pairs = MLA_ROPE // 4
    inv = ROPE_THETA ** (-jnp.arange(n_pairs, dtype=F32) / n_pairs)
    ang = jnp.concatenate([row[:, None] * inv, col[:, None] * inv], axis=-1)
    c, s = jnp.cos(ang), jnp.sin(ang)
    z = jnp.zeros_like(c)
    cos = jnp.concatenate([c, c, z, z], axis=-1)
    sinn = jnp.concatenate([-s, z, z, z], axis=-1)
    sinp = jnp.concatenate([z, s, z, z], axis=-1)
    return cos, sinn, sinp


FLASH_TQ = 1024
FLASH_TK = 2048
FLASH_HEADS = 2
FLASH_SUB_Q = 256


def _flash_kernel(safe_ref, qn_ref, qr_ref, kn_ref, kr_ref, v_ref, *rest, hb, sub_q, nk):
    if nk == 1:
        o_ref = rest[-1]
    else:
        o_ref, m_scr, l_scr, acc_scr = rest[-4:]
        kk = pl.program_id(3)

        @pl.when(kk == 0)
        def _():
            m_scr[...] = jnp.full_like(m_scr, -jnp.inf)
            l_scr[...] = jnp.zeros_like(l_scr)
            acc_scr[...] = jnp.zeros_like(acc_scr)

    tq = qn_ref.shape[0]

    def run(shift):
        kr = kr_ref[...]
        for h in range(hb):
            hs = slice(h * LANES, (h + 1) * LANES)
            k = jnp.concatenate([kn_ref[:, hs], kr], axis=-1)
            v = v_ref[:, hs]
            for qi in range(tq // sub_q):
                rs = slice(qi * sub_q, (qi + 1) * sub_q)
                q = jnp.concatenate([qn_ref[rs, hs], qr_ref[rs, hs]], axis=-1)
                s = lax.dot_general(q, k, (((1,), (1,)), ((), ())), preferred_element_type=F32)
                if not shift:
                    p = jnp.exp(s)
                    l = jnp.sum(p, axis=-1, keepdims=True)
                    pv = jnp.dot(p.astype(BF16), v, preferred_element_type=F32)
                    if nk == 1:
                        o_ref[rs, hs] = (pv / l).astype(o_ref.dtype)
                    else:
                        l_scr[h, rs] += l
                        acc_scr[rs, hs] += pv
                elif nk == 1:
                    p = jnp.exp(s - jnp.max(s, axis=-1, keepdims=True))
                    l = jnp.sum(p, axis=-1, keepdims=True)
                    o = jnp.dot(p.astype(BF16), v, preferred_element_type=F32) / l
                    o_ref[rs, hs] = o.astype(o_ref.dtype)
                else:
                    m_prev = m_scr[h, rs]
                    m_new = jnp.maximum(m_prev, jnp.max(s, axis=-1, keepdims=True))
                    alpha = jnp.exp(m_prev - m_new)
                    p = jnp.exp(s - m_new)
                    l_scr[h, rs] = alpha * l_scr[h, rs] + jnp.sum(p, axis=-1, keepdims=True)
                    acc_scr[rs, hs] = (alpha * acc_scr[rs, hs]
                                       + jnp.dot(p.astype(BF16), v, preferred_element_type=F32))
                    m_scr[h, rs] = m_new

    @pl.when(safe_ref[0] == 1)
    def _():
        run(False)

    @pl.when(safe_ref[0] != 1)
    def _():
        run(True)

    if nk > 1:
        @pl.when(kk == nk - 1)
        def _():
            for h in range(hb):
                hs = slice(h * LANES, (h + 1) * LANES)
                o_ref[:, hs] = (acc_scr[:, hs] / l_scr[h]).astype(o_ref.dtype)


def _flash(safe, qn, qr, kn, kr, v, out_prev, *, seq, n_batch, row_off, tq, tk, hb, sub_q):
    m = qn.shape[0]
    h = MLA_HEADS
    nq, nk = seq // tq, seq // tk
    qoff, koff = row_off // tq, row_off // tk
    qmap = lambda b, hh, i, j: (qoff + b * nq + i, hh)
    kmap = lambda b, hh, i, j: (koff + b * nk + j, hh)
    in_specs = [
        pl.BlockSpec(memory_space=pltpu.SMEM),
        pl.BlockSpec((tq, hb * LANES), qmap),
        pl.BlockSpec((tq, hb * LANES), qmap),
        pl.BlockSpec((tk, hb * LANES), kmap),
        pl.BlockSpec((tk, LANES), lambda b, hh, i, j: (koff + b * nk + j, 0)),
        pl.BlockSpec((tk, hb * LANES), kmap),
    ]
    args = [safe, qn, qr, kn, kr, v]
    aliases = {}
    if out_prev is not None:
        in_specs.append(pl.BlockSpec(memory_space=pl.ANY))
        args.append(out_prev)
        aliases = {6: 0}
    scratch = []
    if nk > 1:
        scratch = [pltpu.VMEM((hb, tq, 1), F32), pltpu.VMEM((hb, tq, 1), F32), pltpu.VMEM((tq, hb * MLA_V), F32)]
    return pl.pallas_call(
        functools.partial(_flash_kernel, hb=hb, sub_q=sub_q, nk=nk),
        grid=(n_batch, h // hb, nq, nk),
        in_specs=in_specs,
        out_specs=pl.BlockSpec((tq, hb * LANES), qmap),
        out_shape=jax.ShapeDtypeStruct((m, h * MLA_V), BF16),
        scratch_shapes=scratch,
        input_output_aliases=aliases,
        compiler_params=_params(("parallel", "parallel", "parallel", "arbitrary")),
    )(*args)


def _router_kernel(x_ref, g_ref, wr_ref, idx_ref, gate_ref):
    h = _rms(x_ref[...], g_ref[...])
    logits = jnp.dot(h, wr_ref[...], precision=lax.Precision.HIGHEST, preferred_element_type=F32)
    lane = lax.broadcasted_iota(jnp.int32, logits.shape, 1)
    logits = jnp.where(lane < N_EXPERTS, logits, -jnp.inf)
    m1 = jnp.max(logits, axis=-1, keepdims=True)
    i1 = jnp.min(jnp.where(logits == m1, lane, LANES), axis=-1, keepdims=True)
    rest = jnp.where(lane == i1, -jnp.inf, logits)
    m2 = jnp.max(rest, axis=-1, keepdims=True)
    i2 = jnp.min(jnp.where(rest == m2, lane, LANES), axis=-1, keepdims=True)
    e = jnp.exp(m2 - m1)
    g1 = 1.0 / (1.0 + e)
    g2 = e / (1.0 + e)
    idx_ref[...] = jnp.where(lane == 0, i1, jnp.where(lane == 1, i2, 0))
    gate_ref[...] = jnp.where(lane == 0, g1, jnp.where(lane == 1, g2, 0.0))


def _router(x, g, wr, *, tm):
    m, d = x.shape
    return pl.pallas_call(
        _router_kernel,
        grid=(m // tm,),
        in_specs=[
            pl.BlockSpec((tm, d), lambda i: (i, 0)),
            pl.BlockSpec((1, d), lambda i: (0, 0)),
            pl.BlockSpec((d, LANES), lambda i: (0, 0)),
        ],
        out_specs=[pl.BlockSpec((tm, LANES), lambda i: (i, 0)), pl.BlockSpec((tm, LANES), lambda i: (i, 0))],
        out_shape=[jax.ShapeDtypeStruct((m, LANES), jnp.int32), jax.ShapeDtypeStruct((m, LANES), F32)],
        compiler_params=_params(("parallel",)),
    )(x, g, wr)


DMA_UNROLL = 8


def _scatter_kernel(pe_ref, pos_ref, x_ref, o_hbm, zero_scr, sem, zsem, *, tm):
    r = x_ref.shape[0]

    def zero_copy(e):
        end = pe_ref[e]
        return pltpu.make_async_copy(zero_scr, o_hbm.at[pl.ds(pl.multiple_of(end - tm, tm), tm)], zsem)

    def nonempty(e):
        return pe_ref[e] > (pe_ref[e - 1] if e else 0)

    @pl.when(pl.program_id(0) == 0)
    def _():
        zero_scr[...] = jnp.zeros_like(zero_scr)
        for e in range(N_EXPERTS):
            @pl.when(nonempty(e))
            def _(e=e):
                zero_copy(e).start()
        for e in range(N_EXPERTS):
            @pl.when(nonempty(e))
            def _(e=e):
                zero_copy(e).wait()

    def row_copy(j, row):
        return pltpu.make_async_copy(x_ref.at[pl.ds(j, 1)], o_hbm.at[pl.ds(row, 1)], sem)

    def start(j, c):
        for k in range(TOP_K):
            row_copy(j, pos_ref[0, 0, TOP_K * j + k]).start()
        return c

    lax.fori_loop(0, r, start, 0, unroll=DMA_UNROLL)
    for k in range(TOP_K):
        pltpu.make_async_copy(x_ref, o_hbm.at[pl.ds(0, r)], sem).wait()


def _scatter_rows(pad_end, pos, x, p_rows, *, r, tm):
    m, d = x.shape
    grid_spec = pltpu.PrefetchScalarGridSpec(
        num_scalar_prefetch=1,
        grid=(m // r,),
        in_specs=[
            pl.BlockSpec((1, 1, TOP_K * r), lambda i, pe: (i, 0, 0), memory_space=pltpu.SMEM),
            pl.BlockSpec((r, d), lambda i, pe: (i, 0)),
        ],
        out_specs=pl.BlockSpec(memory_space=pl.ANY),
        scratch_shapes=[pltpu.VMEM((tm, d), F32), pltpu.SemaphoreType.DMA(()), pltpu.SemaphoreType.DMA(())],
    )
    return pl.pallas_call(
        functools.partial(_scatter_kernel, tm=tm),
        grid_spec=grid_spec,
        out_shape=jax.ShapeDtypeStruct((p_rows, d), F32),
        compiler_params=_params(("arbitrary",)),
    )(pad_end, pos.reshape(m // r, 1, TOP_K * r), x)


def _combine_kernel(pos_ref, x_ref, gate_ref, y_hbm, *rest, bounds):
    o_refs = rest[:len(bounds)]
    buf, sem = rest[len(bounds):]
    r = x_ref.shape[0]

    def row_copy(j, k, row):
        return pltpu.make_async_copy(y_hbm.at[pl.ds(row, 1)], buf.at[k, pl.ds(j, 1)], sem)

    def start(j, c):
        for k in range(TOP_K):
            row_copy(j, k, pos_ref[0, 0, TOP_K * j + k]).start()
        return c

    lax.fori_loop(0, r, start, 0, unroll=DMA_UNROLL)
    for k in range(TOP_K):
        pltpu.make_async_copy(y_hbm.at[pl.ds(0, r)], buf.at[k], sem).wait()
    gate = gate_ref[...]
    out = x_ref[...] + (gate[:, 0:1] * buf[0] + gate[:, 1:2] * buf[1])

    def write(o_ref):
        o_ref[...] = out
    _select_rows(pl.program_id(0), o_refs, bounds, write)


def _combine(pos, x, gates, y, out_rows, *, r):
    m, d = x.shape
    out_shape = [jax.ShapeDtypeStruct((n, d), F32) for n in out_rows]
    o_specs, bounds, n_row_tiles = _split_specs(out_shape, r, d)
    assert n_row_tiles * r == m
    return pl.pallas_call(
        functools.partial(_combine_kernel, bounds=bounds),
        grid=(n_row_tiles,),
        in_specs=[
            pl.BlockSpec((1, 1, TOP_K * r), lambda i: (i, 0, 0), memory_space=pltpu.SMEM),
            pl.BlockSpec((r, d), lambda i: (i, 0)),
            pl.BlockSpec((r, LANES), lambda i: (i, 0)),
            pl.BlockSpec(memory_space=pl.ANY),
        ],
        out_specs=o_specs,
        out_shape=out_shape,
        scratch_shapes=[pltpu.VMEM((TOP_K, r, d), F32), pltpu.SemaphoreType.DMA(())],
        compiler_params=_params(("arbitrary",)),
    )(pos.reshape(m // r, 1, TOP_K * r), x, gates, y)


def _route_plan(idx, n_tokens, tm):
    n_pairs = n_tokens * TOP_K
    n_tiles = n_pairs // tm + N_EXPERTS
    e_flat = idx[:, :TOP_K].reshape(n_pairs)
    onehot = (e_flat[:, None] == jnp.arange(N_EXPERTS)[None, :]).astype(jnp.int32)
    csum = jnp.cumsum(onehot, axis=0)
    counts = csum[-1]
    padded = ((counts + tm - 1) // tm) * tm
    pad_end = jnp.cumsum(padded)
    pad_start = pad_end - padded
    pos = jnp.sum((csum - onehot + pad_start[None, :]) * onehot, axis=1)
    tile_start = jnp.arange(n_tiles) * tm
    tile_expert = jnp.minimum(jnp.sum(tile_start[:, None] >= pad_end[None, :], axis=1), N_EXPERTS - 1)
    n_used = pad_end[-1:] // tm
    return (pos.astype(jnp.int32), pad_end.astype(jnp.int32), tile_expert.astype(jnp.int32),
            n_used.astype(jnp.int32))


def _trunk(xs, seqs, mix_norm, ffn_norm, na_w_qkv, na_q_gain, na_k_gain, na_rpb, na_w_o,
           mla_w_dqkv, mla_q_lora_gain, mla_kv_lora_gain, mla_w_uq, mla_w_ukv,
           mla_qn_gain, mla_qr_gain, mla_kn_gain, mla_kr_gain, mla_w_o,
           ffn_w_gate, ffn_w_up, ffn_w_down, moe_w_router, moe_w_gate, moe_w_up, moe_w_down):
    m = sum(x.shape[0] for x in xs)
    row2 = lambda a: a.reshape(1, -1).astype(F32)
    na_dim = NA_HEADS * NA_HEAD_DIM

    head_gain = jnp.concatenate([
        jnp.tile(na_q_gain[0] * NA_HEAD_DIM ** -0.5, NA_HEADS),
        jnp.tile(na_k_gain[0], NA_HEADS),
        jnp.ones((na_dim,), F32)]).reshape(1, -1)
    qkv = _norm_matmul(xs, row2(mix_norm[0]), na_w_qkv[0].astype(BF16), head_gain, 2 * na_dim,
                       tm=min(1024, min(x.shape[0] for x in xs)), tn=512)
    bias = _na_bias_table(na_rpb[0])
    amax = lambda a: jnp.max(jnp.abs(a.astype(F32)))
    na_bound = (NA_HEAD_DIM ** 0.5 * amax(na_q_gain[0]) * amax(na_k_gain[0]) + amax(na_rpb[0]))
    na_safe = _safe_flag(na_bound)
    attn = None
    off = 0
    for seq, nb in seqs:
        attn = _na_attention(na_safe, qkv, bias, attn, seq=seq, n_batch=nb, row_block_off=off // seq)
        off += seq * nb
    x = _matmul_res(attn, na_w_o[0].astype(BF16), xs, tm=min(512, m))

    x = _ffn(x, row2(ffn_norm[0]), ffn_w_gate[0].astype(BF16), ffn_w_up[0].astype(BF16),
             ffn_w_down[0].astype(BF16), tm=min(512, m), tf=512)

    scale = (MLA_NOPE + MLA_ROPE) ** -0.5
    half = MLA_ROPE // 2
    lat_w = MLA_Q_LORA + MLA_KV_LORA
    wd = jnp.pad(mla_w_dqkv[0], ((0, 0), (0, ROPE_PAD - MLA_ROPE))).astype(BF16)
    wuq = mla_w_uq[0].reshape(MLA_Q_LORA, MLA_HEADS, MLA_NOPE + MLA_ROPE)
    wuq = jnp.concatenate([
        wuq[:, :, :MLA_NOPE].reshape(MLA_Q_LORA, -1),
        jnp.pad(wuq[:, :, MLA_NOPE:], ((0, 0), (0, 0), (0, ROPE_PAD - MLA_ROPE))).reshape(MLA_Q_LORA, -1),
    ], axis=1).astype(BF16)
    wukv = mla_w_ukv[0].reshape(MLA_KV_LORA, MLA_HEADS, MLA_NOPE + MLA_V)
    wukv = jnp.concatenate([wukv[:, :, :MLA_NOPE].reshape(MLA_KV_LORA, -1),
                            wukv[:, :, MLA_NOPE:].reshape(MLA_KV_LORA, -1)], axis=1).astype(BF16)
    pad_gain = lambda gg: jnp.pad(gg, (0, ROPE_PAD - MLA_ROPE)).reshape(1, -1).astype(F32)
    max_seq = max(s for s, _ in seqs)
    cos, sinn, sinp = _rope_tables(max_seq)
    tm_p = min(512, m)
    bounds = []
    off = 0
    for seq, nb in seqs:
        bounds.append((off // tm_p, seq // tm_p))
        off += seq * nb

    def table_block(i):
        blk = i
        for start, per in bounds:
            blk = jnp.where(i >= start, (i - start) % per, blk)
        return blk

    qn, qr, kn, kr, v = _mla_proj(
        x, row2(mix_norm[1]), wd, row2(mla_q_lora_gain[0]), row2(mla_kv_lora_gain[0]), pad_gain(mla_kr_gain[0]),
        wuq, row2(mla_qn_gain[0] * scale), pad_gain(mla_qr_gain[0] * scale), wukv, row2(mla_kn_gain[0]),
        cos, sinn, sinp, table_block, tm=tm_p)
    mla_bound = scale * (MLA_NOPE * amax(mla_qn_gain[0]) * amax(mla_kn_gain[0])
                         + MLA_ROPE * amax(mla_qr_gain[0]) * amax(mla_kr_gain[0]))
    mla_safe = _safe_flag(mla_bound)
    attn = None
    off = 0
    for seq, nb in seqs:
        tq = min(FLASH_TQ, seq)
        attn = _flash(mla_safe, qn, qr, kn, kr, v, attn, seq=seq, n_batch=nb, row_off=off,
                      tq=tq, tk=min(FLASH_TK, seq), hb=FLASH_HEADS, sub_q=min(FLASH_SUB_Q, tq))
        off += seq * nb
    x = _matmul_res(attn, mla_w_o[0].astype(BF16), [x], tm=min(512, m))

    wr = jnp.pad(moe_w_router[0], ((0, 0), (0, LANES - N_EXPERTS))).astype(F32)
    idx, gates = _router(x, row2(ffn_norm[1]), wr, tm=min(512, m))
    tm_e = min(512, m)
    pos, pad_end, tile_expert, n_used = _route_plan(idx, m, tm_e)
    r = min(256, min(xx.shape[0] for xx in xs))
    x_sorted = _scatter_rows(pad_end, pos, x, tile_expert.shape[0] * tm_e, r=r, tm=tm_e)
    y_sorted = _expert_ffn(tile_expert, n_used, x_sorted, row2(ffn_norm[1]), moe_w_gate[0].astype(BF16),
                           moe_w_up[0].astype(BF16), moe_w_down[0].astype(BF16), tm=tm_e, tf=512)
    return _combine(pos, x, gates, y_sorted, [xx.shape[0] for xx in xs], r=r)


def kernel(x_prompt, x_sample, mix_norm, ffn_norm, na_w_qkv, na_q_gain, na_k_gain, na_rpb, na_w_o, mla_w_dqkv, mla_q_lora_gain, mla_kv_lora_gain, mla_w_uq, mla_w_ukv, mla_qn_gain, mla_qr_gain, mla_kn_gain, mla_kr_gain, mla_w_o, ffn_w_gate, ffn_w_up, ffn_w_down, moe_w_router, moe_w_gate, moe_w_up, moe_w_down):
    bp, sp, d = x_prompt.shape
    bs, ss, _ = x_sample.shape
    yp, ys = _trunk([x_prompt.reshape(bp * sp, d), x_sample.reshape(bs * ss, d)], [(sp, bp), (ss, bs)],
                    mix_norm, ffn_norm, na_w_qkv, na_q_gain, na_k_gain, na_rpb, na_w_o,
                    mla_w_dqkv, mla_q_lora_gain, mla_kv_lora_gain, mla_w_uq, mla_w_ukv,
                    mla_qn_gain, mla_qr_gain, mla_kn_gain, mla_kr_gain, mla_w_o,
                    ffn_w_gate, ffn_w_up, ffn_w_down, moe_w_router, moe_w_gate, moe_w_up, moe_w_down)
    return (yp.reshape(bp, sp, d), ys.reshape(bs, ss, d))
```

```python
import functools

import jax
import jax.numpy as jnp
from jax import lax
from jax.experimental import pallas as pl
from jax.experimental.pallas import tpu as pltpu

F32 = jnp.float32
BF16 = jnp.bfloat16

D_MODEL = 2048
GRID_W = 64
NA_HEADS = 16
NA_HEAD_DIM = 128
NA_WIN_H = 8
NA_WIN_W = 16
MLA_HEADS = 16
MLA_Q_LORA = 512
MLA_KV_LORA = 512
MLA_NOPE = 128
MLA_ROPE = 64
MLA_V = 128
ROPE_THETA = 10000.0
D_FF = 5632
N_EXPERTS = 8
TOP_K = 2
EPS = 1e-6
NEG_INF = -1e30

LANES = 128
VMEM_LIMIT = 56 * 1024 * 1024


def _params(sem):
    return pltpu.CompilerParams(dimension_semantics=sem, vmem_limit_bytes=VMEM_LIMIT)


def _rms(x, g):
    return x * lax.rsqrt(jnp.mean(x * x, axis=-1, keepdims=True) + EPS) * g


def _select_rows(i, refs, bounds, fn):
    for ref, (lo, hi) in zip(refs, bounds):
        @pl.when((i >= lo) & (i < hi))
        def _(ref=ref):
            fn(ref)


def _split_specs(xs, tm, ncols):
    specs, bounds, lo = [], [], 0
    for x in xs:
        n = x.shape[0] // tm
        assert n * tm == x.shape[0]
        specs.append(pl.BlockSpec(
            (tm, ncols), functools.partial(lambda i, *_, lo, n: (jnp.clip(i - lo, 0, n - 1), 0), lo=lo, n=n)))
        bounds.append((lo, lo + n))
        lo += n
    return specs, tuple(bounds), lo


def _norm_matmul_kernel(*refs, n_norm_tiles, bounds):
    nx = len(bounds)
    x_refs = refs[:nx]
    g_ref, w_ref, hg_ref, o_ref, h_scr = refs[nx:]
    j = pl.program_id(1)

    @pl.when(j == 0)
    def _():
        def fill(x_ref):
            h_scr[...] = _rms(x_ref[...], g_ref[...]).astype(BF16)
        _select_rows(pl.program_id(0), x_refs, bounds, fill)

    tn = o_ref.shape[1]
    chunk = min(tn, 2 * LANES)

    def emit(normed):
        for c in range(tn // chunk):
            acc = jnp.dot(h_scr[...], w_ref[:, c * chunk:(c + 1) * chunk], preferred_element_type=F32)
            for g in range(chunk // LANES):
                sl = slice(c * chunk + g * LANES, c * chunk + (g + 1) * LANES)
                a = acc[:, g * LANES:(g + 1) * LANES]
                o_ref[:, sl] = (_rms(a, hg_ref[:, sl]) if normed else a).astype(o_ref.dtype)

    @pl.when(j < n_norm_tiles)
    def _():
        emit(True)

    @pl.when(j >= n_norm_tiles)
    def _():
        emit(False)


def _norm_matmul(xs, g, w, head_gain, n_norm_cols, *, tm, tn):
    d = xs[0].shape[1]
    n = w.shape[1]
    x_specs, bounds, n_row_tiles = _split_specs(xs, tm, d)
    return pl.pallas_call(
        functools.partial(_norm_matmul_kernel, n_norm_tiles=n_norm_cols // tn, bounds=bounds),
        grid=(n_row_tiles, n // tn),
        in_specs=x_specs + [
            pl.BlockSpec((1, d), lambda i, j: (0, 0)),
            pl.BlockSpec((d, tn), lambda i, j: (0, j)),
            pl.BlockSpec((1, tn), lambda i, j: (0, j)),
        ],
        out_specs=pl.BlockSpec((tm, tn), lambda i, j: (i, j)),
        out_shape=jax.ShapeDtypeStruct((n_row_tiles * tm, n), BF16),
        scratch_shapes=[pltpu.VMEM((tm, d), BF16)],
        compiler_params=_params(("parallel", "arbitrary")),
    )(*xs, g, w, head_gain)


NA_QROWS = 4
NA_SPAN = NA_WIN_H + NA_QROWS


UNSHIFTED_SOFTMAX_BOUND = 60.0


def _safe_flag(bound):
    return (bound * 1.02 <= UNSHIFTED_SOFTMAX_BOUND).astype(jnp.int32).reshape(1)


def _na_kernel(safe_ref, q_ref, k_ref, v_ref, b_ref, *rest, rows):
    o_ref = rest[-1]
    nq = NA_QROWS * GRID_W
    nk = NA_SPAN * GRID_W
    n_blk = rows // NA_QROWS

    def run(shift):
        def body(blk, carry):
            r0 = blk * NA_QROWS
            span0 = jnp.clip(r0 - NA_WIN_H // 2, 0, rows - NA_SPAN)
            case = jnp.where(blk == 0, 0, jnp.where(blk == n_blk - 1, 2, 1))
            qoff = pl.multiple_of(r0 * GRID_W, nq)
            koff = pl.multiple_of(span0 * GRID_W, GRID_W)
            q = q_ref[pl.ds(qoff, nq), :]
            k = k_ref[pl.ds(koff, nk), :]
            v = v_ref[pl.ds(koff, nk), :]
            s = lax.dot_general(q, k, (((1,), (1,)), ((), ())), preferred_element_type=F32)
            s = s + b_ref[0, case]
            if shift:
                s = s - jnp.max(s, axis=-1, keepdims=True)
            p = jnp.exp(s)
            l = jnp.sum(p, axis=-1, keepdims=True)
            o = jnp.dot(p.astype(BF16), v, preferred_element_type=F32) / l
            o_ref[pl.ds(qoff, nq), :] = o.astype(o_ref.dtype)
            return carry

        lax.fori_loop(0, n_blk, body, 0, unroll=2)

    @pl.when(safe_ref[0] == 1)
    def _():
        run(False)

    @pl.when(safe_ref[0] != 1)
    def _():
        run(True)


def _na_bias_table(rpb):
    qc = jnp.arange(GRID_W)[:, None]
    kc = jnp.arange(GRID_W)[None, :]
    ws = jnp.clip(qc - NA_WIN_W // 2, 0, GRID_W - NA_WIN_W)
    col_ok = (kc >= ws) & (kc < ws + NA_WIN_W)
    dc = jnp.clip(kc - qc + NA_WIN_W - 1, 0, 2 * NA_WIN_W - 2)
    i = jnp.arange(NA_QROWS)
    half = NA_WIN_H // 2
    w_off = jnp.stack([jnp.zeros_like(i), i, jnp.full_like(i, NA_SPAN - NA_WIN_H)])
    c_off = jnp.stack([i, jnp.full_like(i, half), half + i])
    j = jnp.arange(NA_SPAN)[None, None, :] - w_off[:, :, None]
    row_ok = (j >= 0) & (j < NA_WIN_H)
    dr = jnp.clip(j - c_off[:, :, None] + NA_WIN_H - 1, 0, 2 * NA_WIN_H - 2)
    oh_r = jax.nn.one_hot(dr, 2 * NA_WIN_H - 1, dtype=F32)
    oh_c = jax.nn.one_hot(dc, 2 * NA_WIN_W - 1, dtype=F32)
    t = jnp.einsum('hab,xija,qkb->hxiqjk', rpb.astype(F32), oh_r, oh_c, precision=lax.Precision.HIGHEST)
    ok = row_ok[None, :, :, None, :, None] & col_ok[None, None, None, :, None, :]
    t = jnp.where(ok, t, NEG_INF)
    return t.reshape(rpb.shape[0], 3, NA_QROWS * GRID_W, NA_SPAN * GRID_W)


def _na_attention(safe, qkv, bias, out_prev, *, seq, n_batch, row_block_off):
    m = qkv.shape[0]
    rows = seq // GRID_W
    assert rows >= NA_SPAN + NA_QROWS and rows % NA_QROWS == 0
    h = NA_HEADS

    def spec(sec):
        return pl.BlockSpec((seq, NA_HEAD_DIM), lambda hh, b: (row_block_off + b, sec * h + hh))

    in_specs = [pl.BlockSpec(memory_space=pltpu.SMEM), spec(0), spec(1), spec(2),
                pl.BlockSpec((1,) + bias.shape[1:], lambda hh, b: (hh, 0, 0, 0))]
    args = [safe, qkv, qkv, qkv, bias]
    aliases = {}
    if out_prev is not None:
        in_specs.append(pl.BlockSpec(memory_space=pl.ANY))
        args.append(out_prev)
        aliases = {5: 0}
    return pl.pallas_call(
        functools.partial(_na_kernel, rows=rows),
        grid=(h, n_batch),
        in_specs=in_specs,
        out_specs=pl.BlockSpec((seq, NA_HEAD_DIM), lambda hh, b: (row_block_off + b, hh)),
        out_shape=jax.ShapeDtypeStruct((m, h * NA_HEAD_DIM), BF16),
        input_output_aliases=aliases,
        compiler_params=_params(("parallel", "parallel")),
    )(*args)


def _matmul_res_kernel(a_ref, w_ref, *refs, bounds):
    r_refs, o_ref = refs[:-1], refs[-1]
    y = jnp.dot(a_ref[...], w_ref[...], preferred_element_type=F32)

    def add(r_ref):
        o_ref[...] = r_ref[...] + y
    _select_rows(pl.program_id(0), r_refs, bounds, add)


def _matmul_res(a, w, res_list, *, tm):
    m, k = a.shape
    n = w.shape[1]
    r_specs, bounds, n_row_tiles = _split_specs(res_list, tm, n)
    assert n_row_tiles * tm == m
    return pl.pallas_call(
        functools.partial(_matmul_res_kernel, bounds=bounds),
        grid=(n_row_tiles,),
        in_specs=[
            pl.BlockSpec((tm, k), lambda i: (i, 0)),
            pl.BlockSpec((k, n), lambda i: (0, 0)),
        ] + r_specs,
        out_specs=pl.BlockSpec((tm, n), lambda i: (i, 0)),
        out_shape=jax.ShapeDtypeStruct((m, n), F32),
        compiler_params=_params(("parallel",)),
    )(a, w, *res_list)


def _swiglu_step(h, wg, wu, wd):
    a = jnp.dot(h, wg, preferred_element_type=F32)
    b = jnp.dot(h, wu, preferred_element_type=F32)
    t = (a / (1.0 + jnp.exp(-a)) * b).astype(BF16)
    return jnp.dot(t, wd, preferred_element_type=F32)


def _ffn_kernel(x_ref, g_ref, wg_ref, wu_ref, wd_ref, o_ref, h_scr):
    f = pl.program_id(1)

    @pl.when(f == 0)
    def _():
        x = x_ref[...]
        h_scr[...] = _rms(x, g_ref[...]).astype(BF16)
        o_ref[...] = x

    o_ref[...] += _swiglu_step(h_scr[...], wg_ref[...], wu_ref[...], wd_ref[...])


def _ffn(x, g, wg, wu, wd, *, tm, tf):
    m, d = x.shape
    f = wg.shape[1]
    return pl.pallas_call(
        _ffn_kernel,
        grid=(m // tm, f // tf),
        in_specs=[
            pl.BlockSpec((tm, d), lambda i, j: (i, 0)),
            pl.BlockSpec((1, d), lambda i, j: (0, 0)),
            pl.BlockSpec((d, tf), lambda i, j: (0, j)),
            pl.BlockSpec((d, tf), lambda i, j: (0, j)),
            pl.BlockSpec((tf, d), lambda i, j: (j, 0)),
        ],
        out_specs=pl.BlockSpec((tm, d), lambda i, j: (i, 0)),
        out_shape=jax.ShapeDtypeStruct((m, d), F32),
        scratch_shapes=[pltpu.VMEM((tm, d), BF16)],
        compiler_params=_params(("parallel", "arbitrary")),
    )(x, g, wg, wu, wd)


def _expert_ffn_kernel(te_ref, nu_ref, x_ref, g_ref, wg_ref, wu_ref, wd_ref, o_ref, h_scr):
    i = pl.program_id(0)
    f = pl.program_id(1)

    @pl.when(i < nu_ref[0])
    def _():
        @pl.when(f == 0)
        def _():
            h_scr[...] = _rms(x_ref[...], g_ref[...]).astype(BF16)
            o_ref[...] = jnp.zeros_like(o_ref)

        o_ref[...] += _swiglu_step(h_scr[...], wg_ref[0], wu_ref[0], wd_ref[0])


def _expert_ffn(tile_expert, n_used, xs, g, wg, wu, wd, *, tm, tf):
    p, d = xs.shape
    nf = wg.shape[2] // tf

    def row_map(i, j, te, nu):
        return (jnp.minimum(i, nu[0] - 1), 0)

    def wcol_map(i, j, te, nu):
        used = i < nu[0]
        return (te[jnp.minimum(i, nu[0] - 1)], 0, jnp.where(used, j, nf - 1))

    def wrow_map(i, j, te, nu):
        used = i < nu[0]
        return (te[jnp.minimum(i, nu[0] - 1)], jnp.where(used, j, nf - 1), 0)

    grid_spec = pltpu.PrefetchScalarGridSpec(
        num_scalar_prefetch=2,
        grid=(p // tm, nf),
        in_specs=[
            pl.BlockSpec((tm, d), row_map),
            pl.BlockSpec((1, d), lambda i, j, te, nu: (0, 0)),
            pl.BlockSpec((1, d, tf), wcol_map),
            pl.BlockSpec((1, d, tf), wcol_map),
            pl.BlockSpec((1, tf, d), wrow_map),
        ],
        out_specs=pl.BlockSpec((tm, d), row_map),
        scratch_shapes=[pltpu.VMEM((tm, d), BF16)],
    )
    return pl.pallas_call(
        _expert_ffn_kernel,
        grid_spec=grid_spec,
        out_shape=jax.ShapeDtypeStruct((p, d), F32),
        compiler_params=_params(("arbitrary", "arbitrary")),
    )(tile_expert, n_used, xs, g, wg, wu, wd)


ROPE_PAD = LANES


def _rope_pieces(w):
    half = MLA_ROPE // 2
    x1, x2 = w[..., :half], w[..., half:]
    z = jnp.zeros(w.shape[:-1] + (ROPE_PAD - MLA_ROPE,), w.dtype)
    return jnp.concatenate([x1, x2, z], axis=-1), jnp.concatenate([x2, x1, z], axis=-1)


def _mla_proj_kernel(x_ref, g_ref, wd_ref, gq_ref, gkv_ref, gkr_ref, gkrs_ref, wuq_ref, gqn_ref, gqr_ref,
                     gqrs_ref, wukv_ref, gkn_ref, cos_ref, sin_ref,
                     qn_ref, qr_ref, kn_ref, kr_ref, v_ref):
    cos = cos_ref[...]
    sin = sin_ref[...]

    def rope(piece, piece_sw, gain, gain_sw):
        ms = jnp.sum(piece * piece, axis=-1, keepdims=True) * (1.0 / MLA_ROPE)
        return lax.rsqrt(ms + EPS) * (piece * (gain * cos) + piece_sw * (gain_sw * sin))

    h = _rms(x_ref[...], g_ref[...]).astype(BF16)
    lat = jnp.dot(h, wd_ref[...], preferred_element_type=F32)
    cq = _rms(lat[:, :MLA_Q_LORA], gq_ref[...]).astype(BF16)
    ckv = _rms(lat[:, MLA_Q_LORA:MLA_Q_LORA + MLA_KV_LORA], gkv_ref[...]).astype(BF16)
    kr0 = MLA_Q_LORA + MLA_KV_LORA
    kr_ref[...] = rope(lat[:, kr0:kr0 + ROPE_PAD], lat[:, kr0 + ROPE_PAD:],
                       gkr_ref[...], gkrs_ref[...]).astype(BF16)

    hd = MLA_HEADS * LANES
    chunk = 4 * LANES
    for c in range(hd // chunk):
        base = c * chunk
        qn = jnp.dot(cq, wuq_ref[:, base:base + chunk], preferred_element_type=F32)
        qr = jnp.dot(cq, wuq_ref[:, hd + base:hd + base + chunk], preferred_element_type=F32)
        qs = jnp.dot(cq, wuq_ref[:, 2 * hd + base:2 * hd + base + chunk], preferred_element_type=F32)
        kn = jnp.dot(ckv, wukv_ref[:, base:base + chunk], preferred_element_type=F32)
        v = jnp.dot(ckv, wukv_ref[:, hd + base:hd + base + chunk], preferred_element_type=F32)
        v_ref[:, base:base + chunk] = v.astype(BF16)
        for hh in range(chunk // LANES):
            sl = slice(hh * LANES, (hh + 1) * LANES)
            osl = slice(base + hh * LANES, base + (hh + 1) * LANES)
            qn_ref[:, osl] = _rms(qn[:, sl], gqn_ref[...]).astype(BF16)
            kn_ref[:, osl] = _rms(kn[:, sl], gkn_ref[...]).astype(BF16)
            qr_ref[:, osl] = rope(qr[:, sl], qs[:, sl], gqr_ref[...], gqrs_ref[...]).astype(BF16)


def _mla_proj(x, g, wd, gq, gkv, gkr, gkrs, wuq, gqn, gqr, gqrs, wukv, gkn, cos, sin, table_block, *, tm):
    m, d = x.shape
    hd = MLA_HEADS * LANES
    const = lambda i: (0, 0)
    row = lambda i: (i, 0)
    tab = lambda i: (table_block(i), 0)
    return pl.pallas_call(
        _mla_proj_kernel,
        grid=(m // tm,),
        in_specs=[
            pl.BlockSpec((tm, d), row),
            pl.BlockSpec((1, d), const),
            pl.BlockSpec(wd.shape, const, pipeline_mode=pl.Buffered(1)),
            pl.BlockSpec((1, MLA_Q_LORA), const),
            pl.BlockSpec((1, MLA_KV_LORA), const),
            pl.BlockSpec((1, LANES), const),
            pl.BlockSpec((1, LANES), const),
            pl.BlockSpec(wuq.shape, const, pipeline_mode=pl.Buffered(1)),
            pl.BlockSpec((1, LANES), const),
            pl.BlockSpec((1, LANES), const),
            pl.BlockSpec((1, LANES), const),
            pl.BlockSpec(wukv.shape, const, pipeline_mode=pl.Buffered(1)),
            pl.BlockSpec((1, LANES), const),
            pl.BlockSpec((tm, LANES), tab),
            pl.BlockSpec((tm, LANES), tab),
        ],
        out_specs=[
            pl.BlockSpec((tm, hd), row),
            pl.BlockSpec((tm, hd), row),
            pl.BlockSpec((tm, hd), row),
            pl.BlockSpec((tm, LANES), row),
            pl.BlockSpec((tm, hd), row),
        ],
        out_shape=[
            jax.ShapeDtypeStruct((m, hd), BF16),
            jax.ShapeDtypeStruct((m, hd), BF16),
            jax.ShapeDtypeStruct((m, hd), BF16),
            jax.ShapeDtypeStruct((m, LANES), BF16),
            jax.ShapeDtypeStruct((m, hd), BF16),
        ],
        compiler_params=_params(("parallel",)),
    )(x, g, wd, gq, gkv, gkr, gkrs, wuq, gqn, gqr, gqrs, wukv, gkn, cos, sin)


def _rope_tables(seq):
    t = jnp.arange(seq)
    row = (t // GRID_W).astype(F32)
    col = (t % GRID_W).astype(F32)
    n_pairs = MLA_ROPE // 4
    inv = ROPE_THETA ** (-jnp.arange(n_pairs, dtype=F32) / n_pairs)
    ang = jnp.concatenate([row[:, None] * inv, col[:, None] * inv], axis=-1)
    c, s = jnp.cos(ang), jnp.sin(ang)
    z = jnp.zeros_like(c)
    return jnp.concatenate([c, c, z, z], axis=-1), jnp.concatenate([-s, s, z, z], axis=-1)


FLASH_TQ = 1024
FLASH_TK = 2048
FLASH_HEADS = 2
FLASH_SUB_Q = 256


def _flash_kernel(safe_ref, qn_ref, qr_ref, kn_ref, kr_ref, v_ref, *rest, hb, sub_q, nk):
    if nk == 1:
        o_ref = rest[-1]
    else:
        o_ref, m_scr, l_scr, acc_scr = rest[-4:]
        kk = pl.program_id(3)

        @pl.when(kk == 0)
        def _():
            m_scr[...] = jnp.full_like(m_scr, -jnp.inf)
            l_scr[...] = jnp.zeros_like(l_scr)
            acc_scr[...] = jnp.zeros_like(acc_scr)

    tq = qn_ref.shape[0]

    def run(shift):
        kr = kr_ref[...]
        for h in range(hb):
            hs = slice(h * LANES, (h + 1) * LANES)
            k = jnp.concatenate([kn_ref[:, hs], kr], axis=-1)
            v = v_ref[:, hs]
            for qi in range(tq // sub_q):
                rs = slice(qi * sub_q, (qi + 1) * sub_q)
                q = jnp.concatenate([qn_ref[rs, hs], qr_ref[rs, hs]], axis=-1)
                s = lax.dot_general(q, k, (((1,), (1,)), ((), ())), preferred_element_type=F32)
                if not shift:
                    p = jnp.exp(s)
                    l = jnp.sum(p, axis=-1, keepdims=True)
                    pv = jnp.dot(p.astype(BF16), v, preferred_element_type=F32)
                    if nk == 1:
                        o_ref[rs, hs] = (pv / l).astype(o_ref.dtype)
                    else:
                        l_scr[h, rs] += l
                        acc_scr[rs, hs] += pv
                elif nk == 1:
                    p = jnp.exp(s - jnp.max(s, axis=-1, keepdims=True))
                    l = jnp.sum(p, axis=-1, keepdims=True)
                    o = jnp.dot(p.astype(BF16), v, preferred_element_type=F32) / l
                    o_ref[rs, hs] = o.astype(o_ref.dtype)
                else:
                    m_prev = m_scr[h, rs]
                    m_new = jnp.maximum(m_prev, jnp.max(s, axis=-1, keepdims=True))
                    alpha = jnp.exp(m_prev - m_new)
                    p = jnp.exp(s - m_new)
                    l_scr[h, rs] = alpha * l_scr[h, rs] + jnp.sum(p, axis=-1, keepdims=True)
                    acc_scr[rs, hs] = (alpha * acc_scr[rs, hs]
                                       + jnp.dot(p.astype(BF16), v, preferred_element_type=F32))
                    m_scr[h, rs] = m_new

    @pl.when(safe_ref[0] == 1)
    def _():
        run(False)

    @pl.when(safe_ref[0] != 1)
    def _():
        run(True)

    if nk > 1:
        @pl.when(kk == nk - 1)
        def _():
            for h in range(hb):
                hs = slice(h * LANES, (h + 1) * LANES)
                o_ref[:, hs] = (acc_scr[:, hs] / l_scr[h]).astype(o_ref.dtype)


def _flash(safe, qn, qr, kn, kr, v, out_prev, *, seq, n_batch, row_off, tq, tk, hb, sub_q):
    m = qn.shape[0]
    h = MLA_HEADS
    nq, nk = seq // tq, seq // tk
    qoff, koff = row_off // tq, row_off // tk
    qmap = lambda b, hh, i, j: (qoff + b * nq + i, hh)
    kmap = lambda b, hh, i, j: (koff + b * nk + j, hh)
    in_specs = [
        pl.BlockSpec(memory_space=pltpu.SMEM),
        pl.BlockSpec((tq, hb * LANES), qmap),
        pl.BlockSpec((tq, hb * LANES), qmap),
        pl.BlockSpec((tk, hb * LANES), kmap),
        pl.BlockSpec((tk, LANES), lambda b, hh, i, j: (koff + b * nk + j, 0)),
        pl.BlockSpec((tk, hb * LANES), kmap),
    ]
    args = [safe, qn, qr, kn, kr, v]
    aliases = {}
    if out_prev is not None:
        in_specs.append(pl.BlockSpec(memory_space=pl.ANY))
        args.append(out_prev)
        aliases = {6: 0}
    scratch = []
    if nk > 1:
        scratch = [pltpu.VMEM((hb, tq, 1), F32), pltpu.VMEM((hb, tq, 1), F32), pltpu.VMEM((tq, hb * MLA_V), F32)]
    return pl.pallas_call(
        functools.partial(_flash_kernel, hb=hb, sub_q=sub_q, nk=nk),
        grid=(n_batch, h // hb, nq, nk),
        in_specs=in_specs,
        out_specs=pl.BlockSpec((tq, hb * LANES), qmap),
        out_shape=jax.ShapeDtypeStruct((m, h * MLA_V), BF16),
        scratch_shapes=scratch,
        input_output_aliases=aliases,
        compiler_params=_params(("parallel", "parallel", "parallel", "arbitrary")),
    )(*args)


def _router_kernel(x_ref, g_ref, wr_ref, idx_ref, gate_ref):
    h = _rms(x_ref[...], g_ref[...])
    w = wr_ref[...]
    h_hi = h.astype(BF16)
    h_lo = (h - h_hi.astype(F32)).astype(BF16)
    w_hi = w.astype(BF16)
    w_lo = (w - w_hi.astype(F32)).astype(BF16)
    logits = (jnp.dot(h_hi, w_hi, preferred_element_type=F32)
              + (jnp.dot(h_lo, w_hi, preferred_element_type=F32) + jnp.dot(h_hi, w_lo, preferred_element_type=F32)))
    lane = lax.broadcasted_iota(jnp.int32, logits.shape, 1)
    logits = jnp.where(lane < N_EXPERTS, logits, -jnp.inf)
    m1 = jnp.max(logits, axis=-1, keepdims=True)
    i1 = jnp.min(jnp.where(logits == m1, lane, LANES), axis=-1, keepdims=True)
    rest = jnp.where(lane == i1, -jnp.inf, logits)
    m2 = jnp.max(rest, axis=-1, keepdims=True)
    i2 = jnp.min(jnp.where(rest == m2, lane, LANES), axis=-1, keepdims=True)
    e = jnp.exp(m2 - m1)
    g1 = 1.0 / (1.0 + e)
    g2 = e / (1.0 + e)
    idx_ref[...] = jnp.where(lane == 0, i1, jnp.where(lane == 1, i2, 0))
    gate_ref[...] = jnp.where(lane == 0, g1, jnp.where(lane == 1, g2, 0.0))


def _router(x, g, wr, *, tm):
    m, d = x.shape
    return pl.pallas_call(
        _router_kernel,
        grid=(m // tm,),
        in_specs=[
            pl.BlockSpec((tm, d), lambda i: (i, 0)),
            pl.BlockSpec((1, d), lambda i: (0, 0)),
            pl.BlockSpec((d, LANES), lambda i: (0, 0)),
        ],
        out_specs=[pl.BlockSpec((tm, LANES), lambda i: (i, 0)), pl.BlockSpec((tm, LANES), lambda i: (i, 0))],
        out_shape=[jax.ShapeDtypeStruct((m, LANES), jnp.int32), jax.ShapeDtypeStruct((m, LANES), F32)],
        compiler_params=_params(("parallel",)),
    )(x, g, wr)


DMA_UNROLL = 8


def _scatter_kernel(pe_ref, pos_ref, x_ref, o_hbm, zero_scr, sem, zsem, *, tm):
    r = x_ref.shape[0]

    def zero_copy(e):
        end = pe_ref[e]
        return pltpu.make_async_copy(zero_scr, o_hbm.at[pl.ds(pl.multiple_of(end - tm, tm), tm)], zsem)

    def nonempty(e):
        return pe_ref[e] > (pe_ref[e - 1] if e else 0)

    @pl.when(pl.program_id(0) == 0)
    def _():
        zero_scr[...] = jnp.zeros_like(zero_scr)
        for e in range(N_EXPERTS):
            @pl.when(nonempty(e))
            def _(e=e):
                zero_copy(e).start()
        for e in range(N_EXPERTS):
            @pl.when(nonempty(e))
            def _(e=e):
                zero_copy(e).wait()

    def row_copy(j, row):
        return pltpu.make_async_copy(x_ref.at[pl.ds(j, 1)], o_hbm.at[pl.ds(row, 1)], sem)

    def start(j, c):
        for k in range(TOP_K):
            row_copy(j, pos_ref[0, 0, TOP_K * j + k]).start()
        return c

    lax.fori_loop(0, r, start, 0, unroll=DMA_UNROLL)
    for k in range(TOP_K):
        pltpu.make_async_copy(x_ref, o_hbm.at[pl.ds(0, r)], sem).wait()


def _scatter_rows(pad_end, pos, x, p_rows, *, r, tm):
    m, d = x.shape
    grid_spec = pltpu.PrefetchScalarGridSpec(
        num_scalar_prefetch=1,
        grid=(m // r,),
        in_specs=[
            pl.BlockSpec((1, 1, TOP_K * r), lambda i, pe: (i, 0, 0), memory_space=pltpu.SMEM),
            pl.BlockSpec((r, d), lambda i, pe: (i, 0)),
        ],
        out_specs=pl.BlockSpec(memory_space=pl.ANY),
        scratch_shapes=[pltpu.VMEM((tm, d), F32), pltpu.SemaphoreType.DMA(()), pltpu.SemaphoreType.DMA(())],
    )
    return pl.pallas_call(
        functools.partial(_scatter_kernel, tm=tm),
        grid_spec=grid_spec,
        out_shape=jax.ShapeDtypeStruct((p_rows, d), F32),
        compiler_params=_params(("arbitrary",)),
    )(pad_end, pos.reshape(m // r, 1, TOP_K * r), x)


def _combine_kernel(pos_ref, pos_next_ref, x_ref, gate_ref, y_hbm, *rest, bounds):
    o_refs = rest[:len(bounds)]
    buf, sems = rest[len(bounds):]
    r = x_ref.shape[0]
    i = pl.program_id(0)
    slot = i % 2

    def issue(p_ref, s):
        def start(j, c):
            for k in range(TOP_K):
                pltpu.make_async_copy(y_hbm.at[pl.ds(p_ref[0, 0, TOP_K * j + k], 1)],
                                      buf.at[s, k, pl.ds(j, 1)], sems.at[s]).start()
            return c
        lax.fori_loop(0, r, start, 0, unroll=DMA_UNROLL)

    @pl.when(i == 0)
    def _():
        issue(pos_ref, 0)

    @pl.when(i + 1 < pl.num_programs(0))
    def _():
        issue(pos_next_ref, 1 - slot)

    for k in range(TOP_K):
        pltpu.make_async_copy(y_hbm.at[pl.ds(0, r)], buf.at[slot, k], sems.at[slot]).wait()
    gate = gate_ref[...]
    out = x_ref[...] + (gate[:, 0:1] * buf[slot, 0] + gate[:, 1:2] * buf[slot, 1])

    def write(o_ref):
        o_ref[...] = out
    _select_rows(i, o_refs, bounds, write)


def _combine(pos, x, gates, y, out_rows, *, r):
    m, d = x.shape
    out_shape = [jax.ShapeDtypeStruct((n, d), F32) for n in out_rows]
    o_specs, bounds, n_row_tiles = _split_specs(out_shape, r, d)
    assert n_row_tiles * r == m
    pos3 = pos.reshape(n_row_tiles, 1, TOP_K * r)
    return pl.pallas_call(
        functools.partial(_combine_kernel, bounds=bounds),
        grid=(n_row_tiles,),
        in_specs=[
            pl.BlockSpec((1, 1, TOP_K * r), lambda i: (i, 0, 0), memory_space=pltpu.SMEM),
            pl.BlockSpec((1, 1, TOP_K * r), lambda i: (jnp.minimum(i + 1, n_row_tiles - 1), 0, 0),
                         memory_space=pltpu.SMEM),
            pl.BlockSpec((r, d), lambda i: (i, 0)),
            pl.BlockSpec((r, LANES), lambda i: (i, 0)),
            pl.BlockSpec(memory_space=pl.ANY),
        ],
        out_specs=o_specs,
        out_shape=out_shape,
        scratch_shapes=[pltpu.VMEM((2, TOP_K, r, d), F32), pltpu.SemaphoreType.DMA((2,))],
        compiler_params=_params(("arbitrary",)),
    )(pos3, pos3, x, gates, y)


def _route_plan(idx, n_tokens, tm):
    n_pairs = n_tokens * TOP_K
    n_tiles = n_pairs // tm + N_EXPERTS
    e_flat = idx[:, :TOP_K].reshape(n_pairs)
    onehot = (e_flat[:, None] == jnp.arange(N_EXPERTS)[None, :]).astype(jnp.int32)
    csum = jnp.cumsum(onehot, axis=0)
    counts = csum[-1]
    padded = ((counts + tm - 1) // tm) * tm
    pad_end = jnp.cumsum(padded)
    pad_start = pad_end - padded
    pos = jnp.sum((csum - onehot + pad_start[None, :]) * onehot, axis=1)
    tile_start = jnp.arange(n_tiles) * tm
    tile_expert = jnp.minimum(jnp.sum(tile_start[:, None] >= pad_end[None, :], axis=1), N_EXPERTS - 1)
    n_used = pad_end[-1:] // tm
    return (pos.astype(jnp.int32), pad_end.astype(jnp.int32), tile_expert.astype(jnp.int32),
            n_used.astype(jnp.int32))


def _trunk(xs, seqs, mix_norm, ffn_norm, na_w_qkv, na_q_gain, na_k_gain, na_rpb, na_w_o,
           mla_w_dqkv, mla_q_lora_gain, mla_kv_lora_gain, mla_w_uq, mla_w_ukv,
           mla_qn_gain, mla_qr_gain, mla_kn_gain, mla_kr_gain, mla_w_o,
           ffn_w_gate, ffn_w_up, ffn_w_down, moe_w_router, moe_w_gate, moe_w_up, moe_w_down):
    m = sum(x.shape[0] for x in xs)
    row2 = lambda a: a.reshape(1, -1).astype(F32)
    na_dim = NA_HEADS * NA_HEAD_DIM

    head_gain = jnp.concatenate([
        jnp.tile(na_q_gain[0] * NA_HEAD_DIM ** -0.5, NA_HEADS),
        jnp.tile(na_k_gain[0], NA_HEADS),
        jnp.ones((na_dim,), F32)]).reshape(1, -1)
    qkv = _norm_matmul(xs, row2(mix_norm[0]), na_w_qkv[0].astype(BF16), head_gain, 2 * na_dim,
                       tm=min(1024, min(x.shape[0] for x in xs)), tn=512)
    bias = _na_bias_table(na_rpb[0])
    amax = lambda a: jnp.max(jnp.abs(a.astype(F32)))
    na_bound = (NA_HEAD_DIM ** 0.5 * amax(na_q_gain[0]) * amax(na_k_gain[0]) + amax(na_rpb[0]))
    na_safe = _safe_flag(na_bound)
    attn = None
    off = 0
    for seq, nb in seqs:
        attn = _na_attention(na_safe, qkv, bias, attn, seq=seq, n_batch=nb, row_block_off=off // seq)
        off += seq * nb
    x = _matmul_res(attn, na_w_o[0].astype(BF16), xs, tm=min(512, m))

    x = _ffn(x, row2(ffn_norm[0]), ffn_w_gate[0].astype(BF16), ffn_w_up[0].astype(BF16),
             ffn_w_down[0].astype(BF16), tm=min(512, m), tf=512)

    scale = (MLA_NOPE + MLA_ROPE) ** -0.5
    lat_w = MLA_Q_LORA + MLA_KV_LORA
    wd = jnp.concatenate((mla_w_dqkv[0][:, :lat_w],) + _rope_pieces(mla_w_dqkv[0][:, lat_w:]), axis=1).astype(BF16)
    wuq = mla_w_uq[0].reshape(MLA_Q_LORA, MLA_HEADS, MLA_NOPE + MLA_ROPE)
    wuq = jnp.concatenate(
        [wuq[:, :, :MLA_NOPE].reshape(MLA_Q_LORA, -1)]
        + [p.reshape(MLA_Q_LORA, -1) for p in _rope_pieces(wuq[:, :, MLA_NOPE:])], axis=1).astype(BF16)
    wukv = mla_w_ukv[0].reshape(MLA_KV_LORA, MLA_HEADS, MLA_NOPE + MLA_V)
    wukv = jnp.concatenate([wukv[:, :, :MLA_NOPE].reshape(MLA_KV_LORA, -1),
                            wukv[:, :, MLA_NOPE:].reshape(MLA_KV_LORA, -1)], axis=1).astype(BF16)
    gain_pieces = lambda gg: [p.reshape(1, -1).astype(F32) for p in _rope_pieces(gg)]
    max_seq = max(s for s, _ in seqs)
    cos, sin = _rope_tables(max_seq)
    tm_p = min(512, m)
    bounds = []
    off = 0
    for seq, nb in seqs:
        bounds.append((off // tm_p, seq // tm_p))
        off += seq * nb

    def table_block(i):
        blk = i
        for start, per in bounds:
            blk = jnp.where(i >= start, (i - start) % per, blk)
        return blk

    qn, qr, kn, kr, v = _mla_proj(
        x, row2(mix_norm[1]), wd, row2(mla_q_lora_gain[0]), row2(mla_kv_lora_gain[0]),
        *gain_pieces(mla_kr_gain[0]), wuq, row2(mla_qn_gain[0] * scale), *gain_pieces(mla_qr_gain[0] * scale),
        wukv, row2(mla_kn_gain[0]), cos, sin, table_block, tm=tm_p)
    mla_bound = scale * (MLA_NOPE * amax(mla_qn_gain[0]) * amax(mla_kn_gain[0])
                         + MLA_ROPE * amax(mla_qr_gain[0]) * amax(mla_kr_gain[0]))
    mla_safe = _safe_flag(mla_bound)
    attn = None
    off = 0
    for seq, nb in seqs:
        tq = min(FLASH_TQ, seq)
        attn = _flash(mla_safe, qn, qr, kn, kr, v, attn, seq=seq, n_batch=nb, row_off=off,
                      tq=tq, tk=min(FLASH_TK, seq), hb=FLASH_HEADS, sub_q=min(FLASH_SUB_Q, tq))
        off += seq * nb
    x = _matmul_res(attn, mla_w_o[0].astype(BF16), [x], tm=min(512, m))

    wr = jnp.pad(moe_w_router[0], ((0, 0), (0, LANES - N_EXPERTS))).astype(F32)
    idx, gates = _router(x, row2(ffn_norm[1]), wr, tm=min(512, m))
    tm_e = min(512, m)
    pos, pad_end, tile_expert, n_used = _route_plan(idx, m, tm_e)
    r = min(512, min(xx.shape[0] for xx in xs))
    x_sorted = _scatter_rows(pad_end, pos, x, tile_expert.shape[0] * tm_e, r=r, tm=tm_e)
    y_sorted = _expert_ffn(tile_expert, n_used, x_sorted, row2(ffn_norm[1]), moe_w_gate[0].astype(BF16),
                           moe_w_up[0].astype(BF16), moe_w_down[0].astype(BF16), tm=tm_e, tf=512)
    return _combine(pos, x, gates, y_sorted, [xx.shape[0] for xx in xs], r=r)


def kernel(x_prompt, x_sample, mix_norm, ffn_norm, na_w_qkv, na_q_gain, na_k_gain, na_rpb, na_w_o, mla_w_dqkv, mla_q_lora_gain, mla_kv_lora_gain, mla_w_uq, mla_w_ukv, mla_qn_gain, mla_qr_gain, mla_kn_gain, mla_kr_gain, mla_w_o, ffn_w_gate, ffn_w_up, ffn_w_down, moe_w_router, moe_w_gate, moe_w_up, moe_w_down):
    bp, sp, d = x_prompt.shape
    bs, ss, _ = x_sample.shape
    yp, ys = _trunk([x_prompt.reshape(bp * sp, d), x_sample.reshape(bs * ss, d)], [(sp, bp), (ss, bs)],
                    mix_norm, ffn_norm, na_w_qkv, na_q_gain, na_k_gain, na_rpb, na_w_o,
                    mla_w_dqkv, mla_q_lora_gain, mla_kv_lora_gain, mla_w_uq, mla_w_ukv,
                    mla_qn_gain, mla_qr_gain, mla_kn_gain, mla_kr_gain, mla_w_o,
                    ffn_w_gate, ffn_w_up, ffn_w_down, moe_w_router, moe_w_gate, moe_w_up, moe_w_down)
    return (yp.reshape(bp, sp, d), ys.reshape(bs, ss, d))
```

```python
import functools

import jax
import jax.numpy as jnp
from jax import lax
from jax.experimental import pallas as pl
from jax.experimental.pallas import tpu as pltpu

F32 = jnp.float32
BF16 = jnp.bfloat16

D_MODEL = 2048
GRID_W = 64
NA_HEADS = 16
NA_HEAD_DIM = 128
NA_WIN_H = 8
NA_WIN_W = 16
MLA_HEADS = 16
MLA_Q_LORA = 512
MLA_KV_LORA = 512
MLA_NOPE = 128
MLA_ROPE = 64
MLA_V = 128
ROPE_THETA = 10000.0
D_FF = 5632
N_EXPERTS = 8
TOP_K = 2
EPS = 1e-6
NEG_INF = -1e30

LANES = 128
VMEM_LIMIT = 56 * 1024 * 1024


def _params(sem):
    return pltpu.CompilerParams(dimension_semantics=sem, vmem_limit_bytes=VMEM_LIMIT)


def _rms(x, g):
    return x * lax.rsqrt(jnp.mean(x * x, axis=-1, keepdims=True) + EPS) * g


def _select_rows(i, refs, bounds, fn):
    for ref, (lo, hi) in zip(refs, bounds):
        @pl.when((i >= lo) & (i < hi))
        def _(ref=ref):
            fn(ref)


def _split_specs(xs, tm, ncols):
    specs, bounds, lo = [], [], 0
    for x in xs:
        n = x.shape[0] // tm
        assert n * tm == x.shape[0]
        specs.append(pl.BlockSpec(
            (tm, ncols), functools.partial(lambda i, *_, lo, n: (jnp.clip(i - lo, 0, n - 1), 0), lo=lo, n=n)))
        bounds.append((lo, lo + n))
        lo += n
    return specs, tuple(bounds), lo


def _norm_matmul_kernel(*refs, n_norm_tiles, bounds):
    nx = len(bounds)
    x_refs = refs[:nx]
    g_ref, w_ref, hg_ref, o_ref, h_scr = refs[nx:]
    j = pl.program_id(1)

    @pl.when(j == 0)
    def _():
        def fill(x_ref):
            h_scr[...] = _rms(x_ref[...], g_ref[...]).astype(BF16)
        _select_rows(pl.program_id(0), x_refs, bounds, fill)

    tn = o_ref.shape[1]
    chunk = min(tn, 2 * LANES)

    def emit(normed):
        for c in range(tn // chunk):
            acc = jnp.dot(h_scr[...], w_ref[:, c * chunk:(c + 1) * chunk], preferred_element_type=F32)
            for g in range(chunk // LANES):
                sl = slice(c * chunk + g * LANES, c * chunk + (g + 1) * LANES)
                a = acc[:, g * LANES:(g + 1) * LANES]
                o_ref[:, sl] = (_rms(a, hg_ref[:, sl]) if normed else a).astype(o_ref.dtype)

    @pl.when(j < n_norm_tiles)
    def _():
        emit(True)

    @pl.when(j >= n_norm_tiles)
    def _():
        emit(False)


def _norm_matmul(xs, g, w, head_gain, n_norm_cols, *, tm, tn):
    d = xs[0].shape[1]
    n = w.shape[1]
    x_specs, bounds, n_row_tiles = _split_specs(xs, tm, d)
    return pl.pallas_call(
        functools.partial(_norm_matmul_kernel, n_norm_tiles=n_norm_cols // tn, bounds=bounds),
        grid=(n_row_tiles, n // tn),
        in_specs=x_specs + [
            pl.BlockSpec((1, d), lambda i, j: (0, 0)),
            pl.BlockSpec((d, tn), lambda i, j: (0, j)),
            pl.BlockSpec((1, tn), lambda i, j: (0, j)),
        ],
        out_specs=pl.BlockSpec((tm, tn), lambda i, j: (i, j)),
        out_shape=jax.ShapeDtypeStruct((n_row_tiles * tm, n), BF16),
        scratch_shapes=[pltpu.VMEM((tm, d), BF16)],
        compiler_params=_params(("parallel", "arbitrary")),
    )(*xs, g, w, head_gain)


NA_QROWS = 4
NA_SPAN = NA_WIN_H + NA_QROWS


UNSHIFTED_SOFTMAX_BOUND = 60.0


def _safe_flag(bound):
    return (bound * 1.02 <= UNSHIFTED_SOFTMAX_BOUND).astype(jnp.int32).reshape(1)


def _na_kernel(safe_ref, q_ref, k_ref, v_ref, b_ref, *rest, rows):
    o_ref = rest[-1]
    nq = NA_QROWS * GRID_W
    nk = NA_SPAN * GRID_W
    n_blk = rows // NA_QROWS

    def run(shift):
        def body(blk, carry):
            r0 = blk * NA_QROWS
            span0 = jnp.clip(r0 - NA_WIN_H // 2, 0, rows - NA_SPAN)
            case = jnp.where(blk == 0, 0, jnp.where(blk == n_blk - 1, 2, 1))
            qoff = pl.multiple_of(r0 * GRID_W, nq)
            koff = pl.multiple_of(span0 * GRID_W, GRID_W)
            q = q_ref[pl.ds(qoff, nq), :]
            k = k_ref[pl.ds(koff, nk), :]
            v = v_ref[pl.ds(koff, nk), :]
            s = lax.dot_general(q, k, (((1,), (1,)), ((), ())), preferred_element_type=F32)
            s = s + b_ref[0, case]
            if shift:
                s = s - jnp.max(s, axis=-1, keepdims=True)
            p = jnp.exp(s)
            l = jnp.sum(p, axis=-1, keepdims=True)
            o = jnp.dot(p.astype(BF16), v, preferred_element_type=F32) / l
            o_ref[pl.ds(qoff, nq), :] = o.astype(o_ref.dtype)
            return carry

        lax.fori_loop(0, n_blk, body, 0, unroll=8)

    @pl.when(safe_ref[0] == 1)
    def _():
        run(False)

    @pl.when(safe_ref[0] != 1)
    def _():
        run(True)


def _na_bias_table(rpb):
    qc = jnp.arange(GRID_W)[:, None]
    kc = jnp.arange(GRID_W)[None, :]
    ws = jnp.clip(qc - NA_WIN_W // 2, 0, GRID_W - NA_WIN_W)
    col_ok = (kc >= ws) & (kc < ws + NA_WIN_W)
    dc = jnp.clip(kc - qc + NA_WIN_W - 1, 0, 2 * NA_WIN_W - 2)
    i = jnp.arange(NA_QROWS)
    half = NA_WIN_H // 2
    w_off = jnp.stack([jnp.zeros_like(i), i, jnp.full_like(i, NA_SPAN - NA_WIN_H)])
    c_off = jnp.stack([i, jnp.full_like(i, half), half + i])
    j = jnp.arange(NA_SPAN)[None, None, :] - w_off[:, :, None]
    row_ok = (j >= 0) & (j < NA_WIN_H)
    dr = jnp.clip(j - c_off[:, :, None] + NA_WIN_H - 1, 0, 2 * NA_WIN_H - 2)
    oh_r = jax.nn.one_hot(dr, 2 * NA_WIN_H - 1, dtype=F32)
    oh_c = jax.nn.one_hot(dc, 2 * NA_WIN_W - 1, dtype=F32)
    t = jnp.einsum('hab,xija,qkb->hxiqjk', rpb.astype(F32), oh_r, oh_c, precision=lax.Precision.HIGHEST)
    ok = row_ok[None, :, :, None, :, None] & col_ok[None, None, None, :, None, :]
    t = jnp.where(ok, t, NEG_INF)
    return t.reshape(rpb.shape[0], 3, NA_QROWS * GRID_W, NA_SPAN * GRID_W)


def _na_attention(safe, qkv, bias, out_prev, *, seq, n_batch, row_block_off):
    m = qkv.shape[0]
    rows = seq // GRID_W
    assert rows >= NA_SPAN + NA_QROWS and rows % NA_QROWS == 0
    h = NA_HEADS

    def spec(sec):
        return pl.BlockSpec((seq, NA_HEAD_DIM), lambda hh, b: (row_block_off + b, sec * h + hh))

    in_specs = [pl.BlockSpec(memory_space=pltpu.SMEM), spec(0), spec(1), spec(2),
                pl.BlockSpec((1,) + bias.shape[1:], lambda hh, b: (hh, 0, 0, 0))]
    args = [safe, qkv, qkv, qkv, bias]
    aliases = {}
    if out_prev is not None:
        in_specs.append(pl.BlockSpec(memory_space=pl.ANY))
        args.append(out_prev)
        aliases = {5: 0}
    return pl.pallas_call(
        functools.partial(_na_kernel, rows=rows),
        grid=(h, n_batch),
        in_specs=in_specs,
        out_specs=pl.BlockSpec((seq, NA_HEAD_DIM), lambda hh, b: (row_block_off + b, hh)),
        out_shape=jax.ShapeDtypeStruct((m, h * NA_HEAD_DIM), BF16),
        input_output_aliases=aliases,
        compiler_params=_params(("parallel", "parallel")),
    )(*args)


def _matmul_res_kernel(a_ref, w_ref, *refs, bounds):
    r_refs, o_ref = refs[:-1], refs[-1]
    y = jnp.dot(a_ref[...], w_ref[...], preferred_element_type=F32)

    def add(r_ref):
        o_ref[...] = r_ref[...] + y
    _select_rows(pl.program_id(0), r_refs, bounds, add)


def _matmul_res(a, w, res_list, *, tm):
    m, k = a.shape
    n = w.shape[1]
    r_specs, bounds, n_row_tiles = _split_specs(res_list, tm, n)
    assert n_row_tiles * tm == m
    return pl.pallas_call(
        functools.partial(_matmul_res_kernel, bounds=bounds),
        grid=(n_row_tiles,),
        in_specs=[
            pl.BlockSpec((tm, k), lambda i: (i, 0)),
            pl.BlockSpec((k, n), lambda i: (0, 0)),
        ] + r_specs,
        out_specs=pl.BlockSpec((tm, n), lambda i: (i, 0)),
        out_shape=jax.ShapeDtypeStruct((m, n), F32),
        compiler_params=_params(("parallel",)),
    )(a, w, *res_list)


def _swiglu_step(h, wg_ref, wu_ref, wd_ref, lead=()):
    tf = wd_ref.shape[-2]
    chunk = min(tf, 2 * LANES)
    out = None
    for c in range(tf // chunk):
        cs = slice(c * chunk, (c + 1) * chunk)
        a = jnp.dot(h, wg_ref[lead + (slice(None), cs)], preferred_element_type=F32)
        b = jnp.dot(h, wu_ref[lead + (slice(None), cs)], preferred_element_type=F32)
        t = (a / (1.0 + jnp.exp(-a)) * b).astype(BF16)
        y = jnp.dot(t, wd_ref[lead + (cs, slice(None))], preferred_element_type=F32)
        out = y if out is None else out + y
    return out


def _ffn_kernel(x_ref, g_ref, wg_ref, wu_ref, wd_ref, o_ref, h_scr):
    f = pl.program_id(1)

    @pl.when(f == 0)
    def _():
        x = x_ref[...]
        h_scr[...] = _rms(x, g_ref[...]).astype(BF16)
        o_ref[...] = x

    o_ref[...] += _swiglu_step(h_scr[...], wg_ref, wu_ref, wd_ref)


def _ffn(x, g, wg, wu, wd, *, tm, tf):
    m, d = x.shape
    f = wg.shape[1]
    return pl.pallas_call(
        _ffn_kernel,
        grid=(m // tm, f // tf),
        in_specs=[
            pl.BlockSpec((tm, d), lambda i, j: (i, 0)),
            pl.BlockSpec((1, d), lambda i, j: (0, 0)),
            pl.BlockSpec((d, tf), lambda i, j: (0, j)),
            pl.BlockSpec((d, tf), lambda i, j: (0, j)),
            pl.BlockSpec((tf, d), lambda i, j: (j, 0)),
        ],
        out_specs=pl.BlockSpec((tm, d), lambda i, j: (i, 0)),
        out_shape=jax.ShapeDtypeStruct((m, d), F32),
        scratch_shapes=[pltpu.VMEM((tm, d), BF16)],
        compiler_params=_params(("parallel", "arbitrary")),
    )(x, g, wg, wu, wd)


def _expert_ffn_kernel(te_ref, nu_ref, x_ref, g_ref, wg_ref, wu_ref, wd_ref, o_ref, h_scr):
    i = pl.program_id(0)
    f = pl.program_id(1)

    @pl.when(i < nu_ref[0])
    def _():
        @pl.when(f == 0)
        def _():
            h_scr[...] = _rms(x_ref[...], g_ref[...]).astype(BF16)
            o_ref[...] = jnp.zeros_like(o_ref)

        o_ref[...] += _swiglu_step(h_scr[...], wg_ref, wu_ref, wd_ref, lead=(0,))


def _expert_ffn(tile_expert, n_used, xs, g, wg, wu, wd, *, tm, tf):
    p, d = xs.shape
    nf = wg.shape[2] // tf

    def row_map(i, j, te, nu):
        return (jnp.minimum(i, nu[0] - 1), 0)

    def wcol_map(i, j, te, nu):
        used = i < nu[0]
        return (te[jnp.minimum(i, nu[0] - 1)], 0, jnp.where(used, j, nf - 1))

    def wrow_map(i, j, te, nu):
        used = i < nu[0]
        return (te[jnp.minimum(i, nu[0] - 1)], jnp.where(used, j, nf - 1), 0)

    grid_spec = pltpu.PrefetchScalarGridSpec(
        num_scalar_prefetch=2,
        grid=(p // tm, nf),
        in_specs=[
            pl.BlockSpec((tm, d), row_map),
            pl.BlockSpec((1, d), lambda i, j, te, nu: (0, 0)),
            pl.BlockSpec((1, d, tf), wcol_map),
            pl.BlockSpec((1, d, tf), wcol_map),
            pl.BlockSpec((1, tf, d), wrow_map),
        ],
        out_specs=pl.BlockSpec((tm, d), row_map),
        scratch_shapes=[pltpu.VMEM((tm, d), BF16)],
    )
    return pl.pallas_call(
        _expert_ffn_kernel,
        grid_spec=grid_spec,
        out_shape=jax.ShapeDtypeStruct((p, d), F32),
        compiler_params=_params(("arbitrary", "arbitrary")),
    )(tile_expert, n_used, xs, g, wg, wu, wd)


ROPE_PAD = LANES


def _rope_pieces(w):
    half = MLA_ROPE // 2
    x1, x2 = w[..., :half], w[..., half:]
    z = jnp.zeros(w.shape[:-1] + (ROPE_PAD - MLA_ROPE,), w.dtype)
    return jnp.concatenate([x1, x2, z], axis=-1), jnp.concatenate([x2, x1, z], axis=-1)


def _mla_proj_kernel(x_ref, g_ref, wd_ref, gq_ref, gkv_ref, gkr_ref, gkrs_ref, wuq_ref, gqn_ref, gqr_ref,
                     gqrs_ref, wukv_ref, gkn_ref, cos_ref, sin_ref,
                     qn_ref, qr_ref, kn_ref, kr_ref, v_ref):
    cos = cos_ref[...]
    sin = sin_ref[...]

    def rope(piece, piece_sw, gain, gain_sw):
        ms = jnp.sum(piece * piece, axis=-1, keepdims=True) * (1.0 / MLA_ROPE)
        return lax.rsqrt(ms + EPS) * (piece * (gain * cos) + piece_sw * (gain_sw * sin))

    h = _rms(x_ref[...], g_ref[...]).astype(BF16)
    lat = jnp.dot(h, wd_ref[...], preferred_element_type=F32)
    cq = _rms(lat[:, :MLA_Q_LORA], gq_ref[...]).astype(BF16)
    ckv = _rms(lat[:, MLA_Q_LORA:MLA_Q_LORA + MLA_KV_LORA], gkv_ref[...]).astype(BF16)
    kr0 = MLA_Q_LORA + MLA_KV_LORA
    kr_ref[...] = rope(lat[:, kr0:kr0 + ROPE_PAD], lat[:, kr0 + ROPE_PAD:],
                       gkr_ref[...], gkrs_ref[...]).astype(BF16)

    hd = MLA_HEADS * LANES
    chunk = 4 * LANES
    for c in range(hd // chunk):
        base = c * chunk
        qn = jnp.dot(cq, wuq_ref[:, base:base + chunk], preferred_element_type=F32)
        qr = jnp.dot(cq, wuq_ref[:, hd + base:hd + base + chunk], preferred_element_type=F32)
        qs = jnp.dot(cq, wuq_ref[:, 2 * hd + base:2 * hd + base + chunk], preferred_element_type=F32)
        kn = jnp.dot(ckv, wukv_ref[:, base:base + chunk], preferred_element_type=F32)
        v = jnp.dot(ckv, wukv_ref[:, hd + base:hd + base + chunk], preferred_element_type=F32)
        v_ref[:, base:base + chunk] = v.astype(BF16)
        for hh in range(chunk // LANES):
            sl = slice(hh * LANES, (hh + 1) * LANES)
            osl = slice(base + hh * LANES, base + (hh + 1) * LANES)
            qn_ref[:, osl] = _rms(qn[:, sl], gqn_ref[...]).astype(BF16)
            kn_ref[:, osl] = _rms(kn[:, sl], gkn_ref[...]).astype(BF16)
            qr_ref[:, osl] = rope(qr[:, sl], qs[:, sl], gqr_ref[...], gqrs_ref[...]).astype(BF16)


def _mla_proj(x, g, wd, gq, gkv, gkr, gkrs, wuq, gqn, gqr, gqrs, wukv, gkn, cos, sin, table_block, *, tm):
    m, d = x.shape
    hd = MLA_HEADS * LANES
    const = lambda i: (0, 0)
    row = lambda i: (i, 0)
    tab = lambda i: (table_block(i), 0)
    return pl.pallas_call(
        _mla_proj_kernel,
        grid=(m // tm,),
        in_specs=[
            pl.BlockSpec((tm, d), row),
            pl.BlockSpec((1, d), const),
            pl.BlockSpec(wd.shape, const, pipeline_mode=pl.Buffered(1)),
            pl.BlockSpec((1, MLA_Q_LORA), const),
            pl.BlockSpec((1, MLA_KV_LORA), const),
            pl.BlockSpec((1, LANES), const),
            pl.BlockSpec((1, LANES), const),
            pl.BlockSpec(wuq.shape, const, pipeline_mode=pl.Buffered(1)),
            pl.BlockSpec((1, LANES), const),
            pl.BlockSpec((1, LANES), const),
            pl.BlockSpec((1, LANES), const),
            pl.BlockSpec(wukv.shape, const, pipeline_mode=pl.Buffered(1)),
            pl.BlockSpec((1, LANES), const),
            pl.BlockSpec((tm, LANES), tab),
            pl.BlockSpec((tm, LANES), tab),
        ],
        out_specs=[
            pl.BlockSpec((tm, hd), row),
            pl.BlockSpec((tm, hd), row),
            pl.BlockSpec((tm, hd), row),
            pl.BlockSpec((tm, LANES), row),
            pl.BlockSpec((tm, hd), row),
        ],
        out_shape=[
            jax.ShapeDtypeStruct((m, hd), BF16),
            jax.ShapeDtypeStruct((m, hd), BF16),
            jax.ShapeDtypeStruct((m, hd), BF16),
            jax.ShapeDtypeStruct((m, LANES), BF16),
            jax.ShapeDtypeStruct((m, hd), BF16),
        ],
        compiler_params=_params(("parallel",)),
    )(x, g, wd, gq, gkv, gkr, gkrs, wuq, gqn, gqr, gqrs, wukv, gkn, cos, sin)


def _rope_tables(seq):
    t = jnp.arange(seq)
    row = (t // GRID_W).astype(F32)
    col = (t % GRID_W).astype(F32)
    n_pairs = MLA_ROPE // 4
    inv = ROPE_THETA ** (-jnp.arange(n_pairs, dtype=F32) / n_pairs)
    ang = jnp.concatenate([row[:, None] * inv, col[:, None] * inv], axis=-1)
    c, s = jnp.cos(ang), jnp.sin(ang)
    z = jnp.zeros_like(c)
    return jnp.concatenate([c, c, z, z], axis=-1), jnp.concatenate([-s, s, z, z], axis=-1)


FLASH_TQ = 1024
FLASH_TK = 2048
FLASH_HEADS = 2
FLASH_SUB_Q = 256


def _flash_kernel(safe_ref, qn_ref, qr_ref, kn_ref, kr_ref, v_ref, *rest, hb, sub_q, nk):
    if nk == 1:
        o_ref = rest[-1]
    else:
        o_ref, m_scr, l_scr, acc_scr = rest[-4:]
        kk = pl.program_id(3)

        @pl.when(kk == 0)
        def _():
            m_scr[...] = jnp.full_like(m_scr, -jnp.inf)
            l_scr[...] = jnp.zeros_like(l_scr)
            acc_scr[...] = jnp.zeros_like(acc_scr)

    tq = qn_ref.shape[0]

    def run(shift):
        kr = kr_ref[...]
        for h in range(hb):
            hs = slice(h * LANES, (h + 1) * LANES)
            k = jnp.concatenate([kn_ref[:, hs], kr], axis=-1)
            v = v_ref[:, hs]
            for qi in range(tq // sub_q):
                rs = slice(qi * sub_q, (qi + 1) * sub_q)
                q = jnp.concatenate([qn_ref[rs, hs], qr_ref[rs, hs]], axis=-1)
                s = lax.dot_general(q, k, (((1,), (1,)), ((), ())), preferred_element_type=F32)
                if not shift:
                    p = jnp.exp(s)
                    l = jnp.sum(p, axis=-1, keepdims=True)
                    pv = jnp.dot(p.astype(BF16), v, preferred_element_type=F32)
                    if nk == 1:
                        o_ref[rs, hs] = (pv / l).astype(o_ref.dtype)
                    else:
                        l_scr[h, rs] += l
                        acc_scr[rs, hs] += pv
                elif nk == 1:
                    p = jnp.exp(s - jnp.max(s, axis=-1, keepdims=True))
                    l = jnp.sum(p, axis=-1, keepdims=True)
                    o = jnp.dot(p.astype(BF16), v, preferred_element_type=F32) / l
                    o_ref[rs, hs] = o.astype(o_ref.dtype)
                else:
                    m_prev = m_scr[h, rs]
                    m_new = jnp.maximum(m_prev, jnp.max(s, axis=-1, keepdims=True))
                    alpha = jnp.exp(m_prev - m_new)
                    p = jnp.exp(s - m_new)
                    l_scr[h, rs] = alpha * l_scr[h, rs] + jnp.sum(p, axis=-1, keepdims=True)
                    acc_scr[rs, hs] = (alpha * acc_scr[rs, hs]
                                       + jnp.dot(p.astype(BF16), v, preferred_element_type=F32))
                    m_scr[h, rs] = m_new

    @pl.when(safe_ref[0] == 1)
    def _():
        run(False)

    @pl.when(safe_ref[0] != 1)
    def _():
        run(True)

    if nk > 1:
        @pl.when(kk == nk - 1)
        def _():
            for h in range(hb):
                hs = slice(h * LANES, (h + 1) * LANES)
                o_ref[:, hs] = (acc_scr[:, hs] / l_scr[h]).astype(o_ref.dtype)


def _flash(safe, qn, qr, kn, kr, v, out_prev, *, seq, n_batch, row_off, tq, tk, hb, sub_q):
    m = qn.shape[0]
    h = MLA_HEADS
    nq, nk = seq // tq, seq // tk
    qoff, koff = row_off // tq, row_off // tk
    qmap = lambda b, hh, i, j: (qoff + b * nq + i, hh)
    kmap = lambda b, hh, i, j: (koff + b * nk + j, hh)
    in_specs = [
        pl.BlockSpec(memory_space=pltpu.SMEM),
        pl.BlockSpec((tq, hb * LANES), qmap),
        pl.BlockSpec((tq, hb * LANES), qmap),
        pl.BlockSpec((tk, hb * LANES), kmap),
        pl.BlockSpec((tk, LANES), lambda b, hh, i, j: (koff + b * nk + j, 0)),
        pl.BlockSpec((tk, hb * LANES), kmap),
    ]
    args = [safe, qn, qr, kn, kr, v]
    aliases = {}
    if out_prev is not None:
        in_specs.append(pl.BlockSpec(memory_space=pl.ANY))
        args.append(out_prev)
        aliases = {6: 0}
    scratch = []
    if nk > 1:
        scratch = [pltpu.VMEM((hb, tq, 1), F32), pltpu.VMEM((hb, tq, 1), F32), pltpu.VMEM((tq, hb * MLA_V), F32)]
    return pl.pallas_call(
        functools.partial(_flash_kernel, hb=hb, sub_q=sub_q, nk=nk),
        grid=(n_batch, h // hb, nq, nk),
        in_specs=in_specs,
        out_specs=pl.BlockSpec((tq, hb * LANES), qmap),
        out_shape=jax.ShapeDtypeStruct((m, h * MLA_V), BF16),
        scratch_shapes=scratch,
        input_output_aliases=aliases,
        compiler_params=_params(("parallel", "parallel", "parallel", "arbitrary")),
    )(*args)


def _router_kernel(x_ref, g_ref, wr_ref, idx_ref, gate_ref):
    h = _rms(x_ref[...], g_ref[...])
    w = wr_ref[...]
    h_hi = h.astype(BF16)
    h_lo = (h - h_hi.astype(F32)).astype(BF16)
    w_hi = w.astype(BF16)
    w_lo = (w - w_hi.astype(F32)).astype(BF16)
    logits = (jnp.dot(h_hi, w_hi, preferred_element_type=F32)
              + (jnp.dot(h_lo, w_hi, preferred_element_type=F32) + jnp.dot(h_hi, w_lo, preferred_element_type=F32)))
    lane = lax.broadcasted_iota(jnp.int32, logits.shape, 1)
    logits = jnp.where(lane < N_EXPERTS, logits, -jnp.inf)
    m1 = jnp.max(logits, axis=-1, keepdims=True)
    i1 = jnp.min(jnp.where(logits == m1, lane, LANES), axis=-1, keepdims=True)
    rest = jnp.where(lane == i1, -jnp.inf, logits)
    m2 = jnp.max(rest, axis=-1, keepdims=True)
    i2 = jnp.min(jnp.where(rest == m2, lane, LANES), axis=-1, keepdims=True)
    e = jnp.exp(m2 - m1)
    g1 = 1.0 / (1.0 + e)
    g2 = e / (1.0 + e)
    idx_ref[...] = jnp.where(lane == 0, i1, jnp.where(lane == 1, i2, 0))
    gate_ref[...] = jnp.where(lane == 0, g1, jnp.where(lane == 1, g2, 0.0))


def _router(x, g, wr, *, tm):
    m, d = x.shape
    return pl.pallas_call(
        _router_kernel,
        grid=(m // tm,),
        in_specs=[
            pl.BlockSpec((tm, d), lambda i: (i, 0)),
            pl.BlockSpec((1, d), lambda i: (0, 0)),
            pl.BlockSpec((d, LANES), lambda i: (0, 0)),
        ],
        out_specs=[pl.BlockSpec((tm, LANES), lambda i: (i, 0)), pl.BlockSpec((tm, LANES), lambda i: (i, 0))],
        out_shape=[jax.ShapeDtypeStruct((m, LANES), jnp.int32), jax.ShapeDtypeStruct((m, LANES), F32)],
        compiler_params=_params(("parallel",)),
    )(x, g, wr)


DMA_UNROLL = 8


def _scatter_kernel(pe_ref, pos_ref, x_ref, o_hbm, zero_scr, sem, zsem, *, tm):
    r = x_ref.shape[0]

    def zero_copy(e):
        end = pe_ref[e]
        return pltpu.make_async_copy(zero_scr, o_hbm.at[pl.ds(pl.multiple_of(end - tm, tm), tm)], zsem)

    def nonempty(e):
        return pe_ref[e] > (pe_ref[e - 1] if e else 0)

    @pl.when(pl.program_id(0) == 0)
    def _():
        zero_scr[...] = jnp.zeros_like(zero_scr)
        for e in range(N_EXPERTS):
            @pl.when(nonempty(e))
            def _(e=e):
                zero_copy(e).start()
        for e in range(N_EXPERTS):
            @pl.when(nonempty(e))
            def _(e=e):
                zero_copy(e).wait()

    def row_copy(j, row):
        return pltpu.make_async_copy(x_ref.at[pl.ds(j, 1)], o_hbm.at[pl.ds(row, 1)], sem)

    def start(j, c):
        for k in range(TOP_K):
            row_copy(j, pos_ref[0, 0, TOP_K * j + k]).start()
        return c

    lax.fori_loop(0, r, start, 0, unroll=DMA_UNROLL)
    for k in range(TOP_K):
        pltpu.make_async_copy(x_ref, o_hbm.at[pl.ds(0, r)], sem).wait()


def _scatter_rows(pad_end, pos, x, p_rows, *, r, tm):
    m, d = x.shape
    grid_spec = pltpu.PrefetchScalarGridSpec(
        num_scalar_prefetch=1,
        grid=(m // r,),
        in_specs=[
            pl.BlockSpec((1, 1, TOP_K * r), lambda i, pe: (i, 0, 0), memory_space=pltpu.SMEM),
            pl.BlockSpec((r, d), lambda i, pe: (i, 0)),
        ],
        out_specs=pl.BlockSpec(memory_space=pl.ANY),
        scratch_shapes=[pltpu.VMEM((tm, d), F32), pltpu.SemaphoreType.DMA(()), pltpu.SemaphoreType.DMA(())],
    )
    return pl.pallas_call(
        functools.partial(_scatter_kernel, tm=tm),
        grid_spec=grid_spec,
        out_shape=jax.ShapeDtypeStruct((p_rows, d), F32),
        compiler_params=_params(("arbitrary",)),
    )(pad_end, pos.reshape(m // r, 1, TOP_K * r), x)


def _combine_kernel(pos_ref, pos_next_ref, x_ref, gate_ref, y_hbm, *rest, bounds):
    o_refs = rest[:len(bounds)]
    buf, sems = rest[len(bounds):]
    r = x_ref.shape[0]
    i = pl.program_id(0)
    slot = i % 2

    def issue(p_ref, s):
        def start(j, c):
            for k in range(TOP_K):
                pltpu.make_async_copy(y_hbm.at[pl.ds(p_ref[0, 0, TOP_K * j + k], 1)],
                                      buf.at[s, k, pl.ds(j, 1)], sems.at[s]).start()
            return c
        lax.fori_loop(0, r, start, 0, unroll=DMA_UNROLL)

    @pl.when(i == 0)
    def _():
        issue(pos_ref, 0)

    @pl.when(i + 1 < pl.num_programs(0))
    def _():
        issue(pos_next_ref, 1 - slot)

    for k in range(TOP_K):
        pltpu.make_async_copy(y_hbm.at[pl.ds(0, r)], buf.at[slot, k], sems.at[slot]).wait()
    gate = gate_ref[...]
    out = x_ref[...] + (gate[:, 0:1] * buf[slot, 0] + gate[:, 1:2] * buf[slot, 1])

    def write(o_ref):
        o_ref[...] = out
    _select_rows(i, o_refs, bounds, write)


def _combine(pos, x, gates, y, out_rows, *, r):
    m, d = x.shape
    out_shape = [jax.ShapeDtypeStruct((n, d), F32) for n in out_rows]
    o_specs, bounds, n_row_tiles = _split_specs(out_shape, r, d)
    assert n_row_tiles * r == m
    pos3 = pos.reshape(n_row_tiles, 1, TOP_K * r)
    return pl.pallas_call(
        functools.partial(_combine_kernel, bounds=bounds),
        grid=(n_row_tiles,),
        in_specs=[
            pl.BlockSpec((1, 1, TOP_K * r), lambda i: (i, 0, 0), memory_space=pltpu.SMEM),
            pl.BlockSpec((1, 1, TOP_K * r), lambda i: (jnp.minimum(i + 1, n_row_tiles - 1), 0, 0),
                         memory_space=pltpu.SMEM),
            pl.BlockSpec((r, d), lambda i: (i, 0)),
            pl.BlockSpec((r, LANES), lambda i: (i, 0)),
            pl.BlockSpec(memory_space=pl.ANY),
        ],
        out_specs=o_specs,
        out_shape=out_shape,
        scratch_shapes=[pltpu.VMEM((2, TOP_K, r, d), F32), pltpu.SemaphoreType.DMA((2,))],
        compiler_params=_params(("arbitrary",)),
    )(pos3, pos3, x, gates, y)


def _route_plan(idx, n_tokens, tm):
    n_pairs = n_tokens * TOP_K
    n_tiles = n_pairs // tm + N_EXPERTS
    e_flat = idx[:, :TOP_K].reshape(n_pairs)
    onehot = (e_flat[:, None] == jnp.arange(N_EXPERTS)[None, :]).astype(jnp.int32)
    csum = jnp.cumsum(onehot, axis=0)
    counts = csum[-1]
    padded = ((counts + tm - 1) // tm) * tm
    pad_end = jnp.cumsum(padded)
    pad_start = pad_end - padded
    pos = jnp.sum((csum - onehot + pad_start[None, :]) * onehot, axis=1)
    tile_start = jnp.arange(n_tiles) * tm
    tile_expert = jnp.minimum(jnp.sum(tile_start[:, None] >= pad_end[None, :], axis=1), N_EXPERTS - 1)
    n_used = pad_end[-1:] // tm
    return (pos.astype(jnp.int32), pad_end.astype(jnp.int32), tile_expert.astype(jnp.int32),
            n_used.astype(jnp.int32))


def _trunk(xs, seqs, mix_norm, ffn_norm, na_w_qkv, na_q_gain, na_k_gain, na_rpb, na_w_o,
           mla_w_dqkv, mla_q_lora_gain, mla_kv_lora_gain, mla_w_uq, mla_w_ukv,
           mla_qn_gain, mla_qr_gain, mla_kn_gain, mla_kr_gain, mla_w_o,
           ffn_w_gate, ffn_w_up, ffn_w_down, moe_w_router, moe_w_gate, moe_w_up, moe_w_down):
    m = sum(x.shape[0] for x in xs)
    row2 = lambda a: a.reshape(1, -1).astype(F32)
    na_dim = NA_HEADS * NA_HEAD_DIM

    head_gain = jnp.concatenate([
        jnp.tile(na_q_gain[0] * NA_HEAD_DIM ** -0.5, NA_HEADS),
        jnp.tile(na_k_gain[0], NA_HEADS),
        jnp.ones((na_dim,), F32)]).reshape(1, -1)
    qkv = _norm_matmul(xs, row2(mix_norm[0]), na_w_qkv[0].astype(BF16), head_gain, 2 * na_dim,
                       tm=min(1024, min(x.shape[0] for x in xs)), tn=1024)
    bias = _na_bias_table(na_rpb[0])
    amax = lambda a: jnp.max(jnp.abs(a.astype(F32)))
    na_bound = (NA_HEAD_DIM ** 0.5 * amax(na_q_gain[0]) * amax(na_k_gain[0]) + amax(na_rpb[0]))
    na_safe = _safe_flag(na_bound)
    attn = None
    off = 0
    for seq, nb in seqs:
        attn = _na_attention(na_safe, qkv, bias, attn, seq=seq, n_batch=nb, row_block_off=off // seq)
        off += seq * nb
    x = _matmul_res(attn, na_w_o[0].astype(BF16), xs, tm=min(512, m))

    x = _ffn(x, row2(ffn_norm[0]), ffn_w_gate[0].astype(BF16), ffn_w_up[0].astype(BF16),
             ffn_w_down[0].astype(BF16), tm=min(512, m), tf=512)

    scale = (MLA_NOPE + MLA_ROPE) ** -0.5
    lat_w = MLA_Q_LORA + MLA_KV_LORA
    wd = jnp.concatenate((mla_w_dqkv[0][:, :lat_w],) + _rope_pieces(mla_w_dqkv[0][:, lat_w:]), axis=1).astype(BF16)
    wuq = mla_w_uq[0].reshape(MLA_Q_LORA, MLA_HEADS, MLA_NOPE + MLA_ROPE)
    wuq = jnp.concatenate(
        [wuq[:, :, :MLA_NOPE].reshape(MLA_Q_LORA, -1)]
        + [p.reshape(MLA_Q_LORA, -1) for p in _rope_pieces(wuq[:, :, MLA_NOPE:])], axis=1).astype(BF16)
    wukv = mla_w_ukv[0].reshape(MLA_KV_LORA, MLA_HEADS, MLA_NOPE + MLA_V)
    wukv = jnp.concatenate([wukv[:, :, :MLA_NOPE].reshape(MLA_KV_LORA, -1),
                            wukv[:, :, MLA_NOPE:].reshape(MLA_KV_LORA, -1)], axis=1).astype(BF16)
    gain_pieces = lambda gg: [p.reshape(1, -1).astype(F32) for p in _rope_pieces(gg)]
    max_seq = max(s for s, _ in seqs)
    cos, sin = _rope_tables(max_seq)
    tm_p = min(512, m)
    bounds = []
    off = 0
    for seq, nb in seqs:
        bounds.append((off // tm_p, seq // tm_p))
        off += seq * nb

    def table_block(i):
        blk = i
        for start, per in bounds:
            blk = jnp.where(i >= start, (i - start) % per, blk)
        return blk

    qn, qr, kn, kr, v = _mla_proj(
        x, row2(mix_norm[1]), wd, row2(mla_q_lora_gain[0]), row2(mla_kv_lora_gain[0]),
        *gain_pieces(mla_kr_gain[0]), wuq, row2(mla_qn_gain[0] * scale), *gain_pieces(mla_qr_gain[0] * scale),
        wukv, row2(mla_kn_gain[0]), cos, sin, table_block, tm=tm_p)
    mla_bound = scale * (MLA_NOPE * amax(mla_qn_gain[0]) * amax(mla_kn_gain[0])
                         + MLA_ROPE * amax(mla_qr_gain[0]) * amax(mla_kr_gain[0]))
    mla_safe = _safe_flag(mla_bound)
    attn = None
    off = 0
    for seq, nb in seqs:
        tq = min(FLASH_TQ, seq)
        attn = _flash(mla_safe, qn, qr, kn, kr, v, attn, seq=seq, n_batch=nb, row_off=off,
                      tq=tq, tk=min(FLASH_TK, seq), hb=FLASH_HEADS, sub_q=min(FLASH_SUB_Q, tq))
        off += seq * nb
    x = _matmul_res(attn, mla_w_o[0].astype(BF16), [x], tm=min(512, m))

    wr = jnp.pad(moe_w_router[0], ((0, 0), (0, LANES - N_EXPERTS))).astype(F32)
    idx, gates = _router(x, row2(ffn_norm[1]), wr, tm=min(512, m))
    tm_e = min(512, m)
    pos, pad_end, tile_expert, n_used = _route_plan(idx, m, tm_e)
    r = min(512, min(xx.shape[0] for xx in xs))
    x_sorted = _scatter_rows(pad_end, pos, x, tile_expert.shape[0] * tm_e, r=r, tm=tm_e)
    y_sorted = _expert_ffn(tile_expert, n_used, x_sorted, row2(ffn_norm[1]), moe_w_gate[0].astype(BF16),
                           moe_w_up[0].astype(BF16), moe_w_down[0].astype(BF16), tm=tm_e, tf=512)
    return _combine(pos, x, gates, y_sorted, [xx.shape[0] for xx in xs], r=r)


def kernel(x_prompt, x_sample, mix_norm, ffn_norm, na_w_qkv, na_q_gain, na_k_gain, na_rpb, na_w_o, mla_w_dqkv, mla_q_lora_gain, mla_kv_lora_gain, mla_w_uq, mla_w_ukv, mla_qn_gain, mla_qr_gain, mla_kn_gain, mla_kr_gain, mla_w_o, ffn_w_gate, ffn_w_up, ffn_w_down, moe_w_router, moe_w_gate, moe_w_up, moe_w_down):
    bp, sp, d = x_prompt.shape
    bs, ss, _ = x_sample.shape
    yp, ys = _trunk([x_prompt.reshape(bp * sp, d), x_sample.reshape(bs * ss, d)], [(sp, bp), (ss, bs)],
                    mix_norm, ffn_norm, na_w_qkv, na_q_gain, na_k_gain, na_rpb, na_w_o,
                    mla_w_dqkv, mla_q_lora_gain, mla_kv_lora_gain, mla_w_uq, mla_w_ukv,
                    mla_qn_gain, mla_qr_gain, mla_kn_gain, mla_kr_gain, mla_w_o,
                    ffn_w_gate, ffn_w_up, ffn_w_down, moe_w_router, moe_w_gate, moe_w_up, moe_w_down)
    return (yp.reshape(bp, sp, d), ys.reshape(bs, ss, d))
```

```python
import functools

import jax
import jax.numpy as jnp
from jax import lax
from jax.experimental import pallas as pl
from jax.experimental.pallas import tpu as pltpu

F32 = jnp.float32
BF16 = jnp.bfloat16

D_MODEL = 2048
GRID_W = 64
NA_HEADS = 16
NA_HEAD_DIM = 128
NA_WIN_H = 8
NA_WIN_W = 16
MLA_HEADS = 16
MLA_Q_LORA = 512
MLA_KV_LORA = 512
MLA_NOPE = 128
MLA_ROPE = 64
MLA_V = 128
ROPE_THETA = 10000.0
D_FF = 5632
N_EXPERTS = 8
TOP_K = 2
EPS = 1e-6
NEG_INF = -1e30

LANES = 128
VMEM_LIMIT = 56 * 1024 * 1024


def _params(sem):
    return pltpu.CompilerParams(dimension_semantics=sem, vmem_limit_bytes=VMEM_LIMIT)


def _rms(x, g):
    return x * lax.rsqrt(jnp.mean(x * x, axis=-1, keepdims=True) + EPS) * g


def _select_rows(i, refs, bounds, fn):
    for ref, (lo, hi) in zip(refs, bounds):
        @pl.when((i >= lo) & (i < hi))
        def _(ref=ref):
            fn(ref)


def _split_specs(xs, tm, ncols):
    specs, bounds, lo = [], [], 0
    for x in xs:
        n = x.shape[0] // tm
        assert n * tm == x.shape[0]
        specs.append(pl.BlockSpec(
            (tm, ncols), functools.partial(lambda i, *_, lo, n: (jnp.clip(i - lo, 0, n - 1), 0), lo=lo, n=n)))
        bounds.append((lo, lo + n))
        lo += n
    return specs, tuple(bounds), lo


def _norm_matmul_kernel(*refs, n_norm_tiles, bounds):
    nx = len(bounds)
    x_refs = refs[:nx]
    g_ref, w_ref, hg_ref, o_ref, h_scr = refs[nx:]
    j = pl.program_id(1)

    @pl.when(j == 0)
    def _():
        def fill(x_ref):
            h_scr[...] = _rms(x_ref[...], g_ref[...]).astype(BF16)
        _select_rows(pl.program_id(0), x_refs, bounds, fill)

    tn = o_ref.shape[1]
    chunk = min(tn, 2 * LANES)

    def emit(normed):
        for c in range(tn // chunk):
            acc = jnp.dot(h_scr[...], w_ref[:, c * chunk:(c + 1) * chunk], preferred_element_type=F32)
            for g in range(chunk // LANES):
                sl = slice(c * chunk + g * LANES, c * chunk + (g + 1) * LANES)
                a = acc[:, g * LANES:(g + 1) * LANES]
                o_ref[:, sl] = (_rms(a, hg_ref[:, sl]) if normed else a).astype(o_ref.dtype)

    @pl.when(j < n_norm_tiles)
    def _():
        emit(True)

    @pl.when(j >= n_norm_tiles)
    def _():
        emit(False)


def _norm_matmul(xs, g, w, head_gain, n_norm_cols, *, tm, tn):
    d = xs[0].shape[1]
    n = w.shape[1]
    x_specs, bounds, n_row_tiles = _split_specs(xs, tm, d)
    return pl.pallas_call(
        functools.partial(_norm_matmul_kernel, n_norm_tiles=n_norm_cols // tn, bounds=bounds),
        grid=(n_row_tiles, n // tn),
        in_specs=x_specs + [
            pl.BlockSpec((1, d), lambda i, j: (0, 0)),
            pl.BlockSpec((d, tn), lambda i, j: (0, j)),
            pl.BlockSpec((1, tn), lambda i, j: (0, j)),
        ],
        out_specs=pl.BlockSpec((tm, tn), lambda i, j: (i, j)),
        out_shape=jax.ShapeDtypeStruct((n_row_tiles * tm, n), BF16),
        scratch_shapes=[pltpu.VMEM((tm, d), BF16)],
        compiler_params=_params(("parallel", "arbitrary")),
    )(*xs, g, w, head_gain)


NA_QROWS = 4
NA_SPAN = NA_WIN_H + NA_QROWS


UNSHIFTED_SOFTMAX_BOUND = 60.0


def _safe_flag(bound):
    return (bound * 1.02 <= UNSHIFTED_SOFTMAX_BOUND).astype(jnp.int32).reshape(1)


def _na_kernel(safe_ref, q_ref, k_ref, v_ref, b_ref, *rest, rows, cast_ranges):
    nc = len(cast_ranges)
    cast_src, o_ref, cast_dst = rest[:nc], rest[-1 - nc], rest[len(rest) - nc:]
    _side_cast(pl.program_id(0) * pl.num_programs(1) + pl.program_id(1), cast_src, cast_dst, cast_ranges)
    nq = NA_QROWS * GRID_W
    nk = NA_SPAN * GRID_W
    n_blk = rows // NA_QROWS

    def run(shift):
        def body(blk, carry):
            r0 = blk * NA_QROWS
            span0 = jnp.clip(r0 - NA_WIN_H // 2, 0, rows - NA_SPAN)
            case = jnp.where(blk == 0, 0, jnp.where(blk == n_blk - 1, 2, 1))
            qoff = pl.multiple_of(r0 * GRID_W, nq)
            koff = pl.multiple_of(span0 * GRID_W, GRID_W)
            q = q_ref[pl.ds(qoff, nq), :]
            k = k_ref[pl.ds(koff, nk), :]
            v = v_ref[pl.ds(koff, nk), :]
            s = lax.dot_general(q, k, (((1,), (1,)), ((), ())), preferred_element_type=F32)
            s = s + b_ref[0, case]
            if shift:
                s = s - jnp.max(s, axis=-1, keepdims=True)
            p = jnp.exp(s)
            l = jnp.sum(p, axis=-1, keepdims=True)
            o = jnp.dot(p.astype(BF16), v, preferred_element_type=F32) / l
            o_ref[pl.ds(qoff, nq), :] = o.astype(o_ref.dtype)
            return carry

        lax.fori_loop(0, n_blk, body, 0, unroll=8)

    @pl.when(safe_ref[0] == 1)
    def _():
        run(False)

    @pl.when(safe_ref[0] != 1)
    def _():
        run(True)


def _na_bias_table(rpb):
    qc = jnp.arange(GRID_W)[:, None]
    kc = jnp.arange(GRID_W)[None, :]
    ws = jnp.clip(qc - NA_WIN_W // 2, 0, GRID_W - NA_WIN_W)
    col_ok = (kc >= ws) & (kc < ws + NA_WIN_W)
    dc = jnp.clip(kc - qc + NA_WIN_W - 1, 0, 2 * NA_WIN_W - 2)
    i = jnp.arange(NA_QROWS)
    half = NA_WIN_H // 2
    w_off = jnp.stack([jnp.zeros_like(i), i, jnp.full_like(i, NA_SPAN - NA_WIN_H)])
    c_off = jnp.stack([i, jnp.full_like(i, half), half + i])
    j = jnp.arange(NA_SPAN)[None, None, :] - w_off[:, :, None]
    row_ok = (j >= 0) & (j < NA_WIN_H)
    dr = jnp.clip(j - c_off[:, :, None] + NA_WIN_H - 1, 0, 2 * NA_WIN_H - 2)
    oh_r = jax.nn.one_hot(dr, 2 * NA_WIN_H - 1, dtype=F32)
    oh_c = jax.nn.one_hot(dc, 2 * NA_WIN_W - 1, dtype=F32)
    t = jnp.einsum('hab,xija,qkb->hxiqjk', rpb.astype(F32), oh_r, oh_c, precision=lax.Precision.HIGHEST)
    ok = row_ok[None, :, :, None, :, None] & col_ok[None, None, None, :, None, :]
    t = jnp.where(ok, t, NEG_INF)
    return t.reshape(rpb.shape[0], 3, NA_QROWS * GRID_W, NA_SPAN * GRID_W)


def _na_attention(safe, qkv, bias, out_prev, cast_arrays, *, seq, n_batch, row_block_off):
    m = qkv.shape[0]
    rows = seq // GRID_W
    assert rows >= NA_SPAN + NA_QROWS and rows % NA_QROWS == 0
    h = NA_HEADS
    plan = _side_cast_plan(cast_arrays, h * n_batch, lambda hh, b: hh * n_batch + b) if cast_arrays else None
    c_in, c_out, c_shapes, ranges = plan if plan is not None else ([], [], [], ())

    def spec(sec):
        return pl.BlockSpec((seq, NA_HEAD_DIM), lambda hh, b: (row_block_off + b, sec * h + hh))

    in_specs = [pl.BlockSpec(memory_space=pltpu.SMEM), spec(0), spec(1), spec(2),
                pl.BlockSpec((1,) + bias.shape[1:], lambda hh, b: (hh, 0, 0, 0))] + c_in
    args = [safe, qkv, qkv, qkv, bias] + (list(cast_arrays) if plan is not None else [])
    aliases = {}
    if out_prev is not None:
        in_specs.append(pl.BlockSpec(memory_space=pl.ANY))
        args.append(out_prev)
        aliases = {len(args) - 1: 0}
    outs = pl.pallas_call(
        functools.partial(_na_kernel, rows=rows, cast_ranges=ranges),
        grid=(h, n_batch),
        in_specs=in_specs,
        out_specs=[pl.BlockSpec((seq, NA_HEAD_DIM), lambda hh, b: (row_block_off + b, hh))] + c_out,
        out_shape=[jax.ShapeDtypeStruct((m, h * NA_HEAD_DIM), BF16)] + c_shapes,
        input_output_aliases=aliases,
        compiler_params=_params(("arbitrary", "arbitrary")),
    )(*args)
    return outs[0], (list(outs[1:]) if plan is not None else None)


def _matmul_res_kernel(a_ref, w_ref, *refs, bounds):
    r_refs, o_ref = refs[:-1], refs[-1]
    y = jnp.dot(a_ref[...], w_ref[...], preferred_element_type=F32)

    def add(r_ref):
        o_ref[...] = r_ref[...] + y
    _select_rows(pl.program_id(0), r_refs, bounds, add)


def _matmul_res(a, w, res_list, *, tm):
    m, k = a.shape
    n = w.shape[1]
    r_specs, bounds, n_row_tiles = _split_specs(res_list, tm, n)
    assert n_row_tiles * tm == m
    return pl.pallas_call(
        functools.partial(_matmul_res_kernel, bounds=bounds),
        grid=(n_row_tiles,),
        in_specs=[
            pl.BlockSpec((tm, k), lambda i: (i, 0)),
            pl.BlockSpec((k, n), lambda i: (0, 0)),
        ] + r_specs,
        out_specs=pl.BlockSpec((tm, n), lambda i: (i, 0)),
        out_shape=jax.ShapeDtypeStruct((m, n), F32),
        compiler_params=_params(("parallel",)),
    )(a, w, *res_list)


def _swiglu_step(h, wg_ref, wu_ref, wd_ref, lead=()):
    tf = wd_ref.shape[-2]
    chunk = min(tf, 2 * LANES)
    out = None
    for c in range(tf // chunk):
        cs = slice(c * chunk, (c + 1) * chunk)
        a = jnp.dot(h, wg_ref[lead + (slice(None), cs)], preferred_element_type=F32)
        b = jnp.dot(h, wu_ref[lead + (slice(None), cs)], preferred_element_type=F32)
        t = (a / (1.0 + jnp.exp(-a)) * b).astype(BF16)
        y = jnp.dot(t, wd_ref[lead + (cs, slice(None))], preferred_element_type=F32)
        out = y if out is None else out + y
    return out


CAST_BLOCK_BYTES = 3 * 512 * 1024


def _side_cast_plan(arrays, n_steps, step_fn):
    budget = n_steps // max(len(arrays), 1)
    in_specs, out_specs, out_shapes, ranges, off = [], [], [], [], 0
    for a in arrays:
        rows, cols = a.shape
        nb = next((n for n in range(budget, 0, -1)
                   if rows % n == 0 and (rows // n) % 16 == 0), None)
        if nb is None or (rows // nb) * cols * 4 > CAST_BLOCK_BYTES:
            return None
        imap = functools.partial(lambda *ids, off, nb: (jnp.clip(step_fn(*ids) - off, 0, nb - 1), 0),
                                 off=off, nb=nb)
        in_specs.append(pl.BlockSpec((rows // nb, cols), imap))
        out_specs.append(pl.BlockSpec((rows // nb, cols), imap))
        out_shapes.append(jax.ShapeDtypeStruct((rows, cols), BF16))
        ranges.append((off, off + nb))
        off += nb
    return in_specs, out_specs, out_shapes, tuple(ranges)


def _side_cast(step, src_refs, dst_refs, ranges):
    for src, dst, (lo, hi) in zip(src_refs, dst_refs, ranges):
        @pl.when((step >= lo) & (step < hi))
        def _(src=src, dst=dst):
            dst[...] = src[...].astype(BF16)


def _ffn_kernel(x_ref, g_ref, wg_ref, wu_ref, wd_ref, *rest, cast_ranges):
    nc = len(cast_ranges)
    cast_src, o_ref, cast_dst, h_scr = rest[:nc], rest[nc], rest[nc + 1:2 * nc + 1], rest[-1]
    f = pl.program_id(1)

    @pl.when(f == 0)
    def _():
        x = x_ref[...]
        h_scr[...] = _rms(x, g_ref[...]).astype(BF16)
        o_ref[...] = x

    o_ref[...] += _swiglu_step(h_scr[...], wg_ref, wu_ref, wd_ref)
    _side_cast(pl.program_id(0) * pl.num_programs(1) + f, cast_src, cast_dst, cast_ranges)


def _ffn(x, g, wg, wu, wd, cast_arrays, *, tm, tf):
    m, d = x.shape
    nf = wg.shape[1] // tf
    plan = _side_cast_plan(cast_arrays, (m // tm) * nf, lambda i, j: i * nf + j)
    if plan is None:
        c_in, c_out, c_shapes, ranges, cast_arrays, fallback = [], [], [], (), [], cast_arrays
    else:
        (c_in, c_out, c_shapes, ranges), fallback = plan, []
    outs = pl.pallas_call(
        functools.partial(_ffn_kernel, cast_ranges=ranges),
        grid=(m // tm, nf),
        in_specs=[
            pl.BlockSpec((tm, d), lambda i, j: (i, 0)),
            pl.BlockSpec((1, d), lambda i, j: (0, 0)),
            pl.BlockSpec((d, tf), lambda i, j: (0, j)),
            pl.BlockSpec((d, tf), lambda i, j: (0, j)),
            pl.BlockSpec((tf, d), lambda i, j: (j, 0)),
        ] + c_in,
        out_specs=[pl.BlockSpec((tm, d), lambda i, j: (i, 0))] + c_out,
        out_shape=[jax.ShapeDtypeStruct((m, d), F32)] + c_shapes,
        scratch_shapes=[pltpu.VMEM((tm, d), BF16)],
        compiler_params=_params(("arbitrary", "arbitrary")),
    )(x, g, wg, wu, wd, *cast_arrays)
    return outs[0], list(outs[1:]) + [a.astype(BF16) for a in fallback]


def _expert_ffn_kernel(te_ref, nu_ref, x_ref, g_ref, wg_ref, wu_ref, wd_ref, o_ref, h_scr):
    i = pl.program_id(0)
    f = pl.program_id(1)

    @pl.when(i < nu_ref[0])
    def _():
        @pl.when(f == 0)
        def _():
            h_scr[...] = _rms(x_ref[...], g_ref[...]).astype(BF16)
            o_ref[...] = jnp.zeros_like(o_ref)

        o_ref[...] += _swiglu_step(h_scr[...], wg_ref, wu_ref, wd_ref, lead=(0,))


def _expert_ffn(tile_expert, n_used, xs, g, wg, wu, wd, *, tm, tf):
    p, d = xs.shape
    nf = wg.shape[2] // tf

    def row_map(i, j, te, nu):
        return (jnp.minimum(i, nu[0] - 1), 0)

    def wcol_map(i, j, te, nu):
        used = i < nu[0]
        return (te[jnp.minimum(i, nu[0] - 1)], 0, jnp.where(used, j, nf - 1))

    def wrow_map(i, j, te, nu):
        used = i < nu[0]
        return (te[jnp.minimum(i, nu[0] - 1)], jnp.where(used, j, nf - 1), 0)

    grid_spec = pltpu.PrefetchScalarGridSpec(
        num_scalar_prefetch=2,
        grid=(p // tm, nf),
        in_specs=[
            pl.BlockSpec((tm, d), row_map),
            pl.BlockSpec((1, d), lambda i, j, te, nu: (0, 0)),
            pl.BlockSpec((1, d, tf), wcol_map),
            pl.BlockSpec((1, d, tf), wcol_map),
            pl.BlockSpec((1, tf, d), wrow_map),
        ],
        out_specs=pl.BlockSpec((tm, d), row_map),
        scratch_shapes=[pltpu.VMEM((tm, d), BF16)],
    )
    return pl.pallas_call(
        _expert_ffn_kernel,
        grid_spec=grid_spec,
        out_shape=jax.ShapeDtypeStruct((p, d), F32),
        compiler_params=_params(("arbitrary", "arbitrary")),
    )(tile_expert, n_used, xs, g, wg, wu, wd)


ROPE_PAD = LANES


def _rope_pieces(w):
    half = MLA_ROPE // 2
    x1, x2 = w[..., :half], w[..., half:]
    z = jnp.zeros(w.shape[:-1] + (ROPE_PAD - MLA_ROPE,), w.dtype)
    return jnp.concatenate([x1, x2, z], axis=-1), jnp.concatenate([x2, x1, z], axis=-1)


def _mla_proj_kernel(x_ref, g_ref, wd_ref, gq_ref, gkv_ref, gkr_ref, gkrs_ref, wuq_ref, gqn_ref, gqr_ref,
                     gqrs_ref, wukv_ref, gkn_ref, cos_ref, sin_ref,
                     qn_ref, qr_ref, kn_ref, kr_ref, v_ref):
    cos = cos_ref[...]
    sin = sin_ref[...]

    def rope(piece, piece_sw, gain, gain_sw):
        ms = jnp.sum(piece * piece, axis=-1, keepdims=True) * (1.0 / MLA_ROPE)
        return lax.rsqrt(ms + EPS) * (piece * (gain * cos) + piece_sw * (gain_sw * sin))

    h = _rms(x_ref[...], g_ref[...]).astype(BF16)
    lat = jnp.dot(h, wd_ref[...], preferred_element_type=F32)
    cq = _rms(lat[:, :MLA_Q_LORA], gq_ref[...]).astype(BF16)
    ckv = _rms(lat[:, MLA_Q_LORA:MLA_Q_LORA + MLA_KV_LORA], gkv_ref[...]).astype(BF16)
    kr0 = MLA_Q_LORA + MLA_KV_LORA
    kr_ref[...] = rope(lat[:, kr0:kr0 + ROPE_PAD], lat[:, kr0 + ROPE_PAD:],
                       gkr_ref[...], gkrs_ref[...]).astype(BF16)

    hd = MLA_HEADS * LANES
    chunk = 4 * LANES
    for c in range(hd // chunk):
        base = c * chunk
        qn = jnp.dot(cq, wuq_ref[:, base:base + chunk], preferred_element_type=F32)
        qr = jnp.dot(cq, wuq_ref[:, hd + base:hd + base + chunk], preferred_element_type=F32)
        qs = jnp.dot(cq, wuq_ref[:, 2 * hd + base:2 * hd + base + chunk], preferred_element_type=F32)
        kn = jnp.dot(ckv, wukv_ref[:, base:base + chunk], preferred_element_type=F32)
        v = jnp.dot(ckv, wukv_ref[:, hd + base:hd + base + chunk], preferred_element_type=F32)
        v_ref[:, base:base + chunk] = v.astype(BF16)
        for hh in range(chunk // LANES):
            sl = slice(hh * LANES, (hh + 1) * LANES)
            osl = slice(base + hh * LANES, base + (hh + 1) * LANES)
            qn_ref[:, osl] = _rms(qn[:, sl], gqn_ref[...]).astype(BF16)
            kn_ref[:, osl] = _rms(kn[:, sl], gkn_ref[...]).astype(BF16)
            qr_ref[:, osl] = rope(qr[:, sl], qs[:, sl], gqr_ref[...], gqrs_ref[...]).astype(BF16)


def _mla_proj(x, g, wd, gq, gkv, gkr, gkrs, wuq, gqn, gqr, gqrs, wukv, gkn, cos, sin, table_block, *, tm):
    m, d = x.shape
    hd = MLA_HEADS * LANES
    const = lambda i: (0, 0)
    row = lambda i: (i, 0)
    tab = lambda i: (table_block(i), 0)
    return pl.pallas_call(
        _mla_proj_kernel,
        grid=(m // tm,),
        in_specs=[
            pl.BlockSpec((tm, d), row),
            pl.BlockSpec((1, d), const),
            pl.BlockSpec(wd.shape, const, pipeline_mode=pl.Buffered(1)),
            pl.BlockSpec((1, MLA_Q_LORA), const),
            pl.BlockSpec((1, MLA_KV_LORA), const),
            pl.BlockSpec((1, LANES), const),
            pl.BlockSpec((1, LANES), const),
            pl.BlockSpec(wuq.shape, const, pipeline_mode=pl.Buffered(1)),
            pl.BlockSpec((1, LANES), const),
            pl.BlockSpec((1, LANES), const),
            pl.BlockSpec((1, LANES), const),
            pl.BlockSpec(wukv.shape, const, pipeline_mode=pl.Buffered(1)),
            pl.BlockSpec((1, LANES), const),
            pl.BlockSpec((tm, LANES), tab),
            pl.BlockSpec((tm, LANES), tab),
        ],
        out_specs=[
            pl.BlockSpec((tm, hd), row),
            pl.BlockSpec((tm, hd), row),
            pl.BlockSpec((tm, hd), row),
            pl.BlockSpec((tm, LANES), row),
            pl.BlockSpec((tm, hd), row),
        ],
        out_shape=[
            jax.ShapeDtypeStruct((m, hd), BF16),
            jax.ShapeDtypeStruct((m, hd), BF16),
            jax.ShapeDtypeStruct((m, hd), BF16),
            jax.ShapeDtypeStruct((m, LANES), BF16),
            jax.ShapeDtypeStruct((m, hd), BF16),
        ],
        compiler_params=_params(("parallel",)),
    )(x, g, wd, gq, gkv, gkr, gkrs, wuq, gqn, gqr, gqrs, wukv, gkn, cos, sin)


def _rope_tables(seq):
    t = jnp.arange(seq)
    row = (t // GRID_W).astype(F32)
    col = (t % GRID_W).astype(F32)
    n_pairs = MLA_ROPE // 4
    inv = ROPE_THETA ** (-jnp.arange(n_pairs, dtype=F32) / n_pairs)
    ang = jnp.concatenate([row[:, None] * inv, col[:, None] * inv], axis=-1)
    c, s = jnp.cos(ang), jnp.sin(ang)
    z = jnp.zeros_like(c)
    return jnp.concatenate([c, c, z, z], axis=-1), jnp.concatenate([-s, s, z, z], axis=-1)


FLASH_TQ = 1024
FLASH_TK = 2048
FLASH_HEADS = 2
FLASH_SUB_Q = 256


def _flash_kernel(safe_ref, qn_ref, qr_ref, kn_ref, kr_ref, v_ref, *rest, hb, sub_q, nk):
    if nk == 1:
        o_ref = rest[-1]
    else:
        o_ref, m_scr, l_scr, acc_scr = rest[-4:]
        kk = pl.program_id(3)

        @pl.when(kk == 0)
        def _():
            m_scr[...] = jnp.full_like(m_scr, -jnp.inf)
            l_scr[...] = jnp.zeros_like(l_scr)
            acc_scr[...] = jnp.zeros_like(acc_scr)

    tq = qn_ref.shape[0]

    def run(shift):
        kr = kr_ref[...]
        for h in range(hb):
            hs = slice(h * LANES, (h + 1) * LANES)
            k = jnp.concatenate([kn_ref[:, hs], kr], axis=-1)
            v = v_ref[:, hs]
            for qi in range(tq // sub_q):
                rs = slice(qi * sub_q, (qi + 1) * sub_q)
                q = jnp.concatenate([qn_ref[rs, hs], qr_ref[rs, hs]], axis=-1)
                s = lax.dot_general(q, k, (((1,), (1,)), ((), ())), preferred_element_type=F32)
                if not shift:
                    p = jnp.exp(s)
                    l = jnp.sum(p, axis=-1, keepdims=True)
                    pv = jnp.dot(p.astype(BF16), v, preferred_element_type=F32)
                    if nk == 1:
                        o_ref[rs, hs] = (pv / l).astype(o_ref.dtype)
                    else:
                        l_scr[h, rs] += l
                        acc_scr[rs, hs] += pv
                elif nk == 1:
                    p = jnp.exp(s - jnp.max(s, axis=-1, keepdims=True))
                    l = jnp.sum(p, axis=-1, keepdims=True)
                    o = jnp.dot(p.astype(BF16), v, preferred_element_type=F32) / l
                    o_ref[rs, hs] = o.astype(o_ref.dtype)
                else:
                    m_prev = m_scr[h, rs]
                    m_new = jnp.maximum(m_prev, jnp.max(s, axis=-1, keepdims=True))
                    alpha = jnp.exp(m_prev - m_new)
                    p = jnp.exp(s - m_new)
                    l_scr[h, rs] = alpha * l_scr[h, rs] + jnp.sum(p, axis=-1, keepdims=True)
                    acc_scr[rs, hs] = (alpha * acc_scr[rs, hs]
                                       + jnp.dot(p.astype(BF16), v, preferred_element_type=F32))
                    m_scr[h, rs] = m_new

    @pl.when(safe_ref[0] == 1)
    def _():
        run(False)

    @pl.when(safe_ref[0] != 1)
    def _():
        run(True)

    if nk > 1:
        @pl.when(kk == nk - 1)
        def _():
            for h in range(hb):
                hs = slice(h * LANES, (h + 1) * LANES)
                o_ref[:, hs] = (acc_scr[:, hs] / l_scr[h]).astype(o_ref.dtype)


def _flash(safe, qn, qr, kn, kr, v, out_prev, *, seq, n_batch, row_off, tq, tk, hb, sub_q):
    m = qn.shape[0]
    h = MLA_HEADS
    nq, nk = seq // tq, seq // tk
    qoff, koff = row_off // tq, row_off // tk
    qmap = lambda b, hh, i, j: (qoff + b * nq + i, hh)
    kmap = lambda b, hh, i, j: (koff + b * nk + j, hh)
    in_specs = [
        pl.BlockSpec(memory_space=pltpu.SMEM),
        pl.BlockSpec((tq, hb * LANES), qmap),
        pl.BlockSpec((tq, hb * LANES), qmap),
        pl.BlockSpec((tk, hb * LANES), kmap),
        pl.BlockSpec((tk, LANES), lambda b, hh, i, j: (koff + b * nk + j, 0)),
        pl.BlockSpec((tk, hb * LANES), kmap),
    ]
    args = [safe, qn, qr, kn, kr, v]
    aliases = {}
    if out_prev is not None:
        in_specs.append(pl.BlockSpec(memory_space=pl.ANY))
        args.append(out_prev)
        aliases = {6: 0}
    scratch = []
    if nk > 1:
        scratch = [pltpu.VMEM((hb, tq, 1), F32), pltpu.VMEM((hb, tq, 1), F32), pltpu.VMEM((tq, hb * MLA_V), F32)]
    return pl.pallas_call(
        functools.partial(_flash_kernel, hb=hb, sub_q=sub_q, nk=nk),
        grid=(n_batch, h // hb, nq, nk),
        in_specs=in_specs,
        out_specs=pl.BlockSpec((tq, hb * LANES), qmap),
        out_shape=jax.ShapeDtypeStruct((m, h * MLA_V), BF16),
        scratch_shapes=scratch,
        input_output_aliases=aliases,
        compiler_params=_params(("parallel", "parallel", "parallel", "arbitrary")),
    )(*args)


def _router_kernel(x_ref, g_ref, wr_ref, idx_ref, gate_ref):
    h = _rms(x_ref[...], g_ref[...])
    w = wr_ref[...]
    h_hi = h.astype(BF16)
    h_lo = (h - h_hi.astype(F32)).astype(BF16)
    w_hi = w.astype(BF16)
    w_lo = (w - w_hi.astype(F32)).astype(BF16)
    logits = (jnp.dot(h_hi, w_hi, preferred_element_type=F32)
              + (jnp.dot(h_lo, w_hi, preferred_element_type=F32) + jnp.dot(h_hi, w_lo, preferred_element_type=F32)))
    lane = lax.broadcasted_iota(jnp.int32, logits.shape, 1)
    logits = jnp.where(lane < N_EXPERTS, logits, -jnp.inf)
    m1 = jnp.max(logits, axis=-1, keepdims=True)
    i1 = jnp.min(jnp.where(logits == m1, lane, LANES), axis=-1, keepdims=True)
    rest = jnp.where(lane == i1, -jnp.inf, logits)
    m2 = jnp.max(rest, axis=-1, keepdims=True)
    i2 = jnp.min(jnp.where(rest == m2, lane, LANES), axis=-1, keepdims=True)
    e = jnp.exp(m2 - m1)
    g1 = 1.0 / (1.0 + e)
    g2 = e / (1.0 + e)
    idx_ref[...] = jnp.where(lane == 0, i1, jnp.where(lane == 1, i2, 0))
    gate_ref[...] = jnp.where(lane == 0, g1, jnp.where(lane == 1, g2, 0.0))


def _router(x, g, wr, *, tm):
    m, d = x.shape
    return pl.pallas_call(
        _router_kernel,
        grid=(m // tm,),
        in_specs=[
            pl.BlockSpec((tm, d), lambda i: (i, 0)),
            pl.BlockSpec((1, d), lambda i: (0, 0)),
            pl.BlockSpec((d, LANES), lambda i: (0, 0)),
        ],
        out_specs=[pl.BlockSpec((tm, LANES), lambda i: (i, 0)), pl.BlockSpec((tm, LANES), lambda i: (i, 0))],
        out_shape=[jax.ShapeDtypeStruct((m, LANES), jnp.int32), jax.ShapeDtypeStruct((m, LANES), F32)],
        compiler_params=_params(("parallel",)),
    )(x, g, wr)


DMA_UNROLL = 8


def _scatter_kernel(pe_ref, pos_ref, x_ref, o_hbm, zero_scr, sem, zsem, *, tm):
    r = x_ref.shape[0]

    def zero_copy(e):
        end = pe_ref[e]
        return pltpu.make_async_copy(zero_scr, o_hbm.at[pl.ds(pl.multiple_of(end - tm, tm), tm)], zsem)

    def nonempty(e):
        return pe_ref[e] > (pe_ref[e - 1] if e else 0)

    @pl.when(pl.program_id(0) == 0)
    def _():
        zero_scr[...] = jnp.zeros_like(zero_scr)
        for e in range(N_EXPERTS):
            @pl.when(nonempty(e))
            def _(e=e):
                zero_copy(e).start()
        for e in range(N_EXPERTS):
            @pl.when(nonempty(e))
            def _(e=e):
                zero_copy(e).wait()

    def row_copy(j, row):
        return pltpu.make_async_copy(x_ref.at[pl.ds(j, 1)], o_hbm.at[pl.ds(row, 1)], sem)

    def start(j, c):
        for k in range(TOP_K):
            row_copy(j, pos_ref[0, 0, TOP_K * j + k]).start()
        return c

    lax.fori_loop(0, r, start, 0, unroll=DMA_UNROLL)
    for k in range(TOP_K):
        pltpu.make_async_copy(x_ref, o_hbm.at[pl.ds(0, r)], sem).wait()


def _scatter_rows(pad_end, pos, x, p_rows, *, r, tm):
    m, d = x.shape
    grid_spec = pltpu.PrefetchScalarGridSpec(
        num_scalar_prefetch=1,
        grid=(m // r,),
        in_specs=[
            pl.BlockSpec((1, 1, TOP_K * r), lambda i, pe: (i, 0, 0), memory_space=pltpu.SMEM),
            pl.BlockSpec((r, d), lambda i, pe: (i, 0)),
        ],
        out_specs=pl.BlockSpec(memory_space=pl.ANY),
        scratch_shapes=[pltpu.VMEM((tm, d), F32), pltpu.SemaphoreType.DMA(()), pltpu.SemaphoreType.DMA(())],
    )
    return pl.pallas_call(
        functools.partial(_scatter_kernel, tm=tm),
        grid_spec=grid_spec,
        out_shape=jax.ShapeDtypeStruct((p_rows, d), F32),
        compiler_params=_params(("arbitrary",)),
    )(pad_end, pos.reshape(m // r, 1, TOP_K * r), x)


def _combine_kernel(pos_ref, pos_next_ref, x_ref, gate_ref, y_hbm, *rest, bounds):
    o_refs = rest[:len(bounds)]
    buf, sems = rest[len(bounds):]
    r = x_ref.shape[0]
    i = pl.program_id(0)
    slot = i % 2

    def issue(p_ref, s):
        def start(j, c):
            for k in range(TOP_K):
                pltpu.make_async_copy(y_hbm.at[pl.ds(p_ref[0, 0, TOP_K * j + k], 1)],
                                      buf.at[s, k, pl.ds(j, 1)], sems.at[s]).start()
            return c
        lax.fori_loop(0, r, start, 0, unroll=DMA_UNROLL)

    @pl.when(i == 0)
    def _():
        issue(pos_ref, 0)

    @pl.when(i + 1 < pl.num_programs(0))
    def _():
        issue(pos_next_ref, 1 - slot)

    for k in range(TOP_K):
        pltpu.make_async_copy(y_hbm.at[pl.ds(0, r)], buf.at[slot, k], sems.at[slot]).wait()
    gate = gate_ref[...]
    out = x_ref[...] + (gate[:, 0:1] * buf[slot, 0] + gate[:, 1:2] * buf[slot, 1])

    def write(o_ref):
        o_ref[...] = out
    _select_rows(i, o_refs, bounds, write)


def _combine(pos, x, gates, y, out_rows, *, r):
    m, d = x.shape
    out_shape = [jax.ShapeDtypeStruct((n, d), F32) for n in out_rows]
    o_specs, bounds, n_row_tiles = _split_specs(out_shape, r, d)
    assert n_row_tiles * r == m
    pos3 = pos.reshape(n_row_tiles, 1, TOP_K * r)
    return pl.pallas_call(
        functools.partial(_combine_kernel, bounds=bounds),
        grid=(n_row_tiles,),
        in_specs=[
            pl.BlockSpec((1, 1, TOP_K * r), lambda i: (i, 0, 0), memory_space=pltpu.SMEM),
            pl.BlockSpec((1, 1, TOP_K * r), lambda i: (jnp.minimum(i + 1, n_row_tiles - 1), 0, 0),
                         memory_space=pltpu.SMEM),
            pl.BlockSpec((r, d), lambda i: (i, 0)),
            pl.BlockSpec((r, LANES), lambda i: (i, 0)),
            pl.BlockSpec(memory_space=pl.ANY),
        ],
        out_specs=o_specs,
        out_shape=out_shape,
        scratch_shapes=[pltpu.VMEM((2, TOP_K, r, d), F32), pltpu.SemaphoreType.DMA((2,))],
        compiler_params=_params(("arbitrary",)),
    )(pos3, pos3, x, gates, y)


def _route_plan(idx, n_tokens, tm):
    n_pairs = n_tokens * TOP_K
    n_tiles = n_pairs // tm + N_EXPERTS
    e_flat = idx[:, :TOP_K].reshape(n_pairs)
    onehot = (e_flat[:, None] == jnp.arange(N_EXPERTS)[None, :]).astype(jnp.int32)
    csum = jnp.cumsum(onehot, axis=0)
    counts = csum[-1]
    padded = ((counts + tm - 1) // tm) * tm
    pad_end = jnp.cumsum(padded)
    pad_start = pad_end - padded
    pos = jnp.sum((csum - onehot + pad_start[None, :]) * onehot, axis=1)
    tile_start = jnp.arange(n_tiles) * tm
    tile_expert = jnp.minimum(jnp.sum(tile_start[:, None] >= pad_end[None, :], axis=1), N_EXPERTS - 1)
    n_used = pad_end[-1:] // tm
    return (pos.astype(jnp.int32), pad_end.astype(jnp.int32), tile_expert.astype(jnp.int32),
            n_used.astype(jnp.int32))


def _trunk(xs, seqs, mix_norm, ffn_norm, na_w_qkv, na_q_gain, na_k_gain, na_rpb, na_w_o,
           mla_w_dqkv, mla_q_lora_gain, mla_kv_lora_gain, mla_w_uq, mla_w_ukv,
           mla_qn_gain, mla_qr_gain, mla_kn_gain, mla_kr_gain, mla_w_o,
           ffn_w_gate, ffn_w_up, ffn_w_down, moe_w_router, moe_w_gate, moe_w_up, moe_w_down):
    m = sum(x.shape[0] for x in xs)
    row2 = lambda a: a.reshape(1, -1).astype(F32)
    na_dim = NA_HEADS * NA_HEAD_DIM

    head_gain = jnp.concatenate([
        jnp.tile(na_q_gain[0] * NA_HEAD_DIM ** -0.5, NA_HEADS),
        jnp.tile(na_k_gain[0], NA_HEADS),
        jnp.ones((na_dim,), F32)]).reshape(1, -1)
    qkv = _norm_matmul(xs, row2(mix_norm[0]), na_w_qkv[0].astype(BF16), head_gain, 2 * na_dim,
                       tm=min(1024, min(x.shape[0] for x in xs)), tn=1024)
    bias = _na_bias_table(na_rpb[0])
    amax = lambda a: jnp.max(jnp.abs(a.astype(F32)))
    na_bound = (NA_HEAD_DIM ** 0.5 * amax(na_q_gain[0]) * amax(na_k_gain[0]) + amax(na_rpb[0]))
    na_safe = _safe_flag(na_bound)
    ffn_w = [ffn_w_gate[0], ffn_w_up[0], ffn_w_down[0]]
    host = max(range(len(seqs)), key=lambda gi: seqs[gi][1])
    ffn_w_bf16 = None
    attn = None
    off = 0
    for gi, (seq, nb) in enumerate(seqs):
        attn, casts = _na_attention(na_safe, qkv, bias, attn, ffn_w if gi == host else [],
                                    seq=seq, n_batch=nb, row_block_off=off // seq)
        ffn_w_bf16 = casts if casts is not None else ffn_w_bf16
        off += seq * nb
    if ffn_w_bf16 is None:
        ffn_w_bf16 = [w.astype(BF16) for w in ffn_w]
    x = _matmul_res(attn, na_w_o[0].astype(BF16), xs, tm=min(512, m))

    moe_w = (moe_w_gate[0], moe_w_up[0], moe_w_down[0])
    x, moe_w_bf16 = _ffn(x, row2(ffn_norm[0]), *ffn_w_bf16, [w.reshape(-1, w.shape[-1]) for w in moe_w],
                         tm=min(512, m), tf=512)
    moe_w_bf16 = [wb.reshape(w.shape) for wb, w in zip(moe_w_bf16, moe_w)]

    scale = (MLA_NOPE + MLA_ROPE) ** -0.5
    lat_w = MLA_Q_LORA + MLA_KV_LORA
    wd = jnp.concatenate((mla_w_dqkv[0][:, :lat_w],) + _rope_pieces(mla_w_dqkv[0][:, lat_w:]), axis=1).astype(BF16)
    wuq = mla_w_uq[0].reshape(MLA_Q_LORA, MLA_HEADS, MLA_NOPE + MLA_ROPE)
    wuq = jnp.concatenate(
        [wuq[:, :, :MLA_NOPE].reshape(MLA_Q_LORA, -1)]
        + [p.reshape(MLA_Q_LORA, -1) for p in _rope_pieces(wuq[:, :, MLA_NOPE:])], axis=1).astype(BF16)
    wukv = mla_w_ukv[0].reshape(MLA_KV_LORA, MLA_HEADS, MLA_NOPE + MLA_V)
    wukv = jnp.concatenate([wukv[:, :, :MLA_NOPE].reshape(MLA_KV_LORA, -1),
                            wukv[:, :, MLA_NOPE:].reshape(MLA_KV_LORA, -1)], axis=1).astype(BF16)
    gain_pieces = lambda gg: [p.reshape(1, -1).astype(F32) for p in _rope_pieces(gg)]
    max_seq = max(s for s, _ in seqs)
    cos, sin = _rope_tables(max_seq)
    tm_p = min(512, m)
    bounds = []
    off = 0
    for seq, nb in seqs:
        bounds.append((off // tm_p, seq // tm_p))
        off += seq * nb

    def table_block(i):
        blk = i
        for start, per in bounds:
            blk = jnp.where(i >= start, (i - start) % per, blk)
        return blk

    qn, qr, kn, kr, v = _mla_proj(
        x, row2(mix_norm[1]), wd, row2(mla_q_lora_gain[0]), row2(mla_kv_lora_gain[0]),
        *gain_pieces(mla_kr_gain[0]), wuq, row2(mla_qn_gain[0] * scale), *gain_pieces(mla_qr_gain[0] * scale),
        wukv, row2(mla_kn_gain[0]), cos, sin, table_block, tm=tm_p)
    mla_bound = scale * (MLA_NOPE * amax(mla_qn_gain[0]) * amax(mla_kn_gain[0])
                         + MLA_ROPE * amax(mla_qr_gain[0]) * amax(mla_kr_gain[0]))
    mla_safe = _safe_flag(mla_bound)
    attn = None
    off = 0
    for seq, nb in seqs:
        tq = min(FLASH_TQ, seq)
        attn = _flash(mla_safe, qn, qr, kn, kr, v, attn, seq=seq, n_batch=nb, row_off=off,
                      tq=tq, tk=min(FLASH_TK, seq), hb=FLASH_HEADS, sub_q=min(FLASH_SUB_Q, tq))
        off += seq * nb
    x = _matmul_res(attn, mla_w_o[0].astype(BF16), [x], tm=min(512, m))

    wr = jnp.pad(moe_w_router[0], ((0, 0), (0, LANES - N_EXPERTS))).astype(F32)
    idx, gates = _router(x, row2(ffn_norm[1]), wr, tm=min(512, m))
    tm_e = min(512, m)
    pos, pad_end, tile_expert, n_used = _route_plan(idx, m, tm_e)
    r = min(512, min(xx.shape[0] for xx in xs))
    x_sorted = _scatter_rows(pad_end, pos, x, tile_expert.shape[0] * tm_e, r=r, tm=tm_e)
    y_sorted = _expert_ffn(tile_expert, n_used, x_sorted, row2(ffn_norm[1]), *moe_w_bf16, tm=tm_e, tf=512)
    return _combine(pos, x, gates, y_sorted, [xx.shape[0] for xx in xs], r=r)


def kernel(x_prompt, x_sample, mix_norm, ffn_norm, na_w_qkv, na_q_gain, na_k_gain, na_rpb, na_w_o, mla_w_dqkv, mla_q_lora_gain, mla_kv_lora_gain, mla_w_uq, mla_w_ukv, mla_qn_gain, mla_qr_gain, mla_kn_gain, mla_kr_gain, mla_w_o, ffn_w_gate, ffn_w_up, ffn_w_down, moe_w_router, moe_w_gate, moe_w_up, moe_w_down):
    bp, sp, d = x_prompt.shape
    bs, ss, _ = x_sample.shape
    yp, ys = _trunk([x_prompt.reshape(bp * sp, d), x_sample.reshape(bs * ss, d)], [(sp, bp), (ss, bs)],
                    mix_norm, ffn_norm, na_w_qkv, na_q_gain, na_k_gain, na_rpb, na_w_o,
                    mla_w_dqkv, mla_q_lora_gain, mla_kv_lora_gain, mla_w_uq, mla_w_ukv,
                    mla_qn_gain, mla_qr_gain, mla_kn_gain, mla_kr_gain, mla_w_o,
                    ffn_w_gate, ffn_w_up, ffn_w_down, moe_w_router, moe_w_gate, moe_w_up, moe_w_down)
    return (yp.reshape(bp, sp, d), ys.reshape(bs, ss, d))
```

```python
import functools

import jax
import jax.numpy as jnp
from jax import lax
from jax.experimental import pallas as pl
from jax.experimental.pallas import tpu as pltpu

F32 = jnp.float32
BF16 = jnp.bfloat16

D_MODEL = 2048
GRID_W = 64
NA_HEADS = 16
NA_HEAD_DIM = 128
NA_WIN_H = 8
NA_WIN_W = 16
MLA_HEADS = 16
MLA_Q_LORA = 512
MLA_KV_LORA = 512
MLA_NOPE = 128
MLA_ROPE = 64
MLA_V = 128
ROPE_THETA = 10000.0
D_FF = 5632
N_EXPERTS = 8
TOP_K = 2
EPS = 1e-6
NEG_INF = -1e30

LANES = 128
VMEM_LIMIT = 56 * 1024 * 1024


def _params(sem):
    return pltpu.CompilerParams(dimension_semantics=sem, vmem_limit_bytes=VMEM_LIMIT)


def _rms(x, g):
    return x * lax.rsqrt(jnp.mean(x * x, axis=-1, keepdims=True) + EPS) * g


def _select_rows(i, refs, bounds, fn):
    for ref, (lo, hi) in zip(refs, bounds):
        @pl.when((i >= lo) & (i < hi))
        def _(ref=ref):
            fn(ref)


def _split_specs(xs, tm, ncols):
    specs, bounds, lo = [], [], 0
    for x in xs:
        n = x.shape[0] // tm
        assert n * tm == x.shape[0]
        specs.append(pl.BlockSpec(
            (tm, ncols), functools.partial(lambda i, *_, lo, n: (jnp.clip(i - lo, 0, n - 1), 0), lo=lo, n=n)))
        bounds.append((lo, lo + n))
        lo += n
    return specs, tuple(bounds), lo


def _norm_matmul_kernel(*refs, n_norm_tiles, bounds):
    nx = len(bounds)
    x_refs = refs[:nx]
    g_ref, w_ref, hg_ref, o_ref, h_scr = refs[nx:]
    j = pl.program_id(1)

    @pl.when(j == 0)
    def _():
        def fill(x_ref):
            h_scr[...] = _rms(x_ref[...], g_ref[...]).astype(BF16)
        _select_rows(pl.program_id(0), x_refs, bounds, fill)

    tn = o_ref.shape[1]
    chunk = min(tn, 2 * LANES)

    def emit(normed):
        for c in range(tn // chunk):
            acc = jnp.dot(h_scr[...], w_ref[:, c * chunk:(c + 1) * chunk], preferred_element_type=F32)
            for g in range(chunk // LANES):
                sl = slice(c * chunk + g * LANES, c * chunk + (g + 1) * LANES)
                a = acc[:, g * LANES:(g + 1) * LANES]
                o_ref[:, sl] = (_rms(a, hg_ref[:, sl]) if normed else a).astype(o_ref.dtype)

    @pl.when(j < n_norm_tiles)
    def _():
        emit(True)

    @pl.when(j >= n_norm_tiles)
    def _():
        emit(False)


def _norm_matmul(xs, g, w, head_gain, n_norm_cols, *, tm, tn):
    d = xs[0].shape[1]
    n = w.shape[1]
    x_specs, bounds, n_row_tiles = _split_specs(xs, tm, d)
    return pl.pallas_call(
        functools.partial(_norm_matmul_kernel, n_norm_tiles=n_norm_cols // tn, bounds=bounds),
        grid=(n_row_tiles, n // tn),
        in_specs=x_specs + [
            pl.BlockSpec((1, d), lambda i, j: (0, 0)),
            pl.BlockSpec((d, tn), lambda i, j: (0, j)),
            pl.BlockSpec((1, tn), lambda i, j: (0, j)),
        ],
        out_specs=pl.BlockSpec((tm, tn), lambda i, j: (i, j)),
        out_shape=jax.ShapeDtypeStruct((n_row_tiles * tm, n), BF16),
        scratch_shapes=[pltpu.VMEM((tm, d), BF16)],
        compiler_params=_params(("parallel", "arbitrary")),
    )(*xs, g, w, head_gain)


NA_QROWS = 4
NA_SPAN = NA_WIN_H + NA_QROWS


UNSHIFTED_SOFTMAX_BOUND = 60.0


def _safe_flag(bound):
    return (bound * 1.02 <= UNSHIFTED_SOFTMAX_BOUND).astype(jnp.int32).reshape(1)


def _na_kernel(safe_ref, q_ref, k_ref, v_ref, b_ref, *rest, rows, cast_ranges):
    nc = len(cast_ranges)
    cast_src, o_ref, cast_dst = rest[:nc], rest[-1 - nc], rest[len(rest) - nc:]
    _side_cast(pl.program_id(0) * pl.num_programs(1) + pl.program_id(1), cast_src, cast_dst, cast_ranges)
    nq = NA_QROWS * GRID_W
    nk = NA_SPAN * GRID_W
    n_blk = rows // NA_QROWS

    def run(shift):
        def body(blk, carry):
            r0 = blk * NA_QROWS
            span0 = jnp.clip(r0 - NA_WIN_H // 2, 0, rows - NA_SPAN)
            case = jnp.where(blk == 0, 0, jnp.where(blk == n_blk - 1, 2, 1))
            qoff = pl.multiple_of(r0 * GRID_W, nq)
            koff = pl.multiple_of(span0 * GRID_W, GRID_W)
            q = q_ref[pl.ds(qoff, nq), :]
            k = k_ref[pl.ds(koff, nk), :]
            v = v_ref[pl.ds(koff, nk), :]
            s = lax.dot_general(q, k, (((1,), (1,)), ((), ())), preferred_element_type=F32)
            s = s + jnp.concatenate(
                [jnp.concatenate([b_ref[0, case, i, jj] for jj in range(NA_SPAN // 2)], axis=-1)
                 for i in range(NA_QROWS)], axis=0)
            if shift:
                s = s - jnp.max(s, axis=-1, keepdims=True)
            p = jnp.exp(s)
            l = jnp.sum(p, axis=-1, keepdims=True)
            o = jnp.dot(p.astype(BF16), v, preferred_element_type=F32) / l
            o_ref[pl.ds(qoff, nq), :] = o.astype(o_ref.dtype)
            return carry

        lax.fori_loop(0, n_blk, body, 0, unroll=8)

    @pl.when(safe_ref[0] == 1)
    def _():
        run(False)

    @pl.when(safe_ref[0] != 1)
    def _():
        run(True)


def _na_bias_table(rpb):
    qc = jnp.arange(GRID_W)[:, None]
    kc = jnp.arange(GRID_W)[None, :]
    ws = jnp.clip(qc - NA_WIN_W // 2, 0, GRID_W - NA_WIN_W)
    col_ok = (kc >= ws) & (kc < ws + NA_WIN_W)
    dc = jnp.clip(kc - qc + NA_WIN_W - 1, 0, 2 * NA_WIN_W - 2)
    i = jnp.arange(NA_QROWS)
    half = NA_WIN_H // 2
    w_off = jnp.stack([jnp.zeros_like(i), i, jnp.full_like(i, NA_SPAN - NA_WIN_H)])
    c_off = jnp.stack([i, jnp.full_like(i, half), half + i])
    j = jnp.arange(NA_SPAN)[None, None, :] - w_off[:, :, None]
    row_ok = (j >= 0) & (j < NA_WIN_H)
    dr = jnp.clip(j - c_off[:, :, None] + NA_WIN_H - 1, 0, 2 * NA_WIN_H - 2)
    oh_r = jax.nn.one_hot(dr, 2 * NA_WIN_H - 1, dtype=F32).reshape(3, NA_QROWS, NA_SPAN // 2, 2, -1)
    oh_c = jax.nn.one_hot(dc, 2 * NA_WIN_W - 1, dtype=F32)
    z = jnp.zeros_like(oh_c)
    oh_c2 = jnp.stack([jnp.concatenate([oh_c, z], axis=1), jnp.concatenate([z, oh_c], axis=1)])
    u = jnp.einsum('hab,pqcb->phaqc', rpb.astype(F32), oh_c2, precision=lax.Precision.HIGHEST)
    t = jnp.einsum('xijpa,phaqc->hxijqc', oh_r, u, precision=lax.Precision.HIGHEST)
    ok = (row_ok.reshape(3, NA_QROWS, NA_SPAN // 2, 1, 2, 1) & col_ok[:, None, :]).reshape(
        3, NA_QROWS, NA_SPAN // 2, GRID_W, 2 * GRID_W)
    return jnp.where(ok[None], t, NEG_INF)


def _na_attention(safe, qkv, bias, out_prev, cast_arrays, *, seq, n_batch, row_block_off):
    m = qkv.shape[0]
    rows = seq // GRID_W
    assert rows >= NA_SPAN + NA_QROWS and rows % NA_QROWS == 0
    h = NA_HEADS
    plan = _side_cast_plan(cast_arrays, h * n_batch, lambda hh, b: hh * n_batch + b) if cast_arrays else None
    c_in, c_out, c_shapes, ranges = plan if plan is not None else ([], [], [], ())

    def spec(sec):
        return pl.BlockSpec((seq, NA_HEAD_DIM), lambda hh, b: (row_block_off + b, sec * h + hh))

    in_specs = [pl.BlockSpec(memory_space=pltpu.SMEM), spec(0), spec(1), spec(2),
                pl.BlockSpec((1,) + bias.shape[1:], lambda hh, b: (hh,) + (0,) * (bias.ndim - 1))] + c_in
    args = [safe, qkv, qkv, qkv, bias] + (list(cast_arrays) if plan is not None else [])
    aliases = {}
    if out_prev is not None:
        in_specs.append(pl.BlockSpec(memory_space=pl.ANY))
        args.append(out_prev)
        aliases = {len(args) - 1: 0}
    outs = pl.pallas_call(
        functools.partial(_na_kernel, rows=rows, cast_ranges=ranges),
        grid=(h, n_batch),
        in_specs=in_specs,
        out_specs=[pl.BlockSpec((seq, NA_HEAD_DIM), lambda hh, b: (row_block_off + b, hh))] + c_out,
        out_shape=[jax.ShapeDtypeStruct((m, h * NA_HEAD_DIM), BF16)] + c_shapes,
        input_output_aliases=aliases,
        compiler_params=_params(("arbitrary", "arbitrary")),
    )(*args)
    return outs[0], (list(outs[1:]) if plan is not None else None)


def _matmul_res_kernel(a_ref, w_ref, *refs, bounds):
    r_refs, o_ref = refs[:-1], refs[-1]
    y = jnp.dot(a_ref[...], w_ref[...], preferred_element_type=F32)

    def add(r_ref):
        o_ref[...] = r_ref[...] + y
    _select_rows(pl.program_id(0), r_refs, bounds, add)


def _matmul_res(a, w, res_list, *, tm):
    m, k = a.shape
    n = w.shape[1]
    r_specs, bounds, n_row_tiles = _split_specs(res_list, tm, n)
    assert n_row_tiles * tm == m
    return pl.pallas_call(
        functools.partial(_matmul_res_kernel, bounds=bounds),
        grid=(n_row_tiles,),
        in_specs=[
            pl.BlockSpec((tm, k), lambda i: (i, 0)),
            pl.BlockSpec((k, n), lambda i: (0, 0)),
        ] + r_specs,
        out_specs=pl.BlockSpec((tm, n), lambda i: (i, 0)),
        out_shape=jax.ShapeDtypeStruct((m, n), F32),
        compiler_params=_params(("parallel",)),
    )(a, w, *res_list)


def _swiglu_step(h, wg_ref, wu_ref, wd_ref, lead=()):
    tf = wd_ref.shape[-2]
    chunk = min(tf, 2 * LANES)
    out = None
    for c in range(tf // chunk):
        cs = slice(c * chunk, (c + 1) * chunk)
        a = jnp.dot(h, wg_ref[lead + (slice(None), cs)], preferred_element_type=F32)
        b = jnp.dot(h, wu_ref[lead + (slice(None), cs)], preferred_element_type=F32)
        t = (a / (1.0 + jnp.exp(-a)) * b).astype(BF16)
        y = jnp.dot(t, wd_ref[lead + (cs, slice(None))], preferred_element_type=F32)
        out = y if out is None else out + y
    return out


CAST_BLOCK_BYTES = 3 * 512 * 1024


def _side_cast_plan(arrays, n_steps, step_fn):
    budget = n_steps // max(len(arrays), 1)
    in_specs, out_specs, out_shapes, ranges, off = [], [], [], [], 0
    for a in arrays:
        rows, cols = a.shape
        nb = next((n for n in range(budget, 0, -1)
                   if rows % n == 0 and (rows // n) % 16 == 0), None)
        if nb is None or (rows // nb) * cols * 4 > CAST_BLOCK_BYTES:
            return None
        imap = functools.partial(lambda *ids, off, nb: (jnp.clip(step_fn(*ids) - off, 0, nb - 1), 0),
                                 off=off, nb=nb)
        in_specs.append(pl.BlockSpec((rows // nb, cols), imap))
        out_specs.append(pl.BlockSpec((rows // nb, cols), imap))
        out_shapes.append(jax.ShapeDtypeStruct((rows, cols), BF16))
        ranges.append((off, off + nb))
        off += nb
    return in_specs, out_specs, out_shapes, tuple(ranges)


def _side_cast(step, src_refs, dst_refs, ranges):
    for src, dst, (lo, hi) in zip(src_refs, dst_refs, ranges):
        @pl.when((step >= lo) & (step < hi))
        def _(src=src, dst=dst):
            dst[...] = src[...].astype(BF16)


def _ffn_kernel(x_ref, g_ref, wg_ref, wu_ref, wd_ref, *rest, cast_ranges):
    nc = len(cast_ranges)
    cast_src, o_ref, cast_dst, h_scr = rest[:nc], rest[nc], rest[nc + 1:2 * nc + 1], rest[-1]
    f = pl.program_id(1)

    @pl.when(f == 0)
    def _():
        x = x_ref[...]
        h_scr[...] = _rms(x, g_ref[...]).astype(BF16)
        o_ref[...] = x

    o_ref[...] += _swiglu_step(h_scr[...], wg_ref, wu_ref, wd_ref)
    _side_cast(pl.program_id(0) * pl.num_programs(1) + f, cast_src, cast_dst, cast_ranges)


def _ffn(x, g, wg, wu, wd, cast_arrays, *, tm, tf):
    m, d = x.shape
    nf = wg.shape[1] // tf
    plan = _side_cast_plan(cast_arrays, (m // tm) * nf, lambda i, j: i * nf + j)
    if plan is None:
        c_in, c_out, c_shapes, ranges, cast_arrays, fallback = [], [], [], (), [], cast_arrays
    else:
        (c_in, c_out, c_shapes, ranges), fallback = plan, []
    outs = pl.pallas_call(
        functools.partial(_ffn_kernel, cast_ranges=ranges),
        grid=(m // tm, nf),
        in_specs=[
            pl.BlockSpec((tm, d), lambda i, j: (i, 0)),
            pl.BlockSpec((1, d), lambda i, j: (0, 0)),
            pl.BlockSpec((d, tf), lambda i, j: (0, j)),
            pl.BlockSpec((d, tf), lambda i, j: (0, j)),
            pl.BlockSpec((tf, d), lambda i, j: (j, 0)),
        ] + c_in,
        out_specs=[pl.BlockSpec((tm, d), lambda i, j: (i, 0))] + c_out,
        out_shape=[jax.ShapeDtypeStruct((m, d), F32)] + c_shapes,
        scratch_shapes=[pltpu.VMEM((tm, d), BF16)],
        compiler_params=_params(("arbitrary", "arbitrary")),
    )(x, g, wg, wu, wd, *cast_arrays)
    return outs[0], list(outs[1:]) + [a.astype(BF16) for a in fallback]


def _expert_ffn_kernel(te_ref, nu_ref, x_ref, g_ref, wg_ref, wu_ref, wd_ref, o_ref, h_scr):
    i = pl.program_id(0)
    f = pl.program_id(1)

    @pl.when(i < nu_ref[0])
    def _():
        @pl.when(f == 0)
        def _():
            h_scr[...] = _rms(x_ref[...], g_ref[...]).astype(BF16)
            o_ref[...] = jnp.zeros_like(o_ref)

        o_ref[...] += _swiglu_step(h_scr[...], wg_ref, wu_ref, wd_ref, lead=(0,))


def _expert_ffn(tile_expert, n_used, xs, g, wg, wu, wd, *, tm, tf):
    p, d = xs.shape
    nf = wg.shape[2] // tf

    def row_map(i, j, te, nu):
        return (jnp.minimum(i, nu[0] - 1), 0)

    def wcol_map(i, j, te, nu):
        used = i < nu[0]
        return (te[jnp.minimum(i, nu[0] - 1)], 0, jnp.where(used, j, nf - 1))

    def wrow_map(i, j, te, nu):
        used = i < nu[0]
        return (te[jnp.minimum(i, nu[0] - 1)], jnp.where(used, j, nf - 1), 0)

    grid_spec = pltpu.PrefetchScalarGridSpec(
        num_scalar_prefetch=2,
        grid=(p // tm, nf),
        in_specs=[
            pl.BlockSpec((tm, d), row_map),
            pl.BlockSpec((1, d), lambda i, j, te, nu: (0, 0)),
            pl.BlockSpec((1, d, tf), wcol_map),
            pl.BlockSpec((1, d, tf), wcol_map),
            pl.BlockSpec((1, tf, d), wrow_map),
        ],
        out_specs=pl.BlockSpec((tm, d), row_map),
        scratch_shapes=[pltpu.VMEM((tm, d), BF16)],
    )
    return pl.pallas_call(
        _expert_ffn_kernel,
        grid_spec=grid_spec,
        out_shape=jax.ShapeDtypeStruct((p, d), F32),
        compiler_params=_params(("arbitrary", "arbitrary")),
    )(tile_expert, n_used, xs, g, wg, wu, wd)


ROPE_PAD = LANES


def _rope_pieces(w):
    half = MLA_ROPE // 2
    x1, x2 = w[..., :half], w[..., half:]
    z = jnp.zeros(w.shape[:-1] + (ROPE_PAD - MLA_ROPE,), w.dtype)
    return jnp.concatenate([x1, x2, z], axis=-1), jnp.concatenate([x2, x1, z], axis=-1)


def _mla_proj_kernel(x_ref, g_ref, wd_ref, gq_ref, gkv_ref, gkr_ref, gkrs_ref, wuq_ref, gqn_ref, gqr_ref,
                     gqrs_ref, wukv_ref, gkn_ref, cos_ref, sin_ref,
                     qn_ref, qr_ref, kn_ref, kr_ref, v_ref):
    cos = cos_ref[...]
    sin = sin_ref[...]

    def rope(piece, piece_sw, gain, gain_sw):
        ms = jnp.sum(piece * piece, axis=-1, keepdims=True) * (1.0 / MLA_ROPE)
        return lax.rsqrt(ms + EPS) * (piece * (gain * cos) + piece_sw * (gain_sw * sin))

    h = _rms(x_ref[...], g_ref[...]).astype(BF16)
    lat = jnp.dot(h, wd_ref[...], preferred_element_type=F32)
    cq = _rms(lat[:, :MLA_Q_LORA], gq_ref[...]).astype(BF16)
    ckv = _rms(lat[:, MLA_Q_LORA:MLA_Q_LORA + MLA_KV_LORA], gkv_ref[...]).astype(BF16)
    kr0 = MLA_Q_LORA + MLA_KV_LORA
    kr_ref[...] = rope(lat[:, kr0:kr0 + ROPE_PAD], lat[:, kr0 + ROPE_PAD:],
                       gkr_ref[...], gkrs_ref[...]).astype(BF16)

    hd = MLA_HEADS * LANES
    chunk = 4 * LANES
    for c in range(hd // chunk):
        base = c * chunk
        qn = jnp.dot(cq, wuq_ref[:, base:base + chunk], preferred_element_type=F32)
        qr = jnp.dot(cq, wuq_ref[:, hd + base:hd + base + chunk], preferred_element_type=F32)
        qs = jnp.dot(cq, wuq_ref[:, 2 * hd + base:2 * hd + base + chunk], preferred_element_type=F32)
        kn = jnp.dot(ckv, wukv_ref[:, base:base + chunk], preferred_element_type=F32)
        v = jnp.dot(ckv, wukv_ref[:, hd + base:hd + base + chunk], preferred_element_type=F32)
        v_ref[:, base:base + chunk] = v.astype(BF16)
        for hh in range(chunk // LANES):
            sl = slice(hh * LANES, (hh + 1) * LANES)
            osl = slice(base + hh * LANES, base + (hh + 1) * LANES)
            qn_ref[:, osl] = _rms(qn[:, sl], gqn_ref[...]).astype(BF16)
            kn_ref[:, osl] = _rms(kn[:, sl], gkn_ref[...]).astype(BF16)
            qr_ref[:, osl] = rope(qr[:, sl], qs[:, sl], gqr_ref[...], gqrs_ref[...]).astype(BF16)


def _mla_proj(x, g, wd, gq, gkv, gkr, gkrs, wuq, gqn, gqr, gqrs, wukv, gkn, cos, sin, table_block, *, tm):
    m, d = x.shape
    hd = MLA_HEADS * LANES
    const = lambda i: (0, 0)
    row = lambda i: (i, 0)
    tab = lambda i: (table_block(i), 0)
    return pl.pallas_call(
        _mla_proj_kernel,
        grid=(m // tm,),
        in_specs=[
            pl.BlockSpec((tm, d), row),
            pl.BlockSpec((1, d), const),
            pl.BlockSpec(wd.shape, const, pipeline_mode=pl.Buffered(1)),
            pl.BlockSpec((1, MLA_Q_LORA), const),
            pl.BlockSpec((1, MLA_KV_LORA), const),
            pl.BlockSpec((1, LANES), const),
            pl.BlockSpec((1, LANES), const),
            pl.BlockSpec(wuq.shape, const, pipeline_mode=pl.Buffered(1)),
            pl.BlockSpec((1, LANES), const),
            pl.BlockSpec((1, LANES), const),
            pl.BlockSpec((1, LANES), const),
            pl.BlockSpec(wukv.shape, const, pipeline_mode=pl.Buffered(1)),
            pl.BlockSpec((1, LANES), const),
            pl.BlockSpec((tm, LANES), tab),
            pl.BlockSpec((tm, LANES), tab),
        ],
        out_specs=[
            pl.BlockSpec((tm, hd), row),
            pl.BlockSpec((tm, hd), row),
            pl.BlockSpec((tm, hd), row),
            pl.BlockSpec((tm, LANES), row),
            pl.BlockSpec((tm, hd), row),
        ],
        out_shape=[
            jax.ShapeDtypeStruct((m, hd), BF16),
            jax.ShapeDtypeStruct((m, hd), BF16),
            jax.ShapeDtypeStruct((m, hd), BF16),
            jax.ShapeDtypeStruct((m, LANES), BF16),
            jax.ShapeDtypeStruct((m, hd), BF16),
        ],
        compiler_params=_params(("parallel",)),
    )(x, g, wd, gq, gkv, gkr, gkrs, wuq, gqn, gqr, gqrs, wukv, gkn, cos, sin)


def _rope_tables(seq):
    t = jnp.arange(seq)
    row = (t // GRID_W).astype(F32)
    col = (t % GRID_W).astype(F32)
    n_pairs = MLA_ROPE // 4
    inv = ROPE_THETA ** (-jnp.arange(n_pairs, dtype=F32) / n_pairs)
    ang = jnp.concatenate([row[:, None] * inv, col[:, None] * inv], axis=-1)
    c, s = jnp.cos(ang), jnp.sin(ang)
    z = jnp.zeros_like(c)
    return jnp.concatenate([c, c, z, z], axis=-1), jnp.concatenate([-s, s, z, z], axis=-1)


FLASH_TQ = 1024
FLASH_TK = 2048
FLASH_HEADS = 2
FLASH_SUB_Q = 256


def _flash_kernel(safe_ref, qn_ref, qr_ref, kn_ref, kr_ref, v_ref, *rest, hb, sub_q, nk):
    if nk == 1:
        o_ref = rest[-1]
    else:
        o_ref, m_scr, l_scr, acc_scr = rest[-4:]
        kk = pl.program_id(3)

        @pl.when(kk == 0)
        def _():
            m_scr[...] = jnp.full_like(m_scr, -jnp.inf)
            l_scr[...] = jnp.zeros_like(l_scr)
            acc_scr[...] = jnp.zeros_like(acc_scr)

    tq = qn_ref.shape[0]

    def run(shift):
        kr = kr_ref[...]
        for h in range(hb):
            hs = slice(h * LANES, (h + 1) * LANES)
            k = jnp.concatenate([kn_ref[:, hs], kr], axis=-1)
            v = v_ref[:, hs]
            for qi in range(tq // sub_q):
                rs = slice(qi * sub_q, (qi + 1) * sub_q)
                q = jnp.concatenate([qn_ref[rs, hs], qr_ref[rs, hs]], axis=-1)
                s = lax.dot_general(q, k, (((1,), (1,)), ((), ())), preferred_element_type=F32)
                if not shift:
                    p = jnp.exp(s)
                    l = jnp.sum(p, axis=-1, keepdims=True)
                    pv = jnp.dot(p.astype(BF16), v, preferred_element_type=F32)
                    if nk == 1:
                        o_ref[rs, hs] = (pv / l).astype(o_ref.dtype)
                    else:
                        l_scr[h, rs] += l
                        acc_scr[rs, hs] += pv
                elif nk == 1:
                    p = jnp.exp(s - jnp.max(s, axis=-1, keepdims=True))
                    l = jnp.sum(p, axis=-1, keepdims=True)
                    o = jnp.dot(p.astype(BF16), v, preferred_element_type=F32) / l
                    o_ref[rs, hs] = o.astype(o_ref.dtype)
                else:
                    m_prev = m_scr[h, rs]
                    m_new = jnp.maximum(m_prev, jnp.max(s, axis=-1, keepdims=True))
                    alpha = jnp.exp(m_prev - m_new)
                    p = jnp.exp(s - m_new)
                    l_scr[h, rs] = alpha * l_scr[h, rs] + jnp.sum(p, axis=-1, keepdims=True)
                    acc_scr[rs, hs] = (alpha * acc_scr[rs, hs]
                                       + jnp.dot(p.astype(BF16), v, preferred_element_type=F32))
                    m_scr[h, rs] = m_new

    @pl.when(safe_ref[0] == 1)
    def _():
        run(False)

    @pl.when(safe_ref[0] != 1)
    def _():
        run(True)

    if nk > 1:
        @pl.when(kk == nk - 1)
        def _():
            for h in range(hb):
                hs = slice(h * LANES, (h + 1) * LANES)
                o_ref[:, hs] = (acc_scr[:, hs] / l_scr[h]).astype(o_ref.dtype)


def _flash(safe, qn, qr, kn, kr, v, out_prev, *, seq, n_batch, row_off, tq, tk, hb, sub_q):
    m = qn.shape[0]
    h = MLA_HEADS
    nq, nk = seq // tq, seq // tk
    qoff, koff = row_off // tq, row_off // tk
    qmap = lambda b, hh, i, j: (qoff + b * nq + i, hh)
    kmap = lambda b, hh, i, j: (koff + b * nk + j, hh)
    in_specs = [
        pl.BlockSpec(memory_space=pltpu.SMEM),
        pl.BlockSpec((tq, hb * LANES), qmap),
        pl.BlockSpec((tq, hb * LANES), qmap),
        pl.BlockSpec((tk, hb * LANES), kmap),
        pl.BlockSpec((tk, LANES), lambda b, hh, i, j: (koff + b * nk + j, 0)),
        pl.BlockSpec((tk, hb * LANES), kmap),
    ]
    args = [safe, qn, qr, kn, kr, v]
    aliases = {}
    if out_prev is not None:
        in_specs.append(pl.BlockSpec(memory_space=pl.ANY))
        args.append(out_prev)
        aliases = {6: 0}
    scratch = []
    if nk > 1:
        scratch = [pltpu.VMEM((hb, tq, 1), F32), pltpu.VMEM((hb, tq, 1), F32), pltpu.VMEM((tq, hb * MLA_V), F32)]
    return pl.pallas_call(
        functools.partial(_flash_kernel, hb=hb, sub_q=sub_q, nk=nk),
        grid=(n_batch, h // hb, nq, nk),
        in_specs=in_specs,
        out_specs=pl.BlockSpec((tq, hb * LANES), qmap),
        out_shape=jax.ShapeDtypeStruct((m, h * MLA_V), BF16),
        scratch_shapes=scratch,
        input_output_aliases=aliases,
        compiler_params=_params(("parallel", "parallel", "parallel", "arbitrary")),
    )(*args)


def _router_kernel(x_ref, g_ref, wr_ref, idx_ref, gate_ref):
    h = _rms(x_ref[...], g_ref[...])
    w = wr_ref[...]
    h_hi = h.astype(BF16)
    h_lo = (h - h_hi.astype(F32)).astype(BF16)
    w_hi = w.astype(BF16)
    w_lo = (w - w_hi.astype(F32)).astype(BF16)
    logits = (jnp.dot(h_hi, w_hi, preferred_element_type=F32)
              + (jnp.dot(h_lo, w_hi, preferred_element_type=F32) + jnp.dot(h_hi, w_lo, preferred_element_type=F32)))
    lane = lax.broadcasted_iota(jnp.int32, logits.shape, 1)
    logits = jnp.where(lane < N_EXPERTS, logits, -jnp.inf)
    m1 = jnp.max(logits, axis=-1, keepdims=True)
    i1 = jnp.min(jnp.where(logits == m1, lane, LANES), axis=-1, keepdims=True)
    rest = jnp.where(lane == i1, -jnp.inf, logits)
    m2 = jnp.max(rest, axis=-1, keepdims=True)
    i2 = jnp.min(jnp.where(rest == m2, lane, LANES), axis=-1, keepdims=True)
    e = jnp.exp(m2 - m1)
    g1 = 1.0 / (1.0 + e)
    g2 = e / (1.0 + e)
    idx_ref[...] = jnp.where(lane == 0, i1, jnp.where(lane == 1, i2, 0))
    gate_ref[...] = jnp.where(lane == 0, g1, jnp.where(lane == 1, g2, 0.0))


def _router(x, g, wr, *, tm):
    m, d = x.shape
    return pl.pallas_call(
        _router_kernel,
        grid=(m // tm,),
        in_specs=[
            pl.BlockSpec((tm, d), lambda i: (i, 0)),
            pl.BlockSpec((1, d), lambda i: (0, 0)),
            pl.BlockSpec((d, LANES), lambda i: (0, 0)),
        ],
        out_specs=[pl.BlockSpec((tm, LANES), lambda i: (i, 0)), pl.BlockSpec((tm, LANES), lambda i: (i, 0))],
        out_shape=[jax.ShapeDtypeStruct((m, LANES), jnp.int32), jax.ShapeDtypeStruct((m, LANES), F32)],
        compiler_params=_params(("parallel",)),
    )(x, g, wr)


DMA_UNROLL = 8


def _scatter_kernel(pe_ref, pos_ref, x_ref, o_hbm, zero_scr, sem, zsem, *, tm):
    r = x_ref.shape[0]

    def zero_copy(e):
        end = pe_ref[e]
        return pltpu.make_async_copy(zero_scr, o_hbm.at[pl.ds(pl.multiple_of(end - tm, tm), tm)], zsem)

    def nonempty(e):
        return pe_ref[e] > (pe_ref[e - 1] if e else 0)

    @pl.when(pl.program_id(0) == 0)
    def _():
        zero_scr[...] = jnp.zeros_like(zero_scr)
        for e in range(N_EXPERTS):
            @pl.when(nonempty(e))
            def _(e=e):
                zero_copy(e).start()
        for e in range(N_EXPERTS):
            @pl.when(nonempty(e))
            def _(e=e):
                zero_copy(e).wait()

    def row_copy(j, row):
        return pltpu.make_async_copy(x_ref.at[pl.ds(j, 1)], o_hbm.at[pl.ds(row, 1)], sem)

    def start(j, c):
        for k in range(TOP_K):
            row_copy(j, pos_ref[0, 0, TOP_K * j + k]).start()
        return c

    lax.fori_loop(0, r, start, 0, unroll=DMA_UNROLL)
    for k in range(TOP_K):
        pltpu.make_async_copy(x_ref, o_hbm.at[pl.ds(0, r)], sem).wait()


def _scatter_rows(pad_end, pos, x, p_rows, *, r, tm):
    m, d = x.shape
    grid_spec = pltpu.PrefetchScalarGridSpec(
        num_scalar_prefetch=1,
        grid=(m // r,),
        in_specs=[
            pl.BlockSpec((1, 1, TOP_K * r), lambda i, pe: (i, 0, 0), memory_space=pltpu.SMEM),
            pl.BlockSpec((r, d), lambda i, pe: (i, 0)),
        ],
        out_specs=pl.BlockSpec(memory_space=pl.ANY),
        scratch_shapes=[pltpu.VMEM((tm, d), F32), pltpu.SemaphoreType.DMA(()), pltpu.SemaphoreType.DMA(())],
    )
    return pl.pallas_call(
        functools.partial(_scatter_kernel, tm=tm),
        grid_spec=grid_spec,
        out_shape=jax.ShapeDtypeStruct((p_rows, d), F32),
        compiler_params=_params(("arbitrary",)),
    )(pad_end, pos.reshape(m // r, 1, TOP_K * r), x)


def _combine_kernel(pos_ref, pos_next_ref, x_ref, gate_ref, y_hbm, *rest, bounds):
    o_refs = rest[:len(bounds)]
    buf, sems = rest[len(bounds):]
    r = x_ref.shape[0]
    i = pl.program_id(0)
    slot = i % 2

    def issue(p_ref, s):
        def start(j, c):
            for k in range(TOP_K):
                pltpu.make_async_copy(y_hbm.at[pl.ds(p_ref[0, 0, TOP_K * j + k], 1)],
                                      buf.at[s, k, pl.ds(j, 1)], sems.at[s]).start()
            return c
        lax.fori_loop(0, r, start, 0, unroll=DMA_UNROLL)

    @pl.when(i == 0)
    def _():
        issue(pos_ref, 0)

    @pl.when(i + 1 < pl.num_programs(0))
    def _():
        issue(pos_next_ref, 1 - slot)

    for k in range(TOP_K):
        pltpu.make_async_copy(y_hbm.at[pl.ds(0, r)], buf.at[slot, k], sems.at[slot]).wait()
    gate = gate_ref[...]
    out = x_ref[...] + (gate[:, 0:1] * buf[slot, 0] + gate[:, 1:2] * buf[slot, 1])

    def write(o_ref):
        o_ref[...] = out
    _select_rows(i, o_refs, bounds, write)


def _combine(pos, x, gates, y, out_rows, *, r):
    m, d = x.shape
    out_shape = [jax.ShapeDtypeStruct((n, d), F32) for n in out_rows]
    o_specs, bounds, n_row_tiles = _split_specs(out_shape, r, d)
    assert n_row_tiles * r == m
    pos3 = pos.reshape(n_row_tiles, 1, TOP_K * r)
    return pl.pallas_call(
        functools.partial(_combine_kernel, bounds=bounds),
        grid=(n_row_tiles,),
        in_specs=[
            pl.BlockSpec((1, 1, TOP_K * r), lambda i: (i, 0, 0), memory_space=pltpu.SMEM),
            pl.BlockSpec((1, 1, TOP_K * r), lambda i: (jnp.minimum(i + 1, n_row_tiles - 1), 0, 0),
                         memory_space=pltpu.SMEM),
            pl.BlockSpec((r, d), lambda i: (i, 0)),
            pl.BlockSpec((r, LANES), lambda i: (i, 0)),
            pl.BlockSpec(memory_space=pl.ANY),
        ],
        out_specs=o_specs,
        out_shape=out_shape,
        scratch_shapes=[pltpu.VMEM((2, TOP_K, r, d), F32), pltpu.SemaphoreType.DMA((2,))],
        compiler_params=_params(("arbitrary",)),
    )(pos3, pos3, x, gates, y)


def _route_plan(idx, n_tokens, tm):
    n_pairs = n_tokens * TOP_K
    n_tiles = n_pairs // tm + N_EXPERTS
    e_flat = idx[:, :TOP_K].reshape(n_pairs)
    onehot = (e_flat[:, None] == jnp.arange(N_EXPERTS)[None, :]).astype(jnp.int32)
    csum = jnp.cumsum(onehot, axis=0)
    counts = csum[-1]
    padded = ((counts + tm - 1) // tm) * tm
    pad_end = jnp.cumsum(padded)
    pad_start = pad_end - padded
    pos = jnp.sum((csum - onehot + pad_start[None, :]) * onehot, axis=1)
    tile_start = jnp.arange(n_tiles) * tm
    tile_expert = jnp.minimum(jnp.sum(tile_start[:, None] >= pad_end[None, :], axis=1), N_EXPERTS - 1)
    n_used = pad_end[-1:] // tm
    return (pos.astype(jnp.int32), pad_end.astype(jnp.int32), tile_expert.astype(jnp.int32),
            n_used.astype(jnp.int32))


def _trunk(xs, seqs, mix_norm, ffn_norm, na_w_qkv, na_q_gain, na_k_gain, na_rpb, na_w_o,
           mla_w_dqkv, mla_q_lora_gain, mla_kv_lora_gain, mla_w_uq, mla_w_ukv,
           mla_qn_gain, mla_qr_gain, mla_kn_gain, mla_kr_gain, mla_w_o,
           ffn_w_gate, ffn_w_up, ffn_w_down, moe_w_router, moe_w_gate, moe_w_up, moe_w_down):
    m = sum(x.shape[0] for x in xs)
    row2 = lambda a: a.reshape(1, -1).astype(F32)
    na_dim = NA_HEADS * NA_HEAD_DIM

    head_gain = jnp.concatenate([
        jnp.tile(na_q_gain[0] * NA_HEAD_DIM ** -0.5, NA_HEADS),
        jnp.tile(na_k_gain[0], NA_HEADS),
        jnp.ones((na_dim,), F32)]).reshape(1, -1)
    qkv = _norm_matmul(xs, row2(mix_norm[0]), na_w_qkv[0].astype(BF16), head_gain, 2 * na_dim,
                       tm=min(1024, min(x.shape[0] for x in xs)), tn=1024)
    bias = _na_bias_table(na_rpb[0])
    amax = lambda a: jnp.max(jnp.abs(a.astype(F32)))
    na_bound = (NA_HEAD_DIM ** 0.5 * amax(na_q_gain[0]) * amax(na_k_gain[0]) + amax(na_rpb[0]))
    na_safe = _safe_flag(na_bound)
    ffn_w = [ffn_w_gate[0], ffn_w_up[0], ffn_w_down[0]]
    host = max(range(len(seqs)), key=lambda gi: seqs[gi][1])
    ffn_w_bf16 = None
    attn = None
    off = 0
    for gi, (seq, nb) in enumerate(seqs):
        attn, casts = _na_attention(na_safe, qkv, bias, attn, ffn_w if gi == host else [],
                                    seq=seq, n_batch=nb, row_block_off=off // seq)
        ffn_w_bf16 = casts if casts is not None else ffn_w_bf16
        off += seq * nb
    if ffn_w_bf16 is None:
        ffn_w_bf16 = [w.astype(BF16) for w in ffn_w]
    x = _matmul_res(attn, na_w_o[0].astype(BF16), xs, tm=min(512, m))

    moe_w = (moe_w_gate[0], moe_w_up[0], moe_w_down[0])
    x, moe_w_bf16 = _ffn(x, row2(ffn_norm[0]), *ffn_w_bf16, [w.reshape(-1, w.shape[-1]) for w in moe_w],
                         tm=min(512, m), tf=512)
    moe_w_bf16 = [wb.reshape(w.shape) for wb, w in zip(moe_w_bf16, moe_w)]

    scale = (MLA_NOPE + MLA_ROPE) ** -0.5
    lat_w = MLA_Q_LORA + MLA_KV_LORA
    wd = jnp.concatenate((mla_w_dqkv[0][:, :lat_w],) + _rope_pieces(mla_w_dqkv[0][:, lat_w:]), axis=1).astype(BF16)
    wuq = mla_w_uq[0].reshape(MLA_Q_LORA, MLA_HEADS, MLA_NOPE + MLA_ROPE)
    wuq = jnp.concatenate(
        [wuq[:, :, :MLA_NOPE].reshape(MLA_Q_LORA, -1)]
        + [p.reshape(MLA_Q_LORA, -1) for p in _rope_pieces(wuq[:, :, MLA_NOPE:])], axis=1).astype(BF16)
    wukv = mla_w_ukv[0].reshape(MLA_KV_LORA, MLA_HEADS, MLA_NOPE + MLA_V)
    wukv = jnp.concatenate([wukv[:, :, :MLA_NOPE].reshape(MLA_KV_LORA, -1),
                            wukv[:, :, MLA_NOPE:].reshape(MLA_KV_LORA, -1)], axis=1).astype(BF16)
    gain_pieces = lambda gg: [p.reshape(1, -1).astype(F32) for p in _rope_pieces(gg)]
    max_seq = max(s for s, _ in seqs)
    cos, sin = _rope_tables(max_seq)
    tm_p = min(512, m)
    bounds = []
    off = 0
    for seq, nb in seqs:
        bounds.append((off // tm_p, seq // tm_p))
        off += seq * nb

    def table_block(i):
        blk = i
        for start, per in bounds:
            blk = jnp.where(i >= start, (i - start) % per, blk)
        return blk

    qn, qr, kn, kr, v = _mla_proj(
        x, row2(mix_norm[1]), wd, row2(mla_q_lora_gain[0]), row2(mla_kv_lora_gain[0]),
        *gain_pieces(mla_kr_gain[0]), wuq, row2(mla_qn_gain[0] * scale), *gain_pieces(mla_qr_gain[0] * scale),
        wukv, row2(mla_kn_gain[0]), cos, sin, table_block, tm=tm_p)
    mla_bound = scale * (MLA_NOPE * amax(mla_qn_gain[0]) * amax(mla_kn_gain[0])
                         + MLA_ROPE * amax(mla_qr_gain[0]) * amax(mla_kr_gain[0]))
    mla_safe = _safe_flag(mla_bound)
    attn = None
    off = 0
    for seq, nb in seqs:
        tq = min(FLASH_TQ, seq)
        attn = _flash(mla_safe, qn, qr, kn, kr, v, attn, seq=seq, n_batch=nb, row_off=off,
                      tq=tq, tk=min(FLASH_TK, seq), hb=FLASH_HEADS, sub_q=min(FLASH_SUB_Q, tq))
        off += seq * nb
    x = _matmul_res(attn, mla_w_o[0].astype(BF16), [x], tm=min(512, m))

    wr = jnp.pad(moe_w_router[0], ((0, 0), (0, LANES - N_EXPERTS))).astype(F32)
    idx, gates = _router(x, row2(ffn_norm[1]), wr, tm=min(512, m))
    tm_e = min(512, m)
    pos, pad_end, tile_expert, n_used = _route_plan(idx, m, tm_e)
    r = min(512, min(xx.shape[0] for xx in xs))
    x_sorted = _scatter_rows(pad_end, pos, x, tile_expert.shape[0] * tm_e, r=r, tm=tm_e)
    y_sorted = _expert_ffn(tile_expert, n_used, x_sorted, row2(ffn_norm[1]), *moe_w_bf16, tm=tm_e, tf=512)
    return _combine(pos, x, gates, y_sorted, [xx.shape[0] for xx in xs], r=r)


def kernel(x_prompt, x_sample, mix_norm, ffn_norm, na_w_qkv, na_q_gain, na_k_gain, na_rpb, na_w_o, mla_w_dqkv, mla_q_lora_gain, mla_kv_lora_gain, mla_w_uq, mla_w_ukv, mla_qn_gain, mla_qr_gain, mla_kn_gain, mla_kr_gain, mla_w_o, ffn_w_gate, ffn_w_up, ffn_w_down, moe_w_router, moe_w_gate, moe_w_up, moe_w_down):
    bp, sp, d = x_prompt.shape
    bs, ss, _ = x_sample.shape
    yp, ys = _trunk([x_prompt.reshape(bp * sp, d), x_sample.reshape(bs * ss, d)], [(sp, bp), (ss, bs)],
                    mix_norm, ffn_norm, na_w_qkv, na_q_gain, na_k_gain, na_rpb, na_w_o,
                    mla_w_dqkv, mla_q_lora_gain, mla_kv_lora_gain, mla_w_uq, mla_w_ukv,
                    mla_qn_gain, mla_qr_gain, mla_kn_gain, mla_kr_gain, mla_w_o,
                    ffn_w_gate, ffn_w_up, ffn_w_down, moe_w_router, moe_w_gate, moe_w_up, moe_w_down)
    return (yp.reshape(bp, sp, d), ys.reshape(bs, ss, d))
```

```python
import functools

import jax
import jax.numpy as jnp
from jax import lax
from jax.experimental import pallas as pl
from jax.experimental.pallas import tpu as pltpu

F32 = jnp.float32
BF16 = jnp.bfloat16

D_MODEL = 2048
GRID_W = 64
NA_HEADS = 16
NA_HEAD_DIM = 128
NA_WIN_H = 8
NA_WIN_W = 16
MLA_HEADS = 16
MLA_Q_LORA = 512
MLA_KV_LORA = 512
MLA_NOPE = 128
MLA_ROPE = 64
MLA_V = 128
ROPE_THETA = 10000.0
D_FF = 5632
N_EXPERTS = 8
TOP_K = 2
EPS = 1e-6
NEG_INF = -1e30

LANES = 128
VMEM_LIMIT = 56 * 1024 * 1024


def _params(sem):
    return pltpu.CompilerParams(dimension_semantics=sem, vmem_limit_bytes=VMEM_LIMIT)


def _rms(x, g):
    return x * lax.rsqrt(jnp.mean(x * x, axis=-1, keepdims=True) + EPS) * g


def _select_rows(i, refs, bounds, fn):
    for ref, (lo, hi) in zip(refs, bounds):
        @pl.when((i >= lo) & (i < hi))
        def _(ref=ref):
            fn(ref)


def _split_specs(xs, tm, ncols):
    specs, bounds, lo = [], [], 0
    for x in xs:
        n = x.shape[0] // tm
        assert n * tm == x.shape[0]
        specs.append(pl.BlockSpec(
            (tm, ncols), functools.partial(lambda i, *_, lo, n: (jnp.clip(i - lo, 0, n - 1), 0), lo=lo, n=n)))
        bounds.append((lo, lo + n))
        lo += n
    return specs, tuple(bounds), lo


def _norm_matmul_kernel(*refs, n_norm_tiles, bounds):
    nx = len(bounds)
    x_refs = refs[:nx]
    g_ref, w_ref, hg_ref, o_ref, h_scr = refs[nx:]
    j = pl.program_id(1)

    @pl.when(j == 0)
    def _():
        def fill(x_ref):
            h_scr[...] = _rms(x_ref[...], g_ref[...]).astype(BF16)
        _select_rows(pl.program_id(0), x_refs, bounds, fill)

    tn = o_ref.shape[1]
    chunk = min(tn, 2 * LANES)

    def emit(normed):
        for c in range(tn // chunk):
            acc = jnp.dot(h_scr[...], w_ref[:, c * chunk:(c + 1) * chunk], preferred_element_type=F32)
            for g in range(chunk // LANES):
                sl = slice(c * chunk + g * LANES, c * chunk + (g + 1) * LANES)
                a = acc[:, g * LANES:(g + 1) * LANES]
                o_ref[:, sl] = (_rms(a, hg_ref[:, sl]) if normed else a).astype(o_ref.dtype)

    @pl.when(j < n_norm_tiles)
    def _():
        emit(True)

    @pl.when(j >= n_norm_tiles)
    def _():
        emit(False)


def _norm_matmul(xs, g, w, head_gain, n_norm_cols, *, tm, tn):
    d = xs[0].shape[1]
    n = w.shape[1]
    x_specs, bounds, n_row_tiles = _split_specs(xs, tm, d)
    return pl.pallas_call(
        functools.partial(_norm_matmul_kernel, n_norm_tiles=n_norm_cols // tn, bounds=bounds),
        grid=(n_row_tiles, n // tn),
        in_specs=x_specs + [
            pl.BlockSpec((1, d), lambda i, j: (0, 0)),
            pl.BlockSpec((d, tn), lambda i, j: (0, j)),
            pl.BlockSpec((1, tn), lambda i, j: (0, j)),
        ],
        out_specs=pl.BlockSpec((tm, tn), lambda i, j: (i, j)),
        out_shape=jax.ShapeDtypeStruct((n_row_tiles * tm, n), BF16),
        scratch_shapes=[pltpu.VMEM((tm, d), BF16)],
        compiler_params=_params(("parallel", "arbitrary")),
    )(*xs, g, w, head_gain)


NA_QROWS = 4
NA_SPAN = NA_WIN_H + NA_QROWS


UNSHIFTED_SOFTMAX_BOUND = 60.0


def _safe_flag(bound):
    return (bound * 1.02 <= UNSHIFTED_SOFTMAX_BOUND).astype(jnp.int32).reshape(1)


def _na_kernel(safe_ref, q_ref, k_ref, v_ref, b_ref, *rest, rows, cast_ranges):
    nc = len(cast_ranges)
    cast_src, o_ref, cast_dst = rest[:nc], rest[-1 - nc], rest[len(rest) - nc:]
    _side_cast(pl.program_id(0) * pl.num_programs(1) + pl.program_id(1), cast_src, cast_dst, cast_ranges)
    nq = NA_QROWS * GRID_W
    nk = NA_SPAN * GRID_W
    n_blk = rows // NA_QROWS

    def run(shift):
        def body(blk, carry):
            r0 = blk * NA_QROWS
            span0 = jnp.clip(r0 - NA_WIN_H // 2, 0, rows - NA_SPAN)
            case = jnp.where(blk == 0, 0, jnp.where(blk == n_blk - 1, 2, 1))
            qoff = pl.multiple_of(r0 * GRID_W, nq)
            koff = pl.multiple_of(span0 * GRID_W, GRID_W)
            q = q_ref[pl.ds(qoff, nq), :]
            k = k_ref[pl.ds(koff, nk), :]
            v = v_ref[pl.ds(koff, nk), :]
            s = lax.dot_general(q, k, (((1,), (1,)), ((), ())), preferred_element_type=F32)
            s = s + jnp.concatenate(
                [jnp.concatenate([b_ref[0, case, i, jj] for jj in range(NA_SPAN // 2)], axis=-1)
                 for i in range(NA_QROWS)], axis=0)
            if shift:
                s = s - jnp.max(s, axis=-1, keepdims=True)
            p = jnp.exp(s)
            l = jnp.sum(p, axis=-1, keepdims=True)
            o = jnp.dot(p.astype(BF16), v, preferred_element_type=F32) / l
            o_ref[pl.ds(qoff, nq), :] = o.astype(o_ref.dtype)
            return carry

        lax.fori_loop(0, n_blk, body, 0, unroll=8)

    @pl.when(safe_ref[0] == 1)
    def _():
        run(False)

    @pl.when(safe_ref[0] != 1)
    def _():
        run(True)


def _na_bias_table(rpb):
    qc = jnp.arange(GRID_W)[:, None]
    kc = jnp.arange(GRID_W)[None, :]
    ws = jnp.clip(qc - NA_WIN_W // 2, 0, GRID_W - NA_WIN_W)
    col_ok = (kc >= ws) & (kc < ws + NA_WIN_W)
    dc = jnp.clip(kc - qc + NA_WIN_W - 1, 0, 2 * NA_WIN_W - 2)
    i = jnp.arange(NA_QROWS)
    half = NA_WIN_H // 2
    w_off = jnp.stack([jnp.zeros_like(i), i, jnp.full_like(i, NA_SPAN - NA_WIN_H)])
    c_off = jnp.stack([i, jnp.full_like(i, half), half + i])
    j = jnp.arange(NA_SPAN)[None, None, :] - w_off[:, :, None]
    row_ok = (j >= 0) & (j < NA_WIN_H)
    dr = jnp.clip(j - c_off[:, :, None] + NA_WIN_H - 1, 0, 2 * NA_WIN_H - 2)
    oh_r = jax.nn.one_hot(dr, 2 * NA_WIN_H - 1, dtype=F32).reshape(3, NA_QROWS, NA_SPAN // 2, 2, -1)
    oh_c = jax.nn.one_hot(dc, 2 * NA_WIN_W - 1, dtype=F32)
    z = jnp.zeros_like(oh_c)
    oh_c2 = jnp.stack([jnp.concatenate([oh_c, z], axis=1), jnp.concatenate([z, oh_c], axis=1)])
    u = jnp.einsum('hab,pqcb->phaqc', rpb.astype(F32), oh_c2, precision=lax.Precision.HIGHEST)
    t = jnp.einsum('xijpa,phaqc->hxijqc', oh_r, u, precision=lax.Precision.HIGHEST)
    ok = (row_ok.reshape(3, NA_QROWS, NA_SPAN // 2, 1, 2, 1) & col_ok[:, None, :]).reshape(
        3, NA_QROWS, NA_SPAN // 2, GRID_W, 2 * GRID_W)
    return jnp.where(ok[None], t, NEG_INF)


def _na_attention(safe, qkv, bias, out_prev, cast_arrays, *, seq, n_batch, row_block_off):
    m = qkv.shape[0]
    rows = seq // GRID_W
    assert rows >= NA_SPAN + NA_QROWS and rows % NA_QROWS == 0
    h = NA_HEADS
    plan = _side_cast_plan(cast_arrays, h * n_batch, lambda hh, b: hh * n_batch + b) if cast_arrays else None
    c_in, c_out, c_shapes, ranges = plan if plan is not None else ([], [], [], ())

    def spec(sec):
        return pl.BlockSpec((seq, NA_HEAD_DIM), lambda hh, b: (row_block_off + b, sec * h + hh))

    in_specs = [pl.BlockSpec(memory_space=pltpu.SMEM), spec(0), spec(1), spec(2),
                pl.BlockSpec((1,) + bias.shape[1:], lambda hh, b: (hh,) + (0,) * (bias.ndim - 1))] + c_in
    args = [safe, qkv, qkv, qkv, bias] + (list(cast_arrays) if plan is not None else [])
    aliases = {}
    if out_prev is not None:
        in_specs.append(pl.BlockSpec(memory_space=pl.ANY))
        args.append(out_prev)
        aliases = {len(args) - 1: 0}
    outs = pl.pallas_call(
        functools.partial(_na_kernel, rows=rows, cast_ranges=ranges),
        grid=(h, n_batch),
        in_specs=in_specs,
        out_specs=[pl.BlockSpec((seq, NA_HEAD_DIM), lambda hh, b: (row_block_off + b, hh))] + c_out,
        out_shape=[jax.ShapeDtypeStruct((m, h * NA_HEAD_DIM), BF16)] + c_shapes,
        input_output_aliases=aliases,
        compiler_params=_params(("arbitrary", "arbitrary")),
    )(*args)
    return outs[0], (list(outs[1:]) if plan is not None else None)


def _matmul_res_kernel(a_ref, w_ref, *refs, bounds):
    r_refs, o_ref = refs[:-1], refs[-1]
    y = jnp.dot(a_ref[...], w_ref[...], preferred_element_type=F32)

    def add(r_ref):
        o_ref[...] = r_ref[...] + y
    _select_rows(pl.program_id(0), r_refs, bounds, add)


def _matmul_res(a, w, res_list, *, tm):
    m, k = a.shape
    n = w.shape[1]
    r_specs, bounds, n_row_tiles = _split_specs(res_list, tm, n)
    assert n_row_tiles * tm == m
    return pl.pallas_call(
        functools.partial(_matmul_res_kernel, bounds=bounds),
        grid=(n_row_tiles,),
        in_specs=[
            pl.BlockSpec((tm, k), lambda i: (i, 0)),
            pl.BlockSpec((k, n), lambda i: (0, 0)),
        ] + r_specs,
        out_specs=pl.BlockSpec((tm, n), lambda i: (i, 0)),
        out_shape=jax.ShapeDtypeStruct((m, n), F32),
        compiler_params=_params(("parallel",)),
    )(a, w, *res_list)


def _swiglu_step(h, wg_ref, wu_ref, wd_ref, lead=(), cols=None):
    start, stop = cols if cols is not None else (0, wd_ref.shape[-2])
    chunk = min(stop - start, 2 * LANES)
    out = None
    for c in range(start // chunk, stop // chunk):
        cs = slice(c * chunk, (c + 1) * chunk)
        a = jnp.dot(h, wg_ref[lead + (slice(None), cs)], preferred_element_type=F32)
        b = jnp.dot(h, wu_ref[lead + (slice(None), cs)], preferred_element_type=F32)
        t = (a / (1.0 + jnp.exp(-a)) * b).astype(BF16)
        y = jnp.dot(t, wd_ref[lead + (cs, slice(None))], preferred_element_type=F32)
        out = y if out is None else out + y
    return out


CAST_BLOCK_BYTES = 3 * 512 * 1024


def _side_cast_plan(arrays, n_steps, step_fn):
    budget = n_steps // max(len(arrays), 1)
    in_specs, out_specs, out_shapes, ranges, off = [], [], [], [], 0
    for a in arrays:
        rows, cols = a.shape
        nb = next((n for n in range(budget, 0, -1)
                   if rows % n == 0 and (rows // n) % 16 == 0), None)
        if nb is None or (rows // nb) * cols * 4 > CAST_BLOCK_BYTES:
            return None
        imap = functools.partial(lambda *ids, off, nb: (jnp.clip(step_fn(*ids) - off, 0, nb - 1), 0),
                                 off=off, nb=nb)
        in_specs.append(pl.BlockSpec((rows // nb, cols), imap))
        out_specs.append(pl.BlockSpec((rows // nb, cols), imap))
        out_shapes.append(jax.ShapeDtypeStruct((rows, cols), BF16))
        ranges.append((off, off + nb))
        off += nb
    return in_specs, out_specs, out_shapes, tuple(ranges)


def _side_cast(step, src_refs, dst_refs, ranges):
    for src, dst, (lo, hi) in zip(src_refs, dst_refs, ranges):
        @pl.when((step >= lo) & (step < hi))
        def _(src=src, dst=dst):
            dst[...] = src[...].astype(BF16)


def _ffn_kernel(x_ref, g_ref, wg_ref, wu_ref, wd_ref, *rest, cast_ranges):
    nc = len(cast_ranges)
    cast_src, o_ref, cast_dst, h_scr = rest[:nc], rest[nc], rest[nc + 1:2 * nc + 1], rest[-1]
    f = pl.program_id(1)

    @pl.when(f == 0)
    def _():
        x = x_ref[...]
        h_scr[...] = _rms(x, g_ref[...]).astype(BF16)
        o_ref[...] = x

    o_ref[...] += _swiglu_step(h_scr[...], wg_ref, wu_ref, wd_ref)
    _side_cast(pl.program_id(0) * pl.num_programs(1) + f, cast_src, cast_dst, cast_ranges)


def _ffn(x, g, wg, wu, wd, cast_arrays, *, tm, tf):
    m, d = x.shape
    nf = wg.shape[1] // tf
    plan = _side_cast_plan(cast_arrays, (m // tm) * nf, lambda i, j: i * nf + j)
    if plan is None:
        c_in, c_out, c_shapes, ranges, cast_arrays, fallback = [], [], [], (), [], cast_arrays
    else:
        (c_in, c_out, c_shapes, ranges), fallback = plan, []
    outs = pl.pallas_call(
        functools.partial(_ffn_kernel, cast_ranges=ranges),
        grid=(m // tm, nf),
        in_specs=[
            pl.BlockSpec((tm, d), lambda i, j: (i, 0)),
            pl.BlockSpec((1, d), lambda i, j: (0, 0)),
            pl.BlockSpec((d, tf), lambda i, j: (0, j)),
            pl.BlockSpec((d, tf), lambda i, j: (0, j)),
            pl.BlockSpec((tf, d), lambda i, j: (j, 0)),
        ] + c_in,
        out_specs=[pl.BlockSpec((tm, d), lambda i, j: (i, 0))] + c_out,
        out_shape=[jax.ShapeDtypeStruct((m, d), F32)] + c_shapes,
        scratch_shapes=[pltpu.VMEM((tm, d), BF16)],
        compiler_params=_params(("arbitrary", "arbitrary")),
    )(x, g, wg, wu, wd, *cast_arrays)
    return outs[0], list(outs[1:]) + [a.astype(BF16) for a in fallback]


def _expert_ffn_kernel(te_ref, nu_ref, x_ref, g_ref, wg_ref, wu_ref, wd_ref, o_ref, h_scr, *, group, last_valid):
    i = pl.program_id(0)
    f = pl.program_id(1)

    @pl.when(i < nu_ref[0])
    def _():
        @pl.when(f == 0)
        def _():
            h_scr[...] = _rms(x_ref[...], g_ref[...]).astype(BF16)
            o_ref[...] = jnp.zeros_like(o_ref)

        tf = wd_ref.shape[-2]
        for lo in range(0, tf, group):
            if lo < last_valid:
                o_ref[...] += _swiglu_step(h_scr[...], wg_ref, wu_ref, wd_ref, lead=(0,), cols=(lo, lo + group))
            else:
                @pl.when(f < pl.num_programs(1) - 1)
                def _(lo=lo):
                    o_ref[...] += _swiglu_step(h_scr[...], wg_ref, wu_ref, wd_ref, lead=(0,),
                                               cols=(lo, lo + group))


def _expert_ffn(tile_expert, n_used, xs, g, wg, wu, wd, *, tm, tf, group):
    p, d = xs.shape
    n_ff = wg.shape[2]
    nf = pl.cdiv(n_ff, tf)
    last_valid = n_ff - (nf - 1) * tf
    assert tf % group == 0 and last_valid % group == 0

    def row_map(i, j, te, nu):
        return (jnp.minimum(i, nu[0] - 1), 0)

    def wcol_map(i, j, te, nu):
        used = i < nu[0]
        return (te[jnp.minimum(i, nu[0] - 1)], 0, jnp.where(used, j, nf - 1))

    def wrow_map(i, j, te, nu):
        used = i < nu[0]
        return (te[jnp.minimum(i, nu[0] - 1)], jnp.where(used, j, nf - 1), 0)

    grid_spec = pltpu.PrefetchScalarGridSpec(
        num_scalar_prefetch=2,
        grid=(p // tm, nf),
        in_specs=[
            pl.BlockSpec((tm, d), row_map),
            pl.BlockSpec((1, d), lambda i, j, te, nu: (0, 0)),
            pl.BlockSpec((1, d, tf), wcol_map),
            pl.BlockSpec((1, d, tf), wcol_map),
            pl.BlockSpec((1, tf, d), wrow_map),
        ],
        out_specs=pl.BlockSpec((tm, d), row_map),
        scratch_shapes=[pltpu.VMEM((tm, d), BF16)],
    )
    return pl.pallas_call(
        functools.partial(_expert_ffn_kernel, group=group, last_valid=last_valid),
        grid_spec=grid_spec,
        out_shape=jax.ShapeDtypeStruct((p, d), F32),
        compiler_params=_params(("arbitrary", "arbitrary")),
    )(tile_expert, n_used, xs, g, wg, wu, wd)


ROPE_PAD = LANES


def _rope_pieces(w):
    half = MLA_ROPE // 2
    x1, x2 = w[..., :half], w[..., half:]
    z = jnp.zeros(w.shape[:-1] + (ROPE_PAD - MLA_ROPE,), w.dtype)
    return jnp.concatenate([x1, x2, z], axis=-1), jnp.concatenate([x2, x1, z], axis=-1)


def _mla_proj_kernel(x_ref, g_ref, wd_ref, gq_ref, gkv_ref, gkr_ref, gkrs_ref, wuq_ref, gqn_ref, gqr_ref,
                     gqrs_ref, wukv_ref, gkn_ref, cos_ref, sin_ref,
                     qn_ref, qr_ref, kn_ref, kr_ref, v_ref):
    cos = cos_ref[...]
    sin = sin_ref[...]

    def rope(piece, piece_sw, gain, gain_sw):
        ms = jnp.sum(piece * piece, axis=-1, keepdims=True) * (1.0 / MLA_ROPE)
        return lax.rsqrt(ms + EPS) * (piece * (gain * cos) + piece_sw * (gain_sw * sin))

    h = _rms(x_ref[...], g_ref[...]).astype(BF16)
    lat = jnp.dot(h, wd_ref[...], preferred_element_type=F32)
    cq = _rms(lat[:, :MLA_Q_LORA], gq_ref[...]).astype(BF16)
    ckv = _rms(lat[:, MLA_Q_LORA:MLA_Q_LORA + MLA_KV_LORA], gkv_ref[...]).astype(BF16)
    kr0 = MLA_Q_LORA + MLA_KV_LORA
    kr_ref[...] = rope(lat[:, kr0:kr0 + ROPE_PAD], lat[:, kr0 + ROPE_PAD:],
                       gkr_ref[...], gkrs_ref[...]).astype(BF16)

    hd = MLA_HEADS * LANES
    chunk = 4 * LANES
    for c in range(hd // chunk):
        base = c * chunk
        qn = jnp.dot(cq, wuq_ref[:, base:base + chunk], preferred_element_type=F32)
        qr = jnp.dot(cq, wuq_ref[:, hd + base:hd + base + chunk], preferred_element_type=F32)
        qs = jnp.dot(cq, wuq_ref[:, 2 * hd + base:2 * hd + base + chunk], preferred_element_type=F32)
        kn = jnp.dot(ckv, wukv_ref[:, base:base + chunk], preferred_element_type=F32)
        v = jnp.dot(ckv, wukv_ref[:, hd + base:hd + base + chunk], preferred_element_type=F32)
        v_ref[:, base:base + chunk] = v.astype(BF16)
        for hh in range(chunk // LANES):
            sl = slice(hh * LANES, (hh + 1) * LANES)
            osl = slice(base + hh * LANES, base + (hh + 1) * LANES)
            qn_ref[:, osl] = _rms(qn[:, sl], gqn_ref[...]).astype(BF16)
            kn_ref[:, osl] = _rms(kn[:, sl], gkn_ref[...]).astype(BF16)
            qr_ref[:, osl] = rope(qr[:, sl], qs[:, sl], gqr_ref[...], gqrs_ref[...]).astype(BF16)


def _mla_proj(x, g, wd, gq, gkv, gkr, gkrs, wuq, gqn, gqr, gqrs, wukv, gkn, cos, sin, table_block, *, tm):
    m, d = x.shape
    hd = MLA_HEADS * LANES
    const = lambda i: (0, 0)
    row = lambda i: (i, 0)
    tab = lambda i: (table_block(i), 0)
    return pl.pallas_call(
        _mla_proj_kernel,
        grid=(m // tm,),
        in_specs=[
            pl.BlockSpec((tm, d), row),
            pl.BlockSpec((1, d), const),
            pl.BlockSpec(wd.shape, const, pipeline_mode=pl.Buffered(1)),
            pl.BlockSpec((1, MLA_Q_LORA), const),
            pl.BlockSpec((1, MLA_KV_LORA), const),
            pl.BlockSpec((1, LANES), const),
            pl.BlockSpec((1, LANES), const),
            pl.BlockSpec(wuq.shape, const, pipeline_mode=pl.Buffered(1)),
            pl.BlockSpec((1, LANES), const),
            pl.BlockSpec((1, LANES), const),
            pl.BlockSpec((1, LANES), const),
            pl.BlockSpec(wukv.shape, const, pipeline_mode=pl.Buffered(1)),
            pl.BlockSpec((1, LANES), const),
            pl.BlockSpec((tm, LANES), tab),
            pl.BlockSpec((tm, LANES), tab),
        ],
        out_specs=[
            pl.BlockSpec((tm, hd), row),
            pl.BlockSpec((tm, hd), row),
            pl.BlockSpec((tm, hd), row),
            pl.BlockSpec((tm, LANES), row),
            pl.BlockSpec((tm, hd), row),
        ],
        out_shape=[
            jax.ShapeDtypeStruct((m, hd), BF16),
            jax.ShapeDtypeStruct((m, hd), BF16),
            jax.ShapeDtypeStruct((m, hd), BF16),
            jax.ShapeDtypeStruct((m, LANES), BF16),
            jax.ShapeDtypeStruct((m, hd), BF16),
        ],
        compiler_params=_params(("parallel",)),
    )(x, g, wd, gq, gkv, gkr, gkrs, wuq, gqn, gqr, gqrs, wukv, gkn, cos, sin)


def _rope_tables(seq):
    t = jnp.arange(seq)
    row = (t // GRID_W).astype(F32)
    col = (t % GRID_W).astype(F32)
    n_pairs = MLA_ROPE // 4
    inv = ROPE_THETA ** (-jnp.arange(n_pairs, dtype=F32) / n_pairs)
    ang = jnp.concatenate([row[:, None] * inv, col[:, None] * inv], axis=-1)
    c, s = jnp.cos(ang), jnp.sin(ang)
    z = jnp.zeros_like(c)
    return jnp.concatenate([c, c, z, z], axis=-1), jnp.concatenate([-s, s, z, z], axis=-1)


FLASH_TQ = 1024
FLASH_TK = 2048
FLASH_HEADS = 2
FLASH_SUB_Q = 256


def _flash_kernel(safe_ref, qn_ref, qr_ref, kn_ref, kr_ref, v_ref, *rest, hb, sub_q, nk):
    if nk == 1:
        o_ref = rest[-1]
    else:
        o_ref, m_scr, l_scr, acc_scr = rest[-4:]
        kk = pl.program_id(3)

        @pl.when(kk == 0)
        def _():
            m_scr[...] = jnp.full_like(m_scr, -jnp.inf)
            l_scr[...] = jnp.zeros_like(l_scr)
            acc_scr[...] = jnp.zeros_like(acc_scr)

    tq = qn_ref.shape[0]

    def run(shift):
        kr = kr_ref[...]
        for h in range(hb):
            hs = slice(h * LANES, (h + 1) * LANES)
            k = jnp.concatenate([kn_ref[:, hs], kr], axis=-1)
            v = v_ref[:, hs]
            for qi in range(tq // sub_q):
                rs = slice(qi * sub_q, (qi + 1) * sub_q)
                q = jnp.concatenate([qn_ref[rs, hs], qr_ref[rs, hs]], axis=-1)
                s = lax.dot_general(q, k, (((1,), (1,)), ((), ())), preferred_element_type=F32)
                if not shift:
                    p = jnp.exp(s)
                    l = jnp.sum(p, axis=-1, keepdims=True)
                    pv = jnp.dot(p.astype(BF16), v, preferred_element_type=F32)
                    if nk == 1:
                        o_ref[rs, hs] = (pv / l).astype(o_ref.dtype)
                    else:
                        l_scr[h, rs] += l
                        acc_scr[rs, hs] += pv
                elif nk == 1:
                    p = jnp.exp(s - jnp.max(s, axis=-1, keepdims=True))
                    l = jnp.sum(p, axis=-1, keepdims=True)
                    o = jnp.dot(p.astype(BF16), v, preferred_element_type=F32) / l
                    o_ref[rs, hs] = o.astype(o_ref.dtype)
                else:
                    m_prev = m_scr[h, rs]
                    m_new = jnp.maximum(m_prev, jnp.max(s, axis=-1, keepdims=True))
                    alpha = jnp.exp(m_prev - m_new)
                    p = jnp.exp(s - m_new)
                    l_scr[h, rs] = alpha * l_scr[h, rs] + jnp.sum(p, axis=-1, keepdims=True)
                    acc_scr[rs, hs] = (alpha * acc_scr[rs, hs]
                                       + jnp.dot(p.astype(BF16), v, preferred_element_type=F32))
                    m_scr[h, rs] = m_new

    @pl.when(safe_ref[0] == 1)
    def _():
        run(False)

    @pl.when(safe_ref[0] != 1)
    def _():
        run(True)

    if nk > 1:
        @pl.when(kk == nk - 1)
        def _():
            for h in range(hb):
                hs = slice(h * LANES, (h + 1) * LANES)
                o_ref[:, hs] = (acc_scr[:, hs] / l_scr[h]).astype(o_ref.dtype)


def _flash(safe, qn, qr, kn, kr, v, out_prev, *, seq, n_batch, row_off, tq, tk, hb, sub_q):
    m = qn.shape[0]
    h = MLA_HEADS
    nq, nk = seq // tq, seq // tk
    qoff, koff = row_off // tq, row_off // tk
    qmap = lambda b, hh, i, j: (qoff + b * nq + i, hh)
    kmap = lambda b, hh, i, j: (koff + b * nk + j, hh)
    in_specs = [
        pl.BlockSpec(memory_space=pltpu.SMEM),
        pl.BlockSpec((tq, hb * LANES), qmap),
        pl.BlockSpec((tq, hb * LANES), qmap),
        pl.BlockSpec((tk, hb * LANES), kmap),
        pl.BlockSpec((tk, LANES), lambda b, hh, i, j: (koff + b * nk + j, 0)),
        pl.BlockSpec((tk, hb * LANES), kmap),
    ]
    args = [safe, qn, qr, kn, kr, v]
    aliases = {}
    if out_prev is not None:
        in_specs.append(pl.BlockSpec(memory_space=pl.ANY))
        args.append(out_prev)
        aliases = {6: 0}
    scratch = []
    if nk > 1:
        scratch = [pltpu.VMEM((hb, tq, 1), F32), pltpu.VMEM((hb, tq, 1), F32), pltpu.VMEM((tq, hb * MLA_V), F32)]
    return pl.pallas_call(
        functools.partial(_flash_kernel, hb=hb, sub_q=sub_q, nk=nk),
        grid=(n_batch, h // hb, nq, nk),
        in_specs=in_specs,
        out_specs=pl.BlockSpec((tq, hb * LANES), qmap),
        out_shape=jax.ShapeDtypeStruct((m, h * MLA_V), BF16),
        scratch_shapes=scratch,
        input_output_aliases=aliases,
        compiler_params=_params(("parallel", "parallel", "parallel", "arbitrary")),
    )(*args)


def _router_kernel(x_ref, g_ref, wr_ref, idx_ref, gate_ref):
    h = _rms(x_ref[...], g_ref[...])
    w = wr_ref[...]
    h_hi = h.astype(BF16)
    h_lo = (h - h_hi.astype(F32)).astype(BF16)
    w_hi = w.astype(BF16)
    w_lo = (w - w_hi.astype(F32)).astype(BF16)
    logits = (jnp.dot(h_hi, w_hi, preferred_element_type=F32)
              + (jnp.dot(h_lo, w_hi, preferred_element_type=F32) + jnp.dot(h_hi, w_lo, preferred_element_type=F32)))
    lane = lax.broadcasted_iota(jnp.int32, logits.shape, 1)
    logits = jnp.where(lane < N_EXPERTS, logits, -jnp.inf)
    m1 = jnp.max(logits, axis=-1, keepdims=True)
    i1 = jnp.min(jnp.where(logits == m1, lane, LANES), axis=-1, keepdims=True)
    rest = jnp.where(lane == i1, -jnp.inf, logits)
    m2 = jnp.max(rest, axis=-1, keepdims=True)
    i2 = jnp.min(jnp.where(rest == m2, lane, LANES), axis=-1, keepdims=True)
    e = jnp.exp(m2 - m1)
    g1 = 1.0 / (1.0 + e)
    g2 = e / (1.0 + e)
    idx_ref[...] = jnp.where(lane == 0, i1, jnp.where(lane == 1, i2, 0))
    gate_ref[...] = jnp.where(lane == 0, g1, jnp.where(lane == 1, g2, 0.0))


def _router(x, g, wr, *, tm):
    m, d = x.shape
    return pl.pallas_call(
        _router_kernel,
        grid=(m // tm,),
        in_specs=[
            pl.BlockSpec((tm, d), lambda i: (i, 0)),
            pl.BlockSpec((1, d), lambda i: (0, 0)),
            pl.BlockSpec((d, LANES), lambda i: (0, 0)),
        ],
        out_specs=[pl.BlockSpec((tm, LANES), lambda i: (i, 0)), pl.BlockSpec((tm, LANES), lambda i: (i, 0))],
        out_shape=[jax.ShapeDtypeStruct((m, LANES), jnp.int32), jax.ShapeDtypeStruct((m, LANES), F32)],
        compiler_params=_params(("parallel",)),
    )(x, g, wr)


DMA_UNROLL = 8


def _scatter_kernel(pe_ref, pos_ref, x_ref, o_hbm, zero_scr, sem, zsem, *, tm):
    r = x_ref.shape[0]

    def zero_copy(e):
        end = pe_ref[e]
        return pltpu.make_async_copy(zero_scr, o_hbm.at[pl.ds(pl.multiple_of(end - tm, tm), tm)], zsem)

    def nonempty(e):
        return pe_ref[e] > (pe_ref[e - 1] if e else 0)

    @pl.when(pl.program_id(0) == 0)
    def _():
        zero_scr[...] = jnp.zeros_like(zero_scr)
        for e in range(N_EXPERTS):
            @pl.when(nonempty(e))
            def _(e=e):
                zero_copy(e).start()
        for e in range(N_EXPERTS):
            @pl.when(nonempty(e))
            def _(e=e):
                zero_copy(e).wait()

    def row_copy(j, row):
        return pltpu.make_async_copy(x_ref.at[pl.ds(j, 1)], o_hbm.at[pl.ds(row, 1)], sem)

    def start(j, c):
        for k in range(TOP_K):
            row_copy(j, pos_ref[0, 0, TOP_K * j + k]).start()
        return c

    lax.fori_loop(0, r, start, 0, unroll=DMA_UNROLL)
    for k in range(TOP_K):
        pltpu.make_async_copy(x_ref, o_hbm.at[pl.ds(0, r)], sem).wait()


def _scatter_rows(pad_end, pos, x, p_rows, *, r, tm):
    m, d = x.shape
    grid_spec = pltpu.PrefetchScalarGridSpec(
        num_scalar_prefetch=1,
        grid=(m // r,),
        in_specs=[
            pl.BlockSpec((1, 1, TOP_K * r), lambda i, pe: (i, 0, 0), memory_space=pltpu.SMEM),
            pl.BlockSpec((r, d), lambda i, pe: (i, 0)),
        ],
        out_specs=pl.BlockSpec(memory_space=pl.ANY),
        scratch_shapes=[pltpu.VMEM((tm, d), F32), pltpu.SemaphoreType.DMA(()), pltpu.SemaphoreType.DMA(())],
    )
    return pl.pallas_call(
        functools.partial(_scatter_kernel, tm=tm),
        grid_spec=grid_spec,
        out_shape=jax.ShapeDtypeStruct((p_rows, d), F32),
        compiler_params=_params(("arbitrary",)),
    )(pad_end, pos.reshape(m // r, 1, TOP_K * r), x)


def _combine_kernel(pos_ref, pos_next_ref, x_ref, gate_ref, y_hbm, *rest, bounds):
    o_refs = rest[:len(bounds)]
    buf, sems = rest[len(bounds):]
    r = x_ref.shape[0]
    i = pl.program_id(0)
    slot = i % 2

    def issue(p_ref, s):
        def start(j, c):
            for k in range(TOP_K):
                pltpu.make_async_copy(y_hbm.at[pl.ds(p_ref[0, 0, TOP_K * j + k], 1)],
                                      buf.at[s, k, pl.ds(j, 1)], sems.at[s]).start()
            return c
        lax.fori_loop(0, r, start, 0, unroll=DMA_UNROLL)

    @pl.when(i == 0)
    def _():
        issue(pos_ref, 0)

    @pl.when(i + 1 < pl.num_programs(0))
    def _():
        issue(pos_next_ref, 1 - slot)

    for k in range(TOP_K):
        pltpu.make_async_copy(y_hbm.at[pl.ds(0, r)], buf.at[slot, k], sems.at[slot]).wait()
    gate = gate_ref[...]
    out = x_ref[...] + (gate[:, 0:1] * buf[slot, 0] + gate[:, 1:2] * buf[slot, 1])

    def write(o_ref):
        o_ref[...] = out
    _select_rows(i, o_refs, bounds, write)


def _combine(pos, x, gates, y, out_rows, *, r):
    m, d = x.shape
    out_shape = [jax.ShapeDtypeStruct((n, d), F32) for n in out_rows]
    o_specs, bounds, n_row_tiles = _split_specs(out_shape, r, d)
    assert n_row_tiles * r == m
    pos3 = pos.reshape(n_row_tiles, 1, TOP_K * r)
    return pl.pallas_call(
        functools.partial(_combine_kernel, bounds=bounds),
        grid=(n_row_tiles,),
        in_specs=[
            pl.BlockSpec((1, 1, TOP_K * r), lambda i: (i, 0, 0), memory_space=pltpu.SMEM),
            pl.BlockSpec((1, 1, TOP_K * r), lambda i: (jnp.minimum(i + 1, n_row_tiles - 1), 0, 0),
                         memory_space=pltpu.SMEM),
            pl.BlockSpec((r, d), lambda i: (i, 0)),
            pl.BlockSpec((r, LANES), lambda i: (i, 0)),
            pl.BlockSpec(memory_space=pl.ANY),
        ],
        out_specs=o_specs,
        out_shape=out_shape,
        scratch_shapes=[pltpu.VMEM((2, TOP_K, r, d), F32), pltpu.SemaphoreType.DMA((2,))],
        compiler_params=_params(("arbitrary",)),
    )(pos3, pos3, x, gates, y)


def _route_plan(idx, n_tokens, tm):
    n_pairs = n_tokens * TOP_K
    n_tiles = n_pairs // tm + N_EXPERTS
    e_flat = idx[:, :TOP_K].reshape(n_pairs)
    onehot = (e_flat[:, None] == jnp.arange(N_EXPERTS)[None, :]).astype(jnp.int32)
    csum = jnp.cumsum(onehot, axis=0)
    counts = csum[-1]
    padded = ((counts + tm - 1) // tm) * tm
    pad_end = jnp.cumsum(padded)
    pad_start = pad_end - padded
    pos = jnp.sum((csum - onehot + pad_start[None, :]) * onehot, axis=1)
    tile_start = jnp.arange(n_tiles) * tm
    tile_expert = jnp.minimum(jnp.sum(tile_start[:, None] >= pad_end[None, :], axis=1), N_EXPERTS - 1)
    n_used = pad_end[-1:] // tm
    return (pos.astype(jnp.int32), pad_end.astype(jnp.int32), tile_expert.astype(jnp.int32),
            n_used.astype(jnp.int32))


def _trunk(xs, seqs, mix_norm, ffn_norm, na_w_qkv, na_q_gain, na_k_gain, na_rpb, na_w_o,
           mla_w_dqkv, mla_q_lora_gain, mla_kv_lora_gain, mla_w_uq, mla_w_ukv,
           mla_qn_gain, mla_qr_gain, mla_kn_gain, mla_kr_gain, mla_w_o,
           ffn_w_gate, ffn_w_up, ffn_w_down, moe_w_router, moe_w_gate, moe_w_up, moe_w_down):
    m = sum(x.shape[0] for x in xs)
    row2 = lambda a: a.reshape(1, -1).astype(F32)
    na_dim = NA_HEADS * NA_HEAD_DIM

    head_gain = jnp.concatenate([
        jnp.tile(na_q_gain[0] * NA_HEAD_DIM ** -0.5, NA_HEADS),
        jnp.tile(na_k_gain[0], NA_HEADS),
        jnp.ones((na_dim,), F32)]).reshape(1, -1)
    qkv = _norm_matmul(xs, row2(mix_norm[0]), na_w_qkv[0].astype(BF16), head_gain, 2 * na_dim,
                       tm=min(1024, min(x.shape[0] for x in xs)), tn=1024)
    bias = _na_bias_table(na_rpb[0])
    amax = lambda a: jnp.max(jnp.abs(a.astype(F32)))
    na_bound = (NA_HEAD_DIM ** 0.5 * amax(na_q_gain[0]) * amax(na_k_gain[0]) + amax(na_rpb[0]))
    na_safe = _safe_flag(na_bound)
    ffn_w = [ffn_w_gate[0], ffn_w_up[0], ffn_w_down[0]]
    host = max(range(len(seqs)), key=lambda gi: seqs[gi][1])
    ffn_w_bf16 = None
    attn = None
    off = 0
    for gi, (seq, nb) in enumerate(seqs):
        attn, casts = _na_attention(na_safe, qkv, bias, attn, ffn_w if gi == host else [],
                                    seq=seq, n_batch=nb, row_block_off=off // seq)
        ffn_w_bf16 = casts if casts is not None else ffn_w_bf16
        off += seq * nb
    if ffn_w_bf16 is None:
        ffn_w_bf16 = [w.astype(BF16) for w in ffn_w]
    x = _matmul_res(attn, na_w_o[0].astype(BF16), xs, tm=min(512, m))

    moe_w = (moe_w_gate[0], moe_w_up[0], moe_w_down[0])
    x, moe_w_bf16 = _ffn(x, row2(ffn_norm[0]), *ffn_w_bf16, [w.reshape(-1, w.shape[-1]) for w in moe_w],
                         tm=min(512, m), tf=512)
    moe_w_bf16 = [wb.reshape(w.shape) for wb, w in zip(moe_w_bf16, moe_w)]

    scale = (MLA_NOPE + MLA_ROPE) ** -0.5
    lat_w = MLA_Q_LORA + MLA_KV_LORA
    wd = jnp.concatenate((mla_w_dqkv[0][:, :lat_w],) + _rope_pieces(mla_w_dqkv[0][:, lat_w:]), axis=1).astype(BF16)
    wuq = mla_w_uq[0].reshape(MLA_Q_LORA, MLA_HEADS, MLA_NOPE + MLA_ROPE)
    wuq = jnp.concatenate(
        [wuq[:, :, :MLA_NOPE].reshape(MLA_Q_LORA, -1)]
        + [p.reshape(MLA_Q_LORA, -1) for p in _rope_pieces(wuq[:, :, MLA_NOPE:])], axis=1).astype(BF16)
    wukv = mla_w_ukv[0].reshape(MLA_KV_LORA, MLA_HEADS, MLA_NOPE + MLA_V)
    wukv = jnp.concatenate([wukv[:, :, :MLA_NOPE].reshape(MLA_KV_LORA, -1),
                            wukv[:, :, MLA_NOPE:].reshape(MLA_KV_LORA, -1)], axis=1).astype(BF16)
    gain_pieces = lambda gg: [p.reshape(1, -1).astype(F32) for p in _rope_pieces(gg)]
    max_seq = max(s for s, _ in seqs)
    cos, sin = _rope_tables(max_seq)
    tm_p = min(512, m)
    bounds = []
    off = 0
    for seq, nb in seqs:
        bounds.append((off // tm_p, seq // tm_p))
        off += seq * nb

    def table_block(i):
        blk = i
        for start, per in bounds:
            blk = jnp.where(i >= start, (i - start) % per, blk)
        return blk

    qn, qr, kn, kr, v = _mla_proj(
        x, row2(mix_norm[1]), wd, row2(mla_q_lora_gain[0]), row2(mla_kv_lora_gain[0]),
        *gain_pieces(mla_kr_gain[0]), wuq, row2(mla_qn_gain[0] * scale), *gain_pieces(mla_qr_gain[0] * scale),
        wukv, row2(mla_kn_gain[0]), cos, sin, table_block, tm=tm_p)
    mla_bound = scale * (MLA_NOPE * amax(mla_qn_gain[0]) * amax(mla_kn_gain[0])
                         + MLA_ROPE * amax(mla_qr_gain[0]) * amax(mla_kr_gain[0]))
    mla_safe = _safe_flag(mla_bound)
    attn = None
    off = 0
    for seq, nb in seqs:
        tq = min(FLASH_TQ, seq)
        attn = _flash(mla_safe, qn, qr, kn, kr, v, attn, seq=seq, n_batch=nb, row_off=off,
                      tq=tq, tk=min(FLASH_TK, seq), hb=FLASH_HEADS, sub_q=min(FLASH_SUB_Q, tq))
        off += seq * nb
    x = _matmul_res(attn, mla_w_o[0].astype(BF16), [x], tm=min(512, m))

    wr = jnp.pad(moe_w_router[0], ((0, 0), (0, LANES - N_EXPERTS))).astype(F32)
    idx, gates = _router(x, row2(ffn_norm[1]), wr, tm=min(512, m))
    tm_e = min(512, m)
    pos, pad_end, tile_expert, n_used = _route_plan(idx, m, tm_e)
    r = min(512, min(xx.shape[0] for xx in xs))
    x_sorted = _scatter_rows(pad_end, pos, x, tile_expert.shape[0] * tm_e, r=r, tm=tm_e)
    y_sorted = _expert_ffn(tile_expert, n_used, x_sorted, row2(ffn_norm[1]), *moe_w_bf16,
                           tm=tm_e, tf=1024, group=512)
    return _combine(pos, x, gates, y_sorted, [xx.shape[0] for xx in xs], r=r)


def kernel(x_prompt, x_sample, mix_norm, ffn_norm, na_w_qkv, na_q_gain, na_k_gain, na_rpb, na_w_o, mla_w_dqkv, mla_q_lora_gain, mla_kv_lora_gain, mla_w_uq, mla_w_ukv, mla_qn_gain, mla_qr_gain, mla_kn_gain, mla_kr_gain, mla_w_o, ffn_w_gate, ffn_w_up, ffn_w_down, moe_w_router, moe_w_gate, moe_w_up, moe_w_down):
    bp, sp, d = x_prompt.shape
    bs, ss, _ = x_sample.shape
    yp, ys = _trunk([x_prompt.reshape(bp * sp, d), x_sample.reshape(bs * ss, d)], [(sp, bp), (ss, bs)],
                    mix_norm, ffn_norm, na_w_qkv, na_q_gain, na_k_gain, na_rpb, na_w_o,
                    mla_w_dqkv, mla_q_lora_gain, mla_kv_lora_gain, mla_w_uq, mla_w_ukv,
                    mla_qn_gain, mla_qr_gain, mla_kn_gain, mla_kr_gain, mla_w_o,
                    ffn_w_gate, ffn_w_up, ffn_w_down, moe_w_router, moe_w_gate, moe_w_up, moe_w_down)
    return (yp.reshape(bp, sp, d), ys.reshape(bs, ss, d))
```

```python
import functools

import jax
import jax.numpy as jnp
from jax import lax
from jax.experimental import pallas as pl
from jax.experimental.pallas import tpu as pltpu

F32 = jnp.float32
BF16 = jnp.bfloat16

D_MODEL = 2048
GRID_W = 64
NA_HEADS = 16
NA_HEAD_DIM = 128
NA_WIN_H = 8
NA_WIN_W = 16
MLA_HEADS = 16
MLA_Q_LORA = 512
MLA_KV_LORA = 512
MLA_NOPE = 128
MLA_ROPE = 64
MLA_V = 128
ROPE_THETA = 10000.0
D_FF = 5632
N_EXPERTS = 8
TOP_K = 2
EPS = 1e-6
NEG_INF = -1e30

LANES = 128
VMEM_LIMIT = 56 * 1024 * 1024


def _params(sem):
    return pltpu.CompilerParams(dimension_semantics=sem, vmem_limit_bytes=VMEM_LIMIT)


def _rms(x, g):
    return x * lax.rsqrt(jnp.mean(x * x, axis=-1, keepdims=True) + EPS) * g


def _select_rows(i, refs, bounds, fn):
    for ref, (lo, hi) in zip(refs, bounds):
        @pl.when((i >= lo) & (i < hi))
        def _(ref=ref):
            fn(ref)


def _split_specs(xs, tm, ncols):
    specs, bounds, lo = [], [], 0
    for x in xs:
        n = x.shape[0] // tm
        assert n * tm == x.shape[0]
        specs.append(pl.BlockSpec(
            (tm, ncols), functools.partial(lambda i, *_, lo, n: (jnp.clip(i - lo, 0, n - 1), 0), lo=lo, n=n)))
        bounds.append((lo, lo + n))
        lo += n
    return specs, tuple(bounds), lo


def _norm_matmul_kernel(*refs, n_norm_tiles, bounds):
    nx = len(bounds)
    x_refs = refs[:nx]
    g_ref, w_ref, hg_ref, o_ref, h_scr = refs[nx:]
    j = pl.program_id(1)

    @pl.when(j == 0)
    def _():
        def fill(x_ref):
            h_scr[...] = _rms(x_ref[...], g_ref[...]).astype(BF16)
        _select_rows(pl.program_id(0), x_refs, bounds, fill)

    tn = o_ref.shape[1]
    chunk = min(tn, 2 * LANES)

    def emit(normed):
        for c in range(tn // chunk):
            acc = jnp.dot(h_scr[...], w_ref[:, c * chunk:(c + 1) * chunk], preferred_element_type=F32)
            for g in range(chunk // LANES):
                sl = slice(c * chunk + g * LANES, c * chunk + (g + 1) * LANES)
                a = acc[:, g * LANES:(g + 1) * LANES]
                o_ref[:, sl] = (_rms(a, hg_ref[:, sl]) if normed else a).astype(o_ref.dtype)

    @pl.when(j < n_norm_tiles)
    def _():
        emit(True)

    @pl.when(j >= n_norm_tiles)
    def _():
        emit(False)


def _norm_matmul(xs, g, w, head_gain, n_norm_cols, *, tm, tn):
    d = xs[0].shape[1]
    n = w.shape[1]
    x_specs, bounds, n_row_tiles = _split_specs(xs, tm, d)
    return pl.pallas_call(
        functools.partial(_norm_matmul_kernel, n_norm_tiles=n_norm_cols // tn, bounds=bounds),
        grid=(n_row_tiles, n // tn),
        in_specs=x_specs + [
            pl.BlockSpec((1, d), lambda i, j: (0, 0)),
            pl.BlockSpec((d, tn), lambda i, j: (0, j)),
            pl.BlockSpec((1, tn), lambda i, j: (0, j)),
        ],
        out_specs=pl.BlockSpec((tm, tn), lambda i, j: (i, j)),
        out_shape=jax.ShapeDtypeStruct((n_row_tiles * tm, n), BF16),
        scratch_shapes=[pltpu.VMEM((tm, d), BF16)],
        compiler_params=_params(("parallel", "arbitrary")),
    )(*xs, g, w, head_gain)


NA_QROWS = 4
NA_SPAN = NA_WIN_H + NA_QROWS


UNSHIFTED_SOFTMAX_BOUND = 60.0


def _safe_flag(bound):
    return (bound * 1.02 <= UNSHIFTED_SOFTMAX_BOUND).astype(jnp.int32).reshape(1)


def _na_kernel(safe_ref, q_ref, k_ref, v_ref, b_ref, *rest, rows, cast_ranges):
    nc = len(cast_ranges)
    cast_src, o_ref, cast_dst = rest[:nc], rest[-1 - nc], rest[len(rest) - nc:]
    _side_cast(pl.program_id(0) * pl.num_programs(1) + pl.program_id(1), cast_src, cast_dst, cast_ranges)
    nq = NA_QROWS * GRID_W
    nk = NA_SPAN * GRID_W
    n_blk = rows // NA_QROWS

    def run(shift):
        def body(blk, carry):
            r0 = blk * NA_QROWS
            span0 = jnp.clip(r0 - NA_WIN_H // 2, 0, rows - NA_SPAN)
            case = jnp.where(blk == 0, 0, jnp.where(blk == n_blk - 1, 2, 1))
            qoff = pl.multiple_of(r0 * GRID_W, nq)
            koff = pl.multiple_of(span0 * GRID_W, GRID_W)
            q = q_ref[pl.ds(qoff, nq), :]
            k = k_ref[pl.ds(koff, nk), :]
            v = v_ref[pl.ds(koff, nk), :]
            s = lax.dot_general(q, k, (((1,), (1,)), ((), ())), preferred_element_type=F32)
            s = s + jnp.concatenate(
                [jnp.concatenate([b_ref[0, case, i, jj] for jj in range(NA_SPAN // 2)], axis=-1)
                 for i in range(NA_QROWS)], axis=0)
            if shift:
                s = s - jnp.max(s, axis=-1, keepdims=True)
            p = jnp.exp(s)
            l = jnp.sum(p, axis=-1, keepdims=True)
            o = jnp.dot(p.astype(BF16), v, preferred_element_type=F32) / l
            o_ref[pl.ds(qoff, nq), :] = o.astype(o_ref.dtype)
            return carry

        lax.fori_loop(0, n_blk, body, 0, unroll=8)

    @pl.when(safe_ref[0] == 1)
    def _():
        run(False)

    @pl.when(safe_ref[0] != 1)
    def _():
        run(True)


def _na_bias_table(rpb):
    qc = jnp.arange(GRID_W)[:, None]
    kc = jnp.arange(GRID_W)[None, :]
    ws = jnp.clip(qc - NA_WIN_W // 2, 0, GRID_W - NA_WIN_W)
    col_ok = (kc >= ws) & (kc < ws + NA_WIN_W)
    dc = jnp.clip(kc - qc + NA_WIN_W - 1, 0, 2 * NA_WIN_W - 2)
    i = jnp.arange(NA_QROWS)
    half = NA_WIN_H // 2
    w_off = jnp.stack([jnp.zeros_like(i), i, jnp.full_like(i, NA_SPAN - NA_WIN_H)])
    c_off = jnp.stack([i, jnp.full_like(i, half), half + i])
    j = jnp.arange(NA_SPAN)[None, None, :] - w_off[:, :, None]
    row_ok = (j >= 0) & (j < NA_WIN_H)
    dr = jnp.clip(j - c_off[:, :, None] + NA_WIN_H - 1, 0, 2 * NA_WIN_H - 2)
    oh_r = jax.nn.one_hot(dr, 2 * NA_WIN_H - 1, dtype=F32).reshape(3, NA_QROWS, NA_SPAN // 2, 2, -1)
    oh_c = jax.nn.one_hot(dc, 2 * NA_WIN_W - 1, dtype=F32)
    z = jnp.zeros_like(oh_c)
    oh_c2 = jnp.stack([jnp.concatenate([oh_c, z], axis=1), jnp.concatenate([z, oh_c], axis=1)])
    u = jnp.einsum('hab,pqcb->phaqc', rpb.astype(F32), oh_c2, precision=lax.Precision.HIGHEST)
    t = jnp.einsum('xijpa,phaqc->hxijqc', oh_r, u, precision=lax.Precision.HIGHEST)
    ok = (row_ok.reshape(3, NA_QROWS, NA_SPAN // 2, 1, 2, 1) & col_ok[:, None, :]).reshape(
        3, NA_QROWS, NA_SPAN // 2, GRID_W, 2 * GRID_W)
    return jnp.where(ok[None], t, NEG_INF)


def _na_attention(safe, qkv, bias, out_prev, cast_arrays, *, seq, n_batch, row_block_off):
    m = qkv.shape[0]
    rows = seq // GRID_W
    assert rows >= NA_SPAN + NA_QROWS and rows % NA_QROWS == 0
    h = NA_HEADS
    plan = _side_cast_plan(cast_arrays, h * n_batch, lambda hh, b: hh * n_batch + b) if cast_arrays else None
    c_in, c_out, c_shapes, ranges = plan if plan is not None else ([], [], [], ())

    def spec(sec):
        return pl.BlockSpec((seq, NA_HEAD_DIM), lambda hh, b: (row_block_off + b, sec * h + hh))

    in_specs = [pl.BlockSpec(memory_space=pltpu.SMEM), spec(0), spec(1), spec(2),
                pl.BlockSpec((1,) + bias.shape[1:], lambda hh, b: (hh,) + (0,) * (bias.ndim - 1))] + c_in
    args = [safe, qkv, qkv, qkv, bias] + (list(cast_arrays) if plan is not None else [])
    aliases = {}
    if out_prev is not None:
        in_specs.append(pl.BlockSpec(memory_space=pl.ANY))
        args.append(out_prev)
        aliases = {len(args) - 1: 0}
    outs = pl.pallas_call(
        functools.partial(_na_kernel, rows=rows, cast_ranges=ranges),
        grid=(h, n_batch),
        in_specs=in_specs,
        out_specs=[pl.BlockSpec((seq, NA_HEAD_DIM), lambda hh, b: (row_block_off + b, hh))] + c_out,
        out_shape=[jax.ShapeDtypeStruct((m, h * NA_HEAD_DIM), BF16)] + c_shapes,
        input_output_aliases=aliases,
        compiler_params=_params(("arbitrary", "arbitrary")),
    )(*args)
    return outs[0], (list(outs[1:]) if plan is not None else None)


def _matmul_res_kernel(a_ref, w_ref, *refs, bounds):
    r_refs, o_ref = refs[:-1], refs[-1]
    y = jnp.dot(a_ref[...], w_ref[...], preferred_element_type=F32)

    def add(r_ref):
        o_ref[...] = r_ref[...] + y
    _select_rows(pl.program_id(0), r_refs, bounds, add)


def _matmul_res(a, w, res_list, *, tm):
    m, k = a.shape
    n = w.shape[1]
    r_specs, bounds, n_row_tiles = _split_specs(res_list, tm, n)
    assert n_row_tiles * tm == m
    return pl.pallas_call(
        functools.partial(_matmul_res_kernel, bounds=bounds),
        grid=(n_row_tiles,),
        in_specs=[
            pl.BlockSpec((tm, k), lambda i: (i, 0)),
            pl.BlockSpec((k, n), lambda i: (0, 0)),
        ] + r_specs,
        out_specs=pl.BlockSpec((tm, n), lambda i: (i, 0)),
        out_shape=jax.ShapeDtypeStruct((m, n), F32),
        compiler_params=_params(("parallel",)),
    )(a, w, *res_list)


def _swiglu_step(h, wg_ref, wu_ref, wd_ref, lead=()):
    tf = wd_ref.shape[-2]
    chunk = min(tf, 2 * LANES)
    out = None
    for c in range(tf // chunk):
        cs = slice(c * chunk, (c + 1) * chunk)
        a = jnp.dot(h, wg_ref[lead + (slice(None), cs)], preferred_element_type=F32)
        b = jnp.dot(h, wu_ref[lead + (slice(None), cs)], preferred_element_type=F32)
        t = (a / (1.0 + jnp.exp(-a)) * b).astype(BF16)
        y = jnp.dot(t, wd_ref[lead + (cs, slice(None))], preferred_element_type=F32)
        out = y if out is None else out + y
    return out


CAST_BLOCK_BYTES = 3 * 512 * 1024


def _side_cast_plan(arrays, n_steps, step_fn):
    budget = n_steps // max(len(arrays), 1)
    in_specs, out_specs, out_shapes, ranges, off = [], [], [], [], 0
    for a in arrays:
        rows, cols = a.shape
        nb = next((n for n in range(budget, 0, -1)
                   if rows % n == 0 and (rows // n) % 16 == 0), None)
        if nb is None or (rows // nb) * cols * 4 > CAST_BLOCK_BYTES:
            return None
        imap = functools.partial(lambda *ids, off, nb: (jnp.clip(step_fn(*ids) - off, 0, nb - 1), 0),
                                 off=off, nb=nb)
        in_specs.append(pl.BlockSpec((rows // nb, cols), imap))
        out_specs.append(pl.BlockSpec((rows // nb, cols), imap))
        out_shapes.append(jax.ShapeDtypeStruct((rows, cols), BF16))
        ranges.append((off, off + nb))
        off += nb
    return in_specs, out_specs, out_shapes, tuple(ranges)


def _side_cast(step, src_refs, dst_refs, ranges):
    for src, dst, (lo, hi) in zip(src_refs, dst_refs, ranges):
        @pl.when((step >= lo) & (step < hi))
        def _(src=src, dst=dst):
            dst[...] = src[...].astype(BF16)


def _ffn_kernel(x_ref, g_ref, wg_ref, wu_ref, wd_ref, *rest, cast_ranges):
    nc = len(cast_ranges)
    cast_src, o_ref, cast_dst, h_scr = rest[:nc], rest[nc], rest[nc + 1:2 * nc + 1], rest[-1]
    f = pl.program_id(1)

    @pl.when(f == 0)
    def _():
        x = x_ref[...]
        h_scr[...] = _rms(x, g_ref[...]).astype(BF16)
        o_ref[...] = x

    o_ref[...] += _swiglu_step(h_scr[...], wg_ref, wu_ref, wd_ref)
    _side_cast(pl.program_id(0) * pl.num_programs(1) + f, cast_src, cast_dst, cast_ranges)


def _ffn(x, g, wg, wu, wd, cast_arrays, *, tm, tf):
    m, d = x.shape
    nf = wg.shape[1] // tf
    plan = _side_cast_plan(cast_arrays, (m // tm) * nf, lambda i, j: i * nf + j)
    if plan is None:
        c_in, c_out, c_shapes, ranges, cast_arrays, fallback = [], [], [], (), [], cast_arrays
    else:
        (c_in, c_out, c_shapes, ranges), fallback = plan, []
    outs = pl.pallas_call(
        functools.partial(_ffn_kernel, cast_ranges=ranges),
        grid=(m // tm, nf),
        in_specs=[
            pl.BlockSpec((tm, d), lambda i, j: (i, 0)),
            pl.BlockSpec((1, d), lambda i, j: (0, 0)),
            pl.BlockSpec((d, tf), lambda i, j: (0, j)),
            pl.BlockSpec((d, tf), lambda i, j: (0, j)),
            pl.BlockSpec((tf, d), lambda i, j: (j, 0)),
        ] + c_in,
        out_specs=[pl.BlockSpec((tm, d), lambda i, j: (i, 0))] + c_out,
        out_shape=[jax.ShapeDtypeStruct((m, d), F32)] + c_shapes,
        scratch_shapes=[pltpu.VMEM((tm, d), BF16)],
        compiler_params=_params(("arbitrary", "arbitrary")),
    )(x, g, wg, wu, wd, *cast_arrays)
    return outs[0], list(outs[1:]) + [a.astype(BF16) for a in fallback]


def _expert_ffn_kernel(te_ref, nu_ref, x_ref, g_ref, wg_ref, wu_ref, wd_ref, o_ref, h_scr):
    i = pl.program_id(0)
    f = pl.program_id(1)

    @pl.when(i < nu_ref[0])
    def _():
        @pl.when(f == 0)
        def _():
            h_scr[...] = _rms(x_ref[...], g_ref[...]).astype(BF16)
            o_ref[...] = jnp.zeros_like(o_ref)

        o_ref[...] += _swiglu_step(h_scr[...], wg_ref, wu_ref, wd_ref, lead=(0,))


def _expert_ffn(tile_expert, n_used, xs, g, wg, wu, wd, *, tm, tf):
    p, d = xs.shape
    nf = wg.shape[2] // tf

    def row_map(i, j, te, nu):
        return (jnp.minimum(i, nu[0] - 1), 0)

    def wcol_map(i, j, te, nu):
        used = i < nu[0]
        return (te[jnp.minimum(i, nu[0] - 1)], 0, jnp.where(used, j, nf - 1))

    def wrow_map(i, j, te, nu):
        used = i < nu[0]
        return (te[jnp.minimum(i, nu[0] - 1)], jnp.where(used, j, nf - 1), 0)

    grid_spec = pltpu.PrefetchScalarGridSpec(
        num_scalar_prefetch=2,
        grid=(p // tm, nf),
        in_specs=[
            pl.BlockSpec((tm, d), row_map),
            pl.BlockSpec((1, d), lambda i, j, te, nu: (0, 0)),
            pl.BlockSpec((1, d, tf), wcol_map),
            pl.BlockSpec((1, d, tf), wcol_map),
            pl.BlockSpec((1, tf, d), wrow_map),
        ],
        out_specs=pl.BlockSpec((tm, d), row_map),
        scratch_shapes=[pltpu.VMEM((tm, d), BF16)],
    )
    return pl.pallas_call(
        _expert_ffn_kernel,
        grid_spec=grid_spec,
        out_shape=jax.ShapeDtypeStruct((p, d), F32),
        compiler_params=_params(("arbitrary", "arbitrary")),
    )(tile_expert, n_used, xs, g, wg, wu, wd)


ROPE_PAD = LANES


def _rope_pieces(w):
    half = MLA_ROPE // 2
    x1, x2 = w[..., :half], w[..., half:]
    z = jnp.zeros(w.shape[:-1] + (ROPE_PAD - MLA_ROPE,), w.dtype)
    return jnp.concatenate([x1, x2, z], axis=-1), jnp.concatenate([x2, x1, z], axis=-1)


def _mla_proj_kernel(x_ref, g_ref, wd_ref, gq_ref, gkv_ref, gkr_ref, gkrs_ref, wuq_ref, gqn_ref, gqr_ref,
                     gqrs_ref, wukv_ref, gkn_ref, cos_ref, sin_ref,
                     qn_ref, qr_ref, kn_ref, kr_ref, v_ref):
    cos = cos_ref[...]
    sin = sin_ref[...]

    def rope(piece, piece_sw, gain, gain_sw):
        ms = jnp.sum(piece * piece, axis=-1, keepdims=True) * (1.0 / MLA_ROPE)
        return lax.rsqrt(ms + EPS) * (piece * (gain * cos) + piece_sw * (gain_sw * sin))

    h = _rms(x_ref[...], g_ref[...]).astype(BF16)
    lat = jnp.dot(h, wd_ref[...], preferred_element_type=F32)
    cq = _rms(lat[:, :MLA_Q_LORA], gq_ref[...]).astype(BF16)
    ckv = _rms(lat[:, MLA_Q_LORA:MLA_Q_LORA + MLA_KV_LORA], gkv_ref[...]).astype(BF16)
    kr0 = MLA_Q_LORA + MLA_KV_LORA
    kr_ref[...] = rope(lat[:, kr0:kr0 + ROPE_PAD], lat[:, kr0 + ROPE_PAD:],
                       gkr_ref[...], gkrs_ref[...]).astype(BF16)

    hd = MLA_HEADS * LANES
    chunk = 4 * LANES
    for c in range(hd // chunk):
        base = c * chunk
        qn = jnp.dot(cq, wuq_ref[:, base:base + chunk], preferred_element_type=F32)
        qr = jnp.dot(cq, wuq_ref[:, hd + base:hd + base + chunk], preferred_element_type=F32)
        qs = jnp.dot(cq, wuq_ref[:, 2 * hd + base:2 * hd + base + chunk], preferred_element_type=F32)
        kn = jnp.dot(ckv, wukv_ref[:, base:base + chunk], preferred_element_type=F32)
        v = jnp.dot(ckv, wukv_ref[:, hd + base:hd + base + chunk], preferred_element_type=F32)
        v_ref[:, base:base + chunk] = v.astype(BF16)
        for hh in range(chunk // LANES):
            sl = slice(hh * LANES, (hh + 1) * LANES)
            osl = slice(base + hh * LANES, base + (hh + 1) * LANES)
            qn_ref[:, osl] = _rms(qn[:, sl], gqn_ref[...]).astype(BF16)
            kn_ref[:, osl] = _rms(kn[:, sl], gkn_ref[...]).astype(BF16)
            qr_ref[:, osl] = rope(qr[:, sl], qs[:, sl], gqr_ref[...], gqrs_ref[...]).astype(BF16)


def _mla_proj(x, g, wd, gq, gkv, gkr, gkrs, wuq, gqn, gqr, gqrs, wukv, gkn, cos, sin, table_block, *, tm):
    m, d = x.shape
    hd = MLA_HEADS * LANES
    const = lambda i: (0, 0)
    row = lambda i: (i, 0)
    tab = lambda i: (table_block(i), 0)
    return pl.pallas_call(
        _mla_proj_kernel,
        grid=(m // tm,),
        in_specs=[
            pl.BlockSpec((tm, d), row),
            pl.BlockSpec((1, d), const),
            pl.BlockSpec(wd.shape, const, pipeline_mode=pl.Buffered(1)),
            pl.BlockSpec((1, MLA_Q_LORA), const),
            pl.BlockSpec((1, MLA_KV_LORA), const),
            pl.BlockSpec((1, LANES), const),
            pl.BlockSpec((1, LANES), const),
            pl.BlockSpec(wuq.shape, const, pipeline_mode=pl.Buffered(1)),
            pl.BlockSpec((1, LANES), const),
            pl.BlockSpec((1, LANES), const),
            pl.BlockSpec((1, LANES), const),
            pl.BlockSpec(wukv.shape, const, pipeline_mode=pl.Buffered(1)),
            pl.BlockSpec((1, LANES), const),
            pl.BlockSpec((tm, LANES), tab),
            pl.BlockSpec((tm, LANES), tab),
        ],
        out_specs=[
            pl.BlockSpec((tm, hd), row),
            pl.BlockSpec((tm, hd), row),
            pl.BlockSpec((tm, hd), row),
            pl.BlockSpec((tm, LANES), row),
            pl.BlockSpec((tm, hd), row),
        ],
        out_shape=[
            jax.ShapeDtypeStruct((m, hd), BF16),
            jax.ShapeDtypeStruct((m, hd), BF16),
            jax.ShapeDtypeStruct((m, hd), BF16),
            jax.ShapeDtypeStruct((m, LANES), BF16),
            jax.ShapeDtypeStruct((m, hd), BF16),
        ],
        compiler_params=_params(("parallel",)),
    )(x, g, wd, gq, gkv, gkr, gkrs, wuq, gqn, gqr, gqrs, wukv, gkn, cos, sin)


def _rope_tables(seq):
    t = jnp.arange(seq)
    row = (t // GRID_W).astype(F32)
    col = (t % GRID_W).astype(F32)
    n_pairs = MLA_ROPE // 4
    inv = ROPE_THETA ** (-jnp.arange(n_pairs, dtype=F32) / n_pairs)
    ang = jnp.concatenate([row[:, None] * inv, col[:, None] * inv], axis=-1)
    c, s = jnp.cos(ang), jnp.sin(ang)
    z = jnp.zeros_like(c)
    return jnp.concatenate([c, c, z, z], axis=-1), jnp.concatenate([-s, s, z, z], axis=-1)


FLASH_TQ = 2048
FLASH_TK = 2048
FLASH_HEADS = 2
FLASH_SUB_Q = 256


def _flash_kernel(safe_ref, qn_ref, qr_ref, kn_ref, kr_ref, v_ref, *rest, hb, sub_q, nk):
    if nk == 1:
        o_ref = rest[-1]
    else:
        o_ref, m_scr, l_scr, acc_scr = rest[-4:]
        kk = pl.program_id(3)

        @pl.when(kk == 0)
        def _():
            m_scr[...] = jnp.full_like(m_scr, -jnp.inf)
            l_scr[...] = jnp.zeros_like(l_scr)
            acc_scr[...] = jnp.zeros_like(acc_scr)

    tq = qn_ref.shape[0]

    def run(shift):
        kr = kr_ref[...]
        for h in range(hb):
            hs = slice(h * LANES, (h + 1) * LANES)
            k = jnp.concatenate([kn_ref[:, hs], kr], axis=-1)
            v = v_ref[:, hs]
            for qi in range(tq // sub_q):
                rs = slice(qi * sub_q, (qi + 1) * sub_q)
                q = jnp.concatenate([qn_ref[rs, hs], qr_ref[rs, hs]], axis=-1)
                s = lax.dot_general(q, k, (((1,), (1,)), ((), ())), preferred_element_type=F32)
                if not shift:
                    p = jnp.exp(s)
                    l = jnp.sum(p, axis=-1, keepdims=True)
                    pv = jnp.dot(p.astype(BF16), v, preferred_element_type=F32)
                    if nk == 1:
                        o_ref[rs, hs] = (pv / l).astype(o_ref.dtype)
                    else:
                        l_scr[h, rs] += l
                        acc_scr[rs, hs] += pv
                elif nk == 1:
                    p = jnp.exp(s - jnp.max(s, axis=-1, keepdims=True))
                    l = jnp.sum(p, axis=-1, keepdims=True)
                    o = jnp.dot(p.astype(BF16), v, preferred_element_type=F32) / l
                    o_ref[rs, hs] = o.astype(o_ref.dtype)
                else:
                    m_prev = m_scr[h, rs]
                    m_new = jnp.maximum(m_prev, jnp.max(s, axis=-1, keepdims=True))
                    alpha = jnp.exp(m_prev - m_new)
                    p = jnp.exp(s - m_new)
                    l_scr[h, rs] = alpha * l_scr[h, rs] + jnp.sum(p, axis=-1, keepdims=True)
                    acc_scr[rs, hs] = (alpha * acc_scr[rs, hs]
                                       + jnp.dot(p.astype(BF16), v, preferred_element_type=F32))
                    m_scr[h, rs] = m_new

    @pl.when(safe_ref[0] == 1)
    def _():
        run(False)

    @pl.when(safe_ref[0] != 1)
    def _():
        run(True)

    if nk > 1:
        @pl.when(kk == nk - 1)
        def _():
            for h in range(hb):
                hs = slice(h * LANES, (h + 1) * LANES)
                o_ref[:, hs] = (acc_scr[:, hs] / l_scr[h]).astype(o_ref.dtype)


def _flash(safe, qn, qr, kn, kr, v, out_prev, *, seq, n_batch, row_off, tq, tk, hb, sub_q):
    m = qn.shape[0]
    h = MLA_HEADS
    nq, nk = seq // tq, seq // tk
    qoff, koff = row_off // tq, row_off // tk
    qmap = lambda b, hh, i, j: (qoff + b * nq + i, hh)
    kmap = lambda b, hh, i, j: (koff + b * nk + j, hh)
    in_specs = [
        pl.BlockSpec(memory_space=pltpu.SMEM),
        pl.BlockSpec((tq, hb * LANES), qmap),
        pl.BlockSpec((tq, hb * LANES), qmap),
        pl.BlockSpec((tk, hb * LANES), kmap),
        pl.BlockSpec((tk, LANES), lambda b, hh, i, j: (koff + b * nk + j, 0)),
        pl.BlockSpec((tk, hb * LANES), kmap),
    ]
    args = [safe, qn, qr, kn, kr, v]
    aliases = {}
    if out_prev is not None:
        in_specs.append(pl.BlockSpec(memory_space=pl.ANY))
        args.append(out_prev)
        aliases = {6: 0}
    scratch = []
    if nk > 1:
        scratch = [pltpu.VMEM((hb, tq, 1), F32), pltpu.VMEM((hb, tq, 1), F32), pltpu.VMEM((tq, hb * MLA_V), F32)]
    return pl.pallas_call(
        functools.partial(_flash_kernel, hb=hb, sub_q=sub_q, nk=nk),
        grid=(n_batch, h // hb, nq, nk),
        in_specs=in_specs,
        out_specs=pl.BlockSpec((tq, hb * LANES), qmap),
        out_shape=jax.ShapeDtypeStruct((m, h * MLA_V), BF16),
        scratch_shapes=scratch,
        input_output_aliases=aliases,
        compiler_params=_params(("parallel", "parallel", "parallel", "arbitrary")),
    )(*args)


def _router_kernel(x_ref, g_ref, wr_ref, idx_ref, gate_ref):
    h = _rms(x_ref[...], g_ref[...])
    w = wr_ref[...]
    h_hi = h.astype(BF16)
    h_lo = (h - h_hi.astype(F32)).astype(BF16)
    w_hi = w.astype(BF16)
    w_lo = (w - w_hi.astype(F32)).astype(BF16)
    logits = (jnp.dot(h_hi, w_hi, preferred_element_type=F32)
              + (jnp.dot(h_lo, w_hi, preferred_element_type=F32) + jnp.dot(h_hi, w_lo, preferred_element_type=F32)))
    lane = lax.broadcasted_iota(jnp.int32, logits.shape, 1)
    logits = jnp.where(lane < N_EXPERTS, logits, -jnp.inf)
    m1 = jnp.max(logits, axis=-1, keepdims=True)
    i1 = jnp.min(jnp.where(logits == m1, lane, LANES), axis=-1, keepdims=True)
    rest = jnp.where(lane == i1, -jnp.inf, logits)
    m2 = jnp.max(rest, axis=-1, keepdims=True)
    i2 = jnp.min(jnp.where(rest == m2, lane, LANES), axis=-1, keepdims=True)
    e = jnp.exp(m2 - m1)
    g1 = 1.0 / (1.0 + e)
    g2 = e / (1.0 + e)
    idx_ref[...] = jnp.where(lane == 0, i1, jnp.where(lane == 1, i2, 0))
    gate_ref[...] = jnp.where(lane == 0, g1, jnp.where(lane == 1, g2, 0.0))


def _router(x, g, wr, *, tm):
    m, d = x.shape
    return pl.pallas_call(
        _router_kernel,
        grid=(m // tm,),
        in_specs=[
            pl.BlockSpec((tm, d), lambda i: (i, 0)),
            pl.BlockSpec((1, d), lambda i: (0, 0)),
            pl.BlockSpec((d, LANES), lambda i: (0, 0)),
        ],
        out_specs=[pl.BlockSpec((tm, LANES), lambda i: (i, 0)), pl.BlockSpec((tm, LANES), lambda i: (i, 0))],
        out_shape=[jax.ShapeDtypeStruct((m, LANES), jnp.int32), jax.ShapeDtypeStruct((m, LANES), F32)],
        compiler_params=_params(("parallel",)),
    )(x, g, wr)


DMA_UNROLL = 8


def _scatter_kernel(pe_ref, pos_ref, x_ref, o_hbm, zero_scr, sem, zsem, *, tm):
    r = x_ref.shape[0]

    def zero_copy(e):
        end = pe_ref[e]
        return pltpu.make_async_copy(zero_scr, o_hbm.at[pl.ds(pl.multiple_of(end - tm, tm), tm)], zsem)

    def nonempty(e):
        return pe_ref[e] > (pe_ref[e - 1] if e else 0)

    @pl.when(pl.program_id(0) == 0)
    def _():
        zero_scr[...] = jnp.zeros_like(zero_scr)
        for e in range(N_EXPERTS):
            @pl.when(nonempty(e))
            def _(e=e):
                zero_copy(e).start()
        for e in range(N_EXPERTS):
            @pl.when(nonempty(e))
            def _(e=e):
                zero_copy(e).wait()

    def row_copy(j, row):
        return pltpu.make_async_copy(x_ref.at[pl.ds(j, 1)], o_hbm.at[pl.ds(row, 1)], sem)

    def start(j, c):
        for k in range(TOP_K):
            row_copy(j, pos_ref[0, 0, TOP_K * j + k]).start()
        return c

    lax.fori_loop(0, r, start, 0, unroll=DMA_UNROLL)
    for k in range(TOP_K):
        pltpu.make_async_copy(x_ref, o_hbm.at[pl.ds(0, r)], sem).wait()


def _scatter_rows(pad_end, pos, x, p_rows, *, r, tm):
    m, d = x.shape
    grid_spec = pltpu.PrefetchScalarGridSpec(
        num_scalar_prefetch=1,
        grid=(m // r,),
        in_specs=[
            pl.BlockSpec((1, 1, TOP_K * r), lambda i, pe: (i, 0, 0), memory_space=pltpu.SMEM),
            pl.BlockSpec((r, d), lambda i, pe: (i, 0)),
        ],
        out_specs=pl.BlockSpec(memory_space=pl.ANY),
        scratch_shapes=[pltpu.VMEM((tm, d), F32), pltpu.SemaphoreType.DMA(()), pltpu.SemaphoreType.DMA(())],
    )
    return pl.pallas_call(
        functools.partial(_scatter_kernel, tm=tm),
        grid_spec=grid_spec,
        out_shape=jax.ShapeDtypeStruct((p_rows, d), F32),
        compiler_params=_params(("arbitrary",)),
    )(pad_end, pos.reshape(m // r, 1, TOP_K * r), x)


def _combine_kernel(pos_ref, pos_next_ref, x_ref, gate_ref, y_hbm, *rest, bounds):
    o_refs = rest[:len(bounds)]
    buf, sems = rest[len(bounds):]
    r = x_ref.shape[0]
    i = pl.program_id(0)
    slot = i % 2

    def issue(p_ref, s):
        def start(j, c):
            for k in range(TOP_K):
                pltpu.make_async_copy(y_hbm.at[pl.ds(p_ref[0, 0, TOP_K * j + k], 1)],
                                      buf.at[s, k, pl.ds(j, 1)], sems.at[s]).start()
            return c
        lax.fori_loop(0, r, start, 0, unroll=DMA_UNROLL)

    @pl.when(i == 0)
    def _():
        issue(pos_ref, 0)

    @pl.when(i + 1 < pl.num_programs(0))
    def _():
        issue(pos_next_ref, 1 - slot)

    for k in range(TOP_K):
        pltpu.make_async_copy(y_hbm.at[pl.ds(0, r)], buf.at[slot, k], sems.at[slot]).wait()
    gate = gate_ref[...]
    out = x_ref[...] + (gate[:, 0:1] * buf[slot, 0] + gate[:, 1:2] * buf[slot, 1])

    def write(o_ref):
        o_ref[...] = out
    _select_rows(i, o_refs, bounds, write)


def _combine(pos, x, gates, y, out_rows, *, r):
    m, d = x.shape
    out_shape = [jax.ShapeDtypeStruct((n, d), F32) for n in out_rows]
    o_specs, bounds, n_row_tiles = _split_specs(out_shape, r, d)
    assert n_row_tiles * r == m
    pos3 = pos.reshape(n_row_tiles, 1, TOP_K * r)
    return pl.pallas_call(
        functools.partial(_combine_kernel, bounds=bounds),
        grid=(n_row_tiles,),
        in_specs=[
            pl.BlockSpec((1, 1, TOP_K * r), lambda i: (i, 0, 0), memory_space=pltpu.SMEM),
            pl.BlockSpec((1, 1, TOP_K * r), lambda i: (jnp.minimum(i + 1, n_row_tiles - 1), 0, 0),
                         memory_space=pltpu.SMEM),
            pl.BlockSpec((r, d), lambda i: (i, 0)),
            pl.BlockSpec((r, LANES), lambda i: (i, 0)),
            pl.BlockSpec(memory_space=pl.ANY),
        ],
        out_specs=o_specs,
        out_shape=out_shape,
        scratch_shapes=[pltpu.VMEM((2, TOP_K, r, d), F32), pltpu.SemaphoreType.DMA((2,))],
        compiler_params=_params(("arbitrary",)),
    )(pos3, pos3, x, gates, y)


def _route_plan(idx, n_tokens, tm):
    n_pairs = n_tokens * TOP_K
    n_tiles = n_pairs // tm + N_EXPERTS
    e_flat = idx[:, :TOP_K].reshape(n_pairs)
    onehot = (e_flat[:, None] == jnp.arange(N_EXPERTS)[None, :]).astype(jnp.int32)
    csum = jnp.cumsum(onehot, axis=0)
    counts = csum[-1]
    padded = ((counts + tm - 1) // tm) * tm
    pad_end = jnp.cumsum(padded)
    pad_start = pad_end - padded
    pos = jnp.sum((csum - onehot + pad_start[None, :]) * onehot, axis=1)
    tile_start = jnp.arange(n_tiles) * tm
    tile_expert = jnp.minimum(jnp.sum(tile_start[:, None] >= pad_end[None, :], axis=1), N_EXPERTS - 1)
    n_used = pad_end[-1:] // tm
    return (pos.astype(jnp.int32), pad_end.astype(jnp.int32), tile_expert.astype(jnp.int32),
            n_used.astype(jnp.int32))


def _trunk(xs, seqs, mix_norm, ffn_norm, na_w_qkv, na_q_gain, na_k_gain, na_rpb, na_w_o,
           mla_w_dqkv, mla_q_lora_gain, mla_kv_lora_gain, mla_w_uq, mla_w_ukv,
           mla_qn_gain, mla_qr_gain, mla_kn_gain, mla_kr_gain, mla_w_o,
           ffn_w_gate, ffn_w_up, ffn_w_down, moe_w_router, moe_w_gate, moe_w_up, moe_w_down):
    m = sum(x.shape[0] for x in xs)
    row2 = lambda a: a.reshape(1, -1).astype(F32)
    na_dim = NA_HEADS * NA_HEAD_DIM

    head_gain = jnp.concatenate([
        jnp.tile(na_q_gain[0] * NA_HEAD_DIM ** -0.5, NA_HEADS),
        jnp.tile(na_k_gain[0], NA_HEADS),
        jnp.ones((na_dim,), F32)]).reshape(1, -1)
    qkv = _norm_matmul(xs, row2(mix_norm[0]), na_w_qkv[0].astype(BF16), head_gain, 2 * na_dim,
                       tm=min(1024, min(x.shape[0] for x in xs)), tn=1024)
    bias = _na_bias_table(na_rpb[0])
    amax = lambda a: jnp.max(jnp.abs(a.astype(F32)))
    na_bound = (NA_HEAD_DIM ** 0.5 * amax(na_q_gain[0]) * amax(na_k_gain[0]) + amax(na_rpb[0]))
    na_safe = _safe_flag(na_bound)
    ffn_w = [ffn_w_gate[0], ffn_w_up[0], ffn_w_down[0]]
    host = max(range(len(seqs)), key=lambda gi: seqs[gi][1])
    ffn_w_bf16 = None
    attn = None
    off = 0
    for gi, (seq, nb) in enumerate(seqs):
        attn, casts = _na_attention(na_safe, qkv, bias, attn, ffn_w if gi == host else [],
                                    seq=seq, n_batch=nb, row_block_off=off // seq)
        ffn_w_bf16 = casts if casts is not None else ffn_w_bf16
        off += seq * nb
    if ffn_w_bf16 is None:
        ffn_w_bf16 = [w.astype(BF16) for w in ffn_w]
    x = _matmul_res(attn, na_w_o[0].astype(BF16), xs, tm=min(512, m))

    moe_w = (moe_w_gate[0], moe_w_up[0], moe_w_down[0])
    x, moe_w_bf16 = _ffn(x, row2(ffn_norm[0]), *ffn_w_bf16, [w.reshape(-1, w.shape[-1]) for w in moe_w],
                         tm=min(512, m), tf=512)
    moe_w_bf16 = [wb.reshape(w.shape) for wb, w in zip(moe_w_bf16, moe_w)]

    scale = (MLA_NOPE + MLA_ROPE) ** -0.5
    lat_w = MLA_Q_LORA + MLA_KV_LORA
    wd = jnp.concatenate((mla_w_dqkv[0][:, :lat_w],) + _rope_pieces(mla_w_dqkv[0][:, lat_w:]), axis=1).astype(BF16)
    wuq = mla_w_uq[0].reshape(MLA_Q_LORA, MLA_HEADS, MLA_NOPE + MLA_ROPE)
    wuq = jnp.concatenate(
        [wuq[:, :, :MLA_NOPE].reshape(MLA_Q_LORA, -1)]
        + [p.reshape(MLA_Q_LORA, -1) for p in _rope_pieces(wuq[:, :, MLA_NOPE:])], axis=1).astype(BF16)
    wukv = mla_w_ukv[0].reshape(MLA_KV_LORA, MLA_HEADS, MLA_NOPE + MLA_V)
    wukv = jnp.concatenate([wukv[:, :, :MLA_NOPE].reshape(MLA_KV_LORA, -1),
                            wukv[:, :, MLA_NOPE:].reshape(MLA_KV_LORA, -1)], axis=1).astype(BF16)
    gain_pieces = lambda gg: [p.reshape(1, -1).astype(F32) for p in _rope_pieces(gg)]
    max_seq = max(s for s, _ in seqs)
    cos, sin = _rope_tables(max_seq)
    tm_p = min(512, m)
    bounds = []
    off = 0
    for seq, nb in seqs:
        bounds.append((off // tm_p, seq // tm_p))
        off += seq * nb

    def table_block(i):
        blk = i
        for start, per in bounds:
            blk = jnp.where(i >= start, (i - start) % per, blk)
        return blk

    qn, qr, kn, kr, v = _mla_proj(
        x, row2(mix_norm[1]), wd, row2(mla_q_lora_gain[0]), row2(mla_kv_lora_gain[0]),
        *gain_pieces(mla_kr_gain[0]), wuq, row2(mla_qn_gain[0] * scale), *gain_pieces(mla_qr_gain[0] * scale),
        wukv, row2(mla_kn_gain[0]), cos, sin, table_block, tm=tm_p)
    mla_bound = scale * (MLA_NOPE * amax(mla_qn_gain[0]) * amax(mla_kn_gain[0])
                         + MLA_ROPE * amax(mla_qr_gain[0]) * amax(mla_kr_gain[0]))
    mla_safe = _safe_flag(mla_bound)
    attn = None
    off = 0
    for seq, nb in seqs:
        tq = min(FLASH_TQ, seq)
        attn = _flash(mla_safe, qn, qr, kn, kr, v, attn, seq=seq, n_batch=nb, row_off=off,
                      tq=tq, tk=min(FLASH_TK, seq), hb=FLASH_HEADS, sub_q=min(FLASH_SUB_Q, tq))
        off += seq * nb
    x = _matmul_res(attn, mla_w_o[0].astype(BF16), [x], tm=min(512, m))

    wr = jnp.pad(moe_w_router[0], ((0, 0), (0, LANES - N_EXPERTS))).astype(F32)
    idx, gates = _router(x, row2(ffn_norm[1]), wr, tm=min(512, m))
    tm_e = min(512, m)
    pos, pad_end, tile_expert, n_used = _route_plan(idx, m, tm_e)
    r = min(512, min(xx.shape[0] for xx in xs))
    x_sorted = _scatter_rows(pad_end, pos, x, tile_expert.shape[0] * tm_e, r=r, tm=tm_e)
    y_sorted = _expert_ffn(tile_expert, n_used, x_sorted, row2(ffn_norm[1]), *moe_w_bf16, tm=tm_e, tf=512)
    return _combine(pos, x, gates, y_sorted, [xx.shape[0] for xx in xs], r=r)


def kernel(x_prompt, x_sample, mix_norm, ffn_norm, na_w_qkv, na_q_gain, na_k_gain, na_rpb, na_w_o, mla_w_dqkv, mla_q_lora_gain, mla_kv_lora_gain, mla_w_uq, mla_w_ukv, mla_qn_gain, mla_qr_gain, mla_kn_gain, mla_kr_gain, mla_w_o, ffn_w_gate, ffn_w_up, ffn_w_down, moe_w_router, moe_w_gate, moe_w_up, moe_w_down):
    bp, sp, d = x_prompt.shape
    bs, ss, _ = x_sample.shape
    yp, ys = _trunk([x_prompt.reshape(bp * sp, d), x_sample.reshape(bs * ss, d)], [(sp, bp), (ss, bs)],
                    mix_norm, ffn_norm, na_w_qkv, na_q_gain, na_k_gain, na_rpb, na_w_o,
                    mla_w_dqkv, mla_q_lora_gain, mla_kv_lora_gain, mla_w_uq, mla_w_ukv,
                    mla_qn_gain, mla_qr_gain, mla_kn_gain, mla_kr_gain, mla_w_o,
                    ffn_w_gate, ffn_w_up, ffn_w_down, moe_w_router, moe_w_gate, moe_w_up, moe_w_down)
    return (yp.reshape(bp, sp, d), ys.reshape(bs, ss, d))
```

```python
import functools

import jax
import jax.numpy as jnp
from jax import lax
from jax.experimental import pallas as pl
from jax.experimental.pallas import tpu as pltpu

F32 = jnp.float32
BF16 = jnp.bfloat16

D_MODEL = 2048
GRID_W = 64
NA_HEADS = 16
NA_HEAD_DIM = 128
NA_WIN_H = 8
NA_WIN_W = 16
MLA_HEADS = 16
MLA_Q_LORA = 512
MLA_KV_LORA = 512
MLA_NOPE = 128
MLA_ROPE = 64
MLA_V = 128
ROPE_THETA = 10000.0
D_FF = 5632
N_EXPERTS = 8
TOP_K = 2
EPS = 1e-6
NEG_INF = -1e30

LANES = 128
VMEM_LIMIT = 56 * 1024 * 1024


def _params(sem):
    return pltpu.CompilerParams(dimension_semantics=sem, vmem_limit_bytes=VMEM_LIMIT)


def _rms(x, g):
    return x * lax.rsqrt(jnp.mean(x * x, axis=-1, keepdims=True) + EPS) * g


def _select_rows(i, refs, bounds, fn):
    for ref, (lo, hi) in zip(refs, bounds):
        @pl.when((i >= lo) & (i < hi))
        def _(ref=ref):
            fn(ref)


def _split_specs(xs, tm, ncols):
    specs, bounds, lo = [], [], 0
    for x in xs:
        n = x.shape[0] // tm
        assert n * tm == x.shape[0]
        specs.append(pl.BlockSpec(
            (tm, ncols), functools.partial(lambda i, *_, lo, n: (jnp.clip(i - lo, 0, n - 1), 0), lo=lo, n=n)))
        bounds.append((lo, lo + n))
        lo += n
    return specs, tuple(bounds), lo


def _norm_matmul_kernel(*refs, n_norm_tiles, bounds):
    nx = len(bounds)
    x_refs = refs[:nx]
    g_ref, w_ref, hg_ref, o_ref, h_scr = refs[nx:]
    j = pl.program_id(1)

    @pl.when(j == 0)
    def _():
        def fill(x_ref):
            h_scr[...] = _rms(x_ref[...], g_ref[...]).astype(BF16)
        _select_rows(pl.program_id(0), x_refs, bounds, fill)

    tn = o_ref.shape[1]
    chunk = min(tn, 2 * LANES)

    def emit(normed):
        for c in range(tn // chunk):
            acc = jnp.dot(h_scr[...], w_ref[:, c * chunk:(c + 1) * chunk], preferred_element_type=F32)
            for g in range(chunk // LANES):
                sl = slice(c * chunk + g * LANES, c * chunk + (g + 1) * LANES)
                a = acc[:, g * LANES:(g + 1) * LANES]
                o_ref[:, sl] = (_rms(a, hg_ref[:, sl]) if normed else a).astype(o_ref.dtype)

    @pl.when(j < n_norm_tiles)
    def _():
        emit(True)

    @pl.when(j >= n_norm_tiles)
    def _():
        emit(False)


def _norm_matmul(xs, g, w, head_gain, n_norm_cols, *, tm, tn):
    d = xs[0].shape[1]
    n = w.shape[1]
    x_specs, bounds, n_row_tiles = _split_specs(xs, tm, d)
    return pl.pallas_call(
        functools.partial(_norm_matmul_kernel, n_norm_tiles=n_norm_cols // tn, bounds=bounds),
        grid=(n_row_tiles, n // tn),
        in_specs=x_specs + [
            pl.BlockSpec((1, d), lambda i, j: (0, 0)),
            pl.BlockSpec((d, tn), lambda i, j: (0, j)),
            pl.BlockSpec((1, tn), lambda i, j: (0, j)),
        ],
        out_specs=pl.BlockSpec((tm, tn), lambda i, j: (i, j)),
        out_shape=jax.ShapeDtypeStruct((n_row_tiles * tm, n), BF16),
        scratch_shapes=[pltpu.VMEM((tm, d), BF16)],
        compiler_params=_params(("parallel", "arbitrary")),
    )(*xs, g, w, head_gain)


NA_QROWS = 4
NA_SPAN = NA_WIN_H + NA_QROWS


UNSHIFTED_SOFTMAX_BOUND = 60.0


def _safe_flag(bound):
    return (bound * 1.02 <= UNSHIFTED_SOFTMAX_BOUND).astype(jnp.int32).reshape(1)


def _na_kernel(safe_ref, q_ref, k_ref, v_ref, b_ref, *rest, rows, cast_ranges):
    nc = len(cast_ranges)
    cast_src, o_ref, cast_dst = rest[:nc], rest[-1 - nc], rest[len(rest) - nc:]
    _side_cast(pl.program_id(0) * pl.num_programs(1) + pl.program_id(1), cast_src, cast_dst, cast_ranges)
    nq = NA_QROWS * GRID_W
    nk = NA_SPAN * GRID_W
    n_blk = rows // NA_QROWS

    def run(shift):
        def body(blk, carry):
            r0 = blk * NA_QROWS
            span0 = jnp.clip(r0 - NA_WIN_H // 2, 0, rows - NA_SPAN)
            case = jnp.where(blk == 0, 0, jnp.where(blk == n_blk - 1, 2, 1))
            qoff = pl.multiple_of(r0 * GRID_W, nq)
            koff = pl.multiple_of(span0 * GRID_W, GRID_W)
            q = q_ref[pl.ds(qoff, nq), :]
            k = k_ref[pl.ds(koff, nk), :]
            v = v_ref[pl.ds(koff, nk), :]
            s = lax.dot_general(q, k, (((1,), (1,)), ((), ())), preferred_element_type=F32)
            s = s + jnp.concatenate(
                [jnp.concatenate([b_ref[0, case, i, jj] for jj in range(NA_SPAN // 2)], axis=-1)
                 for i in range(NA_QROWS)], axis=0)
            if shift:
                s = s - jnp.max(s, axis=-1, keepdims=True)
            p = jnp.exp(s)
            l = jnp.sum(p, axis=-1, keepdims=True)
            o = jnp.dot(p.astype(BF16), v, preferred_element_type=F32) / l
            o_ref[pl.ds(qoff, nq), :] = o.astype(o_ref.dtype)
            return carry

        lax.fori_loop(0, n_blk, body, 0, unroll=8)

    @pl.when(safe_ref[0] == 1)
    def _():
        run(False)

    @pl.when(safe_ref[0] != 1)
    def _():
        run(True)


def _na_bias_table(rpb):
    qc = jnp.arange(GRID_W)[:, None]
    kc = jnp.arange(GRID_W)[None, :]
    ws = jnp.clip(qc - NA_WIN_W // 2, 0, GRID_W - NA_WIN_W)
    col_ok = (kc >= ws) & (kc < ws + NA_WIN_W)
    dc = jnp.clip(kc - qc + NA_WIN_W - 1, 0, 2 * NA_WIN_W - 2)
    i = jnp.arange(NA_QROWS)
    half = NA_WIN_H // 2
    w_off = jnp.stack([jnp.zeros_like(i), i, jnp.full_like(i, NA_SPAN - NA_WIN_H)])
    c_off = jnp.stack([i, jnp.full_like(i, half), half + i])
    j = jnp.arange(NA_SPAN)[None, None, :] - w_off[:, :, None]
    row_ok = (j >= 0) & (j < NA_WIN_H)
    dr = jnp.clip(j - c_off[:, :, None] + NA_WIN_H - 1, 0, 2 * NA_WIN_H - 2)
    oh_r = jax.nn.one_hot(dr, 2 * NA_WIN_H - 1, dtype=F32).reshape(3, NA_QROWS, NA_SPAN // 2, 2, -1)
    oh_c = jax.nn.one_hot(dc, 2 * NA_WIN_W - 1, dtype=F32)
    z = jnp.zeros_like(oh_c)
    oh_c2 = jnp.stack([jnp.concatenate([oh_c, z], axis=1), jnp.concatenate([z, oh_c], axis=1)])
    u = jnp.einsum('hab,pqcb->phaqc', rpb.astype(F32), oh_c2, precision=lax.Precision.HIGHEST)
    t = jnp.einsum('xijpa,phaqc->hxijqc', oh_r, u, precision=lax.Precision.HIGHEST)
    ok = (row_ok.reshape(3, NA_QROWS, NA_SPAN // 2, 1, 2, 1) & col_ok[:, None, :]).reshape(
        3, NA_QROWS, NA_SPAN // 2, GRID_W, 2 * GRID_W)
    return jnp.where(ok[None], t, NEG_INF)


def _na_attention(safe, qkv, bias, out_prev, cast_arrays, *, seq, n_batch, row_block_off):
    m = qkv.shape[0]
    rows = seq // GRID_W
    assert rows >= NA_SPAN + NA_QROWS and rows % NA_QROWS == 0
    h = NA_HEADS
    plan = _side_cast_plan(cast_arrays, h * n_batch, lambda hh, b: hh * n_batch + b) if cast_arrays else None
    c_in, c_out, c_shapes, ranges = plan if plan is not None else ([], [], [], ())

    def spec(sec):
        return pl.BlockSpec((seq, NA_HEAD_DIM), lambda hh, b: (row_block_off + b, sec * h + hh))

    in_specs = [pl.BlockSpec(memory_space=pltpu.SMEM), spec(0), spec(1), spec(2),
                pl.BlockSpec((1,) + bias.shape[1:], lambda hh, b: (hh,) + (0,) * (bias.ndim - 1))] + c_in
    args = [safe, qkv, qkv, qkv, bias] + (list(cast_arrays) if plan is not None else [])
    aliases = {}
    if out_prev is not None:
        in_specs.append(pl.BlockSpec(memory_space=pl.ANY))
        args.append(out_prev)
        aliases = {len(args) - 1: 0}
    outs = pl.pallas_call(
        functools.partial(_na_kernel, rows=rows, cast_ranges=ranges),
        grid=(h, n_batch),
        in_specs=in_specs,
        out_specs=[pl.BlockSpec((seq, NA_HEAD_DIM), lambda hh, b: (row_block_off + b, hh))] + c_out,
        out_shape=[jax.ShapeDtypeStruct((m, h * NA_HEAD_DIM), BF16)] + c_shapes,
        input_output_aliases=aliases,
        compiler_params=_params(("arbitrary", "arbitrary")),
    )(*args)
    return outs[0], (list(outs[1:]) if plan is not None else None)


def _matmul_res_kernel(a_ref, w_ref, *refs, bounds):
    r_refs, o_ref = refs[:-1], refs[-1]
    y = jnp.dot(a_ref[...], w_ref[...], preferred_element_type=F32)

    def add(r_ref):
        o_ref[...] = r_ref[...] + y
    _select_rows(pl.program_id(0), r_refs, bounds, add)


def _matmul_res(a, w, res_list, *, tm):
    m, k = a.shape
    n = w.shape[1]
    r_specs, bounds, n_row_tiles = _split_specs(res_list, tm, n)
    assert n_row_tiles * tm == m
    return pl.pallas_call(
        functools.partial(_matmul_res_kernel, bounds=bounds),
        grid=(n_row_tiles,),
        in_specs=[
            pl.BlockSpec((tm, k), lambda i: (i, 0)),
            pl.BlockSpec((k, n), lambda i: (0, 0)),
        ] + r_specs,
        out_specs=pl.BlockSpec((tm, n), lambda i: (i, 0)),
        out_shape=jax.ShapeDtypeStruct((m, n), F32),
        compiler_params=_params(("parallel",)),
    )(a, w, *res_list)


def _swiglu_step(h, wg_ref, wu_ref, wd_ref, lead=()):
    tf = wd_ref.shape[-2]
    chunk = min(tf, 2 * LANES)
    out = None
    for c in range(tf // chunk):
        cs = slice(c * chunk, (c + 1) * chunk)
        a = jnp.dot(h, wg_ref[lead + (slice(None), cs)], preferred_element_type=F32)
        b = jnp.dot(h, wu_ref[lead + (slice(None), cs)], preferred_element_type=F32)
        t = (a / (1.0 + jnp.exp(-a)) * b).astype(BF16)
        y = jnp.dot(t, wd_ref[lead + (cs, slice(None))], preferred_element_type=F32)
        out = y if out is None else out + y
    return out


CAST_BLOCK_BYTES = 3 * 512 * 1024


def _side_cast_plan(arrays, n_steps, step_fn):
    budget = n_steps // max(len(arrays), 1)
    in_specs, out_specs, out_shapes, ranges, off = [], [], [], [], 0
    for a in arrays:
        rows, cols = a.shape
        nb = next((n for n in range(budget, 0, -1)
                   if rows % n == 0 and (rows // n) % 16 == 0), None)
        if nb is None or (rows // nb) * cols * 4 > CAST_BLOCK_BYTES:
            return None
        imap = functools.partial(lambda *ids, off, nb: (jnp.clip(step_fn(*ids) - off, 0, nb - 1), 0),
                                 off=off, nb=nb)
        in_specs.append(pl.BlockSpec((rows // nb, cols), imap))
        out_specs.append(pl.BlockSpec((rows // nb, cols), imap))
        out_shapes.append(jax.ShapeDtypeStruct((rows, cols), BF16))
        ranges.append((off, off + nb))
        off += nb
    return in_specs, out_specs, out_shapes, tuple(ranges)


def _side_cast(step, src_refs, dst_refs, ranges):
    for src, dst, (lo, hi) in zip(src_refs, dst_refs, ranges):
        @pl.when((step >= lo) & (step < hi))
        def _(src=src, dst=dst):
            dst[...] = src[...].astype(BF16)


def _ffn_kernel(x_ref, g_ref, wg_ref, wu_ref, wd_ref, *rest, cast_ranges):
    nc = len(cast_ranges)
    cast_src, o_ref, cast_dst, h_scr = rest[:nc], rest[nc], rest[nc + 1:2 * nc + 1], rest[-1]
    f = pl.program_id(1)

    @pl.when(f == 0)
    def _():
        x = x_ref[...]
        h_scr[...] = _rms(x, g_ref[...]).astype(BF16)
        o_ref[...] = x

    o_ref[...] += _swiglu_step(h_scr[...], wg_ref, wu_ref, wd_ref)
    _side_cast(pl.program_id(0) * pl.num_programs(1) + f, cast_src, cast_dst, cast_ranges)


def _ffn(x, g, wg, wu, wd, cast_arrays, *, tm, tf):
    m, d = x.shape
    nf = wg.shape[1] // tf
    plan = _side_cast_plan(cast_arrays, (m // tm) * nf, lambda i, j: i * nf + j)
    if plan is None:
        c_in, c_out, c_shapes, ranges, cast_arrays, fallback = [], [], [], (), [], cast_arrays
    else:
        (c_in, c_out, c_shapes, ranges), fallback = plan, []
    outs = pl.pallas_call(
        functools.partial(_ffn_kernel, cast_ranges=ranges),
        grid=(m // tm, nf),
        in_specs=[
            pl.BlockSpec((tm, d), lambda i, j: (i, 0)),
            pl.BlockSpec((1, d), lambda i, j: (0, 0)),
            pl.BlockSpec((d, tf), lambda i, j: (0, j)),
            pl.BlockSpec((d, tf), lambda i, j: (0, j)),
            pl.BlockSpec((tf, d), lambda i, j: (j, 0)),
        ] + c_in,
        out_specs=[pl.BlockSpec((tm, d), lambda i, j: (i, 0))] + c_out,
        out_shape=[jax.ShapeDtypeStruct((m, d), F32)] + c_shapes,
        scratch_shapes=[pltpu.VMEM((tm, d), BF16)],
        compiler_params=_params(("arbitrary", "arbitrary")),
    )(x, g, wg, wu, wd, *cast_arrays)
    return outs[0], list(outs[1:]) + [a.astype(BF16) for a in fallback]


def _expert_ffn_kernel(te_ref, nu_ref, x_ref, g_ref, wg_ref, wu_ref, wd_ref, o_ref, h_scr):
    i = pl.program_id(0)
    f = pl.program_id(1)

    @pl.when(i < nu_ref[0])
    def _():
        @pl.when(f == 0)
        def _():
            h_scr[...] = _rms(x_ref[...], g_ref[...]).astype(BF16)
            o_ref[...] = jnp.zeros_like(o_ref)

        o_ref[...] += _swiglu_step(h_scr[...], wg_ref, wu_ref, wd_ref, lead=(0,))


def _expert_ffn(tile_expert, n_used, xs, g, wg, wu, wd, *, tm, tf):
    p, d = xs.shape
    nf = wg.shape[2] // tf

    def row_map(i, j, te, nu):
        return (jnp.minimum(i, nu[0] - 1), 0)

    def wcol_map(i, j, te, nu):
        used = i < nu[0]
        return (te[jnp.minimum(i, nu[0] - 1)], 0, jnp.where(used, j, nf - 1))

    def wrow_map(i, j, te, nu):
        used = i < nu[0]
        return (te[jnp.minimum(i, nu[0] - 1)], jnp.where(used, j, nf - 1), 0)

    grid_spec = pltpu.PrefetchScalarGridSpec(
        num_scalar_prefetch=2,
        grid=(p // tm, nf),
        in_specs=[
            pl.BlockSpec((tm, d), row_map),
            pl.BlockSpec((1, d), lambda i, j, te, nu: (0, 0)),
            pl.BlockSpec((1, d, tf), wcol_map),
            pl.BlockSpec((1, d, tf), wcol_map),
            pl.BlockSpec((1, tf, d), wrow_map),
        ],
        out_specs=pl.BlockSpec((tm, d), row_map),
        scratch_shapes=[pltpu.VMEM((tm, d), BF16)],
    )
    return pl.pallas_call(
        _expert_ffn_kernel,
        grid_spec=grid_spec,
        out_shape=jax.ShapeDtypeStruct((p, d), F32),
        compiler_params=_params(("arbitrary", "arbitrary")),
    )(tile_expert, n_used, xs, g, wg, wu, wd)


ROPE_PAD = LANES


def _rope_pieces(w):
    half = MLA_ROPE // 2
    x1, x2 = w[..., :half], w[..., half:]
    z = jnp.zeros(w.shape[:-1] + (ROPE_PAD - MLA_ROPE,), w.dtype)
    return jnp.concatenate([x1, x2, z], axis=-1), jnp.concatenate([x2, x1, z], axis=-1)


def _mla_proj_kernel(x_ref, g_ref, wd_ref, gq_ref, gkv_ref, gkr_ref, gkrs_ref, wuq_ref, gqn_ref, gqr_ref,
                     gqrs_ref, wukv_ref, gkn_ref, cos_ref, sin_ref,
                     qn_ref, qr_ref, kn_ref, kr_ref, v_ref):
    cos = cos_ref[...]
    sin = sin_ref[...]

    def rope(piece, piece_sw, gain, gain_sw):
        ms = jnp.sum(piece * piece, axis=-1, keepdims=True) * (1.0 / MLA_ROPE)
        return lax.rsqrt(ms + EPS) * (piece * (gain * cos) + piece_sw * (gain_sw * sin))

    h = _rms(x_ref[...], g_ref[...]).astype(BF16)
    lat = jnp.dot(h, wd_ref[...], preferred_element_type=F32)
    cq = _rms(lat[:, :MLA_Q_LORA], gq_ref[...]).astype(BF16)
    ckv = _rms(lat[:, MLA_Q_LORA:MLA_Q_LORA + MLA_KV_LORA], gkv_ref[...]).astype(BF16)
    kr0 = MLA_Q_LORA + MLA_KV_LORA
    kr_ref[...] = rope(lat[:, kr0:kr0 + ROPE_PAD], lat[:, kr0 + ROPE_PAD:],
                       gkr_ref[...], gkrs_ref[...]).astype(BF16)

    hd = MLA_HEADS * LANES
    chunk = 4 * LANES
    for c in range(hd // chunk):
        base = c * chunk
        qn = jnp.dot(cq, wuq_ref[:, base:base + chunk], preferred_element_type=F32)
        qr = jnp.dot(cq, wuq_ref[:, hd + base:hd + base + chunk], preferred_element_type=F32)
        qs = jnp.dot(cq, wuq_ref[:, 2 * hd + base:2 * hd + base + chunk], preferred_element_type=F32)
        kn = jnp.dot(ckv, wukv_ref[:, base:base + chunk], preferred_element_type=F32)
        v = jnp.dot(ckv, wukv_ref[:, hd + base:hd + base + chunk], preferred_element_type=F32)
        v_ref[:, base:base + chunk] = v.astype(BF16)
        for hh in range(chunk // LANES):
            sl = slice(hh * LANES, (hh + 1) * LANES)
            osl = slice(base + hh * LANES, base + (hh + 1) * LANES)
            qn_ref[:, osl] = _rms(qn[:, sl], gqn_ref[...]).astype(BF16)
            kn_ref[:, osl] = _rms(kn[:, sl], gkn_ref[...]).astype(BF16)
            qr_ref[:, osl] = rope(qr[:, sl], qs[:, sl], gqr_ref[...], gqrs_ref[...]).astype(BF16)


def _mla_proj(x, g, wd, gq, gkv, gkr, gkrs, wuq, gqn, gqr, gqrs, wukv, gkn, cos, sin, table_block, *, tm):
    m, d = x.shape
    hd = MLA_HEADS * LANES
    const = lambda i: (0, 0)
    row = lambda i: (i, 0)
    tab = lambda i: (table_block(i), 0)
    return pl.pallas_call(
        _mla_proj_kernel,
        grid=(m // tm,),
        in_specs=[
            pl.BlockSpec((tm, d), row),
            pl.BlockSpec((1, d), const),
            pl.BlockSpec(wd.shape, const, pipeline_mode=pl.Buffered(1)),
            pl.BlockSpec((1, MLA_Q_LORA), const),
            pl.BlockSpec((1, MLA_KV_LORA), const),
            pl.BlockSpec((1, LANES), const),
            pl.BlockSpec((1, LANES), const),
            pl.BlockSpec(wuq.shape, const, pipeline_mode=pl.Buffered(1)),
            pl.BlockSpec((1, LANES), const),
            pl.BlockSpec((1, LANES), const),
            pl.BlockSpec((1, LANES), const),
            pl.BlockSpec(wukv.shape, const, pipeline_mode=pl.Buffered(1)),
            pl.BlockSpec((1, LANES), const),
            pl.BlockSpec((tm, LANES), tab),
            pl.BlockSpec((tm, LANES), tab),
        ],
        out_specs=[
            pl.BlockSpec((tm, hd), row),
            pl.BlockSpec((tm, hd), row),
            pl.BlockSpec((tm, hd), row),
            pl.BlockSpec((tm, LANES), row),
            pl.BlockSpec((tm, hd), row),
        ],
        out_shape=[
            jax.ShapeDtypeStruct((m, hd), BF16),
            jax.ShapeDtypeStruct((m, hd), BF16),
            jax.ShapeDtypeStruct((m, hd), BF16),
            jax.ShapeDtypeStruct((m, LANES), BF16),
            jax.ShapeDtypeStruct((m, hd), BF16),
        ],
        compiler_params=_params(("parallel",)),
    )(x, g, wd, gq, gkv, gkr, gkrs, wuq, gqn, gqr, gqrs, wukv, gkn, cos, sin)


def _rope_tables(seq):
    t = jnp.arange(seq)
    row = (t // GRID_W).astype(F32)
    col = (t % GRID_W).astype(F32)
    n_pairs = MLA_ROPE // 4
    inv = ROPE_THETA ** (-jnp.arange(n_pairs, dtype=F32) / n_pairs)
    ang = jnp.concatenate([row[:, None] * inv, col[:, None] * inv], axis=-1)
    c, s = jnp.cos(ang), jnp.sin(ang)
    z = jnp.zeros_like(c)
    return jnp.concatenate([c, c, z, z], axis=-1), jnp.concatenate([-s, s, z, z], axis=-1)


FLASH_TQ = 2048
FLASH_TK = 2048
FLASH_HEADS = 2
FLASH_SUB_Q = 256


def _flash_kernel(safe_ref, qn_ref, qr_ref, kn_ref, kr_ref, v_ref, *rest, hb, sub_q, nk):
    if nk == 1:
        o_ref = rest[-1]
    else:
        o_ref, m_scr, l_scr, acc_scr = rest[-4:]
        kk = pl.program_id(3)

        @pl.when(kk == 0)
        def _():
            m_scr[...] = jnp.full_like(m_scr, -jnp.inf)
            l_scr[...] = jnp.zeros_like(l_scr)
            acc_scr[...] = jnp.zeros_like(acc_scr)

    tq = qn_ref.shape[0]

    def run(shift):
        kr = kr_ref[...]
        for h in range(hb):
            hs = slice(h * LANES, (h + 1) * LANES)
            k = jnp.concatenate([kn_ref[:, hs], kr], axis=-1)
            v = v_ref[:, hs]
            for qi in range(tq // sub_q):
                rs = slice(qi * sub_q, (qi + 1) * sub_q)
                q = jnp.concatenate([qn_ref[rs, hs], qr_ref[rs, hs]], axis=-1)
                s = lax.dot_general(q, k, (((1,), (1,)), ((), ())), preferred_element_type=F32)
                if not shift:
                    p = jnp.exp(s)
                    l = jnp.sum(p, axis=-1, keepdims=True)
                    pv = jnp.dot(p.astype(BF16), v, preferred_element_type=F32)
                    if nk == 1:
                        o_ref[rs, hs] = (pv / l).astype(o_ref.dtype)
                    else:
                        l_scr[h, rs] += l
                        acc_scr[rs, hs] += pv
                elif nk == 1:
                    p = jnp.exp(s - jnp.max(s, axis=-1, keepdims=True))
                    l = jnp.sum(p, axis=-1, keepdims=True)
                    o = jnp.dot(p.astype(BF16), v, preferred_element_type=F32) / l
                    o_ref[rs, hs] = o.astype(o_ref.dtype)
                else:
                    m_prev = m_scr[h, rs]
                    m_new = jnp.maximum(m_prev, jnp.max(s, axis=-1, keepdims=True))
                    alpha = jnp.exp(m_prev - m_new)
                    p = jnp.exp(s - m_new)
                    l_scr[h, rs] = alpha * l_scr[h, rs] + jnp.sum(p, axis=-1, keepdims=True)
                    acc_scr[rs, hs] = (alpha * acc_scr[rs, hs]
                                       + jnp.dot(p.astype(BF16), v, preferred_element_type=F32))
                    m_scr[h, rs] = m_new

    @pl.when(safe_ref[0] == 1)
    def _():
        run(False)

    @pl.when(safe_ref[0] != 1)
    def _():
        run(True)

    if nk > 1:
        @pl.when(kk == nk - 1)
        def _():
            for h in range(hb):
                hs = slice(h * LANES, (h + 1) * LANES)
                o_ref[:, hs] = (acc_scr[:, hs] / l_scr[h]).astype(o_ref.dtype)


def _flash(safe, qn, qr, kn, kr, v, out_prev, *, seq, n_batch, row_off, tq, tk, hb, sub_q):
    m = qn.shape[0]
    h = MLA_HEADS
    nq, nk = seq // tq, seq // tk
    qoff, koff = row_off // tq, row_off // tk
    qmap = lambda b, hh, i, j: (qoff + b * nq + i, hh)
    kmap = lambda b, hh, i, j: (koff + b * nk + j, hh)
    in_specs = [
        pl.BlockSpec(memory_space=pltpu.SMEM),
        pl.BlockSpec((tq, hb * LANES), qmap),
        pl.BlockSpec((tq, hb * LANES), qmap),
        pl.BlockSpec((tk, hb * LANES), kmap),
        pl.BlockSpec((tk, LANES), lambda b, hh, i, j: (koff + b * nk + j, 0)),
        pl.BlockSpec((tk, hb * LANES), kmap),
    ]
    args = [safe, qn, qr, kn, kr, v]
    aliases = {}
    if out_prev is not None:
        in_specs.append(pl.BlockSpec(memory_space=pl.ANY))
        args.append(out_prev)
        aliases = {6: 0}
    scratch = []
    if nk > 1:
        scratch = [pltpu.VMEM((hb, tq, 1), F32), pltpu.VMEM((hb, tq, 1), F32), pltpu.VMEM((tq, hb * MLA_V), F32)]
    return pl.pallas_call(
        functools.partial(_flash_kernel, hb=hb, sub_q=sub_q, nk=nk),
        grid=(n_batch, h // hb, nq, nk),
        in_specs=in_specs,
        out_specs=pl.BlockSpec((tq, hb * LANES), qmap),
        out_shape=jax.ShapeDtypeStruct((m, h * MLA_V), BF16),
        scratch_shapes=scratch,
        input_output_aliases=aliases,
        compiler_params=_params(("parallel", "parallel", "parallel", "arbitrary")),
    )(*args)


def _router_kernel(x_ref, g_ref, wr_ref, idx_ref, gate_ref):
    h = _rms(x_ref[...], g_ref[...])
    w = wr_ref[...]
    h_hi = h.astype(BF16)
    h_lo = (h - h_hi.astype(F32)).astype(BF16)
    w_hi = w.astype(BF16)
    w_lo = (w - w_hi.astype(F32)).astype(BF16)
    logits = (jnp.dot(h_hi, w_hi, preferred_element_type=F32)
              + (jnp.dot(h_lo, w_hi, preferred_element_type=F32) + jnp.dot(h_hi, w_lo, preferred_element_type=F32)))
    lane = lax.broadcasted_iota(jnp.int32, logits.shape, 1)
    logits = jnp.where(lane < N_EXPERTS, logits, -jnp.inf)
    m1 = jnp.max(logits, axis=-1, keepdims=True)
    i1 = jnp.min(jnp.where(logits == m1, lane, LANES), axis=-1, keepdims=True)
    rest = jnp.where(lane == i1, -jnp.inf, logits)
    m2 = jnp.max(rest, axis=-1, keepdims=True)
    i2 = jnp.min(jnp.where(rest == m2, lane, LANES), axis=-1, keepdims=True)
    e = jnp.exp(m2 - m1)
    g1 = 1.0 / (1.0 + e)
    g2 = e / (1.0 + e)
    idx_ref[...] = jnp.where(lane == 0, i1, jnp.where(lane == 1, i2, 0))
    gate_ref[...] = jnp.where(lane == 0, g1, jnp.where(lane == 1, g2, 0.0))


def _router(x, g, wr, *, tm):
    m, d = x.shape
    return pl.pallas_call(
        _router_kernel,
        grid=(m // tm,),
        in_specs=[
            pl.BlockSpec((tm, d), lambda i: (i, 0)),
            pl.BlockSpec((1, d), lambda i: (0, 0)),
            pl.BlockSpec((d, LANES), lambda i: (0, 0)),
        ],
        out_specs=[pl.BlockSpec((tm, LANES), lambda i: (i, 0)), pl.BlockSpec((tm, LANES), lambda i: (i, 0))],
        out_shape=[jax.ShapeDtypeStruct((m, LANES), jnp.int32), jax.ShapeDtypeStruct((m, LANES), F32)],
        compiler_params=_params(("parallel",)),
    )(x, g, wr)


DMA_UNROLL = 8


def _scatter_kernel(pe_ref, pos_ref, x_ref, o_hbm, zero_scr, sem, zsem, *, tm):
    r = x_ref.shape[0]

    def zero_copy(e):
        end = pe_ref[e]
        return pltpu.make_async_copy(zero_scr, o_hbm.at[pl.ds(pl.multiple_of(end - tm, tm), tm)], zsem)

    def nonempty(e):
        return pe_ref[e] > (pe_ref[e - 1] if e else 0)

    @pl.when(pl.program_id(0) == 0)
    def _():
        zero_scr[...] = jnp.zeros_like(zero_scr)
        for e in range(N_EXPERTS):
            @pl.when(nonempty(e))
            def _(e=e):
                zero_copy(e).start()
        for e in range(N_EXPERTS):
            @pl.when(nonempty(e))
            def _(e=e):
                zero_copy(e).wait()

    def row_copy(j, row):
        return pltpu.make_async_copy(x_ref.at[pl.ds(j, 1)], o_hbm.at[pl.ds(row, 1)], sem)

    def start(j, c):
        for k in range(TOP_K):
            row_copy(j, pos_ref[0, 0, TOP_K * j + k]).start(priority=k % 2)
        return c

    lax.fori_loop(0, r, start, 0, unroll=DMA_UNROLL)
    for k in range(TOP_K):
        pltpu.make_async_copy(x_ref, o_hbm.at[pl.ds(0, r)], sem).wait()


def _scatter_rows(pad_end, pos, x, p_rows, *, r, tm):
    m, d = x.shape
    grid_spec = pltpu.PrefetchScalarGridSpec(
        num_scalar_prefetch=1,
        grid=(m // r,),
        in_specs=[
            pl.BlockSpec((1, 1, TOP_K * r), lambda i, pe: (i, 0, 0), memory_space=pltpu.SMEM),
            pl.BlockSpec((r, d), lambda i, pe: (i, 0)),
        ],
        out_specs=pl.BlockSpec(memory_space=pl.ANY),
        scratch_shapes=[pltpu.VMEM((tm, d), F32), pltpu.SemaphoreType.DMA(()), pltpu.SemaphoreType.DMA(())],
    )
    return pl.pallas_call(
        functools.partial(_scatter_kernel, tm=tm),
        grid_spec=grid_spec,
        out_shape=jax.ShapeDtypeStruct((p_rows, d), F32),
        compiler_params=_params(("arbitrary",)),
    )(pad_end, pos.reshape(m // r, 1, TOP_K * r), x)


def _combine_kernel(pos_ref, pos_next_ref, x_ref, gate_ref, y_hbm, *rest, bounds):
    o_refs = rest[:len(bounds)]
    buf, sems = rest[len(bounds):]
    r = x_ref.shape[0]
    i = pl.program_id(0)
    slot = i % 2

    def issue(p_ref, s):
        def start(j, c):
            for k in range(TOP_K):
                pltpu.make_async_copy(y_hbm.at[pl.ds(p_ref[0, 0, TOP_K * j + k], 1)],
                                      buf.at[s, k, pl.ds(j, 1)], sems.at[s]).start(priority=k % 2)
            return c
        lax.fori_loop(0, r, start, 0, unroll=DMA_UNROLL)

    @pl.when(i == 0)
    def _():
        issue(pos_ref, 0)

    @pl.when(i + 1 < pl.num_programs(0))
    def _():
        issue(pos_next_ref, 1 - slot)

    for k in range(TOP_K):
        pltpu.make_async_copy(y_hbm.at[pl.ds(0, r)], buf.at[slot, k], sems.at[slot]).wait()
    gate = gate_ref[...]
    out = x_ref[...] + (gate[:, 0:1] * buf[slot, 0] + gate[:, 1:2] * buf[slot, 1])

    def write(o_ref):
        o_ref[...] = out
    _select_rows(i, o_refs, bounds, write)


def _combine(pos, x, gates, y, out_rows, *, r):
    m, d = x.shape
    out_shape = [jax.ShapeDtypeStruct((n, d), F32) for n in out_rows]
    o_specs, bounds, n_row_tiles = _split_specs(out_shape, r, d)
    assert n_row_tiles * r == m
    pos3 = pos.reshape(n_row_tiles, 1, TOP_K * r)
    return pl.pallas_call(
        functools.partial(_combine_kernel, bounds=bounds),
        grid=(n_row_tiles,),
        in_specs=[
            pl.BlockSpec((1, 1, TOP_K * r), lambda i: (i, 0, 0), memory_space=pltpu.SMEM),
            pl.BlockSpec((1, 1, TOP_K * r), lambda i: (jnp.minimum(i + 1, n_row_tiles - 1), 0, 0),
                         memory_space=pltpu.SMEM),
            pl.BlockSpec((r, d), lambda i: (i, 0)),
            pl.BlockSpec((r, LANES), lambda i: (i, 0)),
            pl.BlockSpec(memory_space=pl.ANY),
        ],
        out_specs=o_specs,
        out_shape=out_shape,
        scratch_shapes=[pltpu.VMEM((2, TOP_K, r, d), F32), pltpu.SemaphoreType.DMA((2,))],
        compiler_params=_params(("arbitrary",)),
    )(pos3, pos3, x, gates, y)


def _route_plan(idx, n_tokens, tm):
    n_pairs = n_tokens * TOP_K
    n_tiles = n_pairs // tm + N_EXPERTS
    e_flat = idx[:, :TOP_K].reshape(n_pairs)
    onehot = (e_flat[:, None] == jnp.arange(N_EXPERTS)[None, :]).astype(jnp.int32)
    csum = jnp.cumsum(onehot, axis=0)
    counts = csum[-1]
    padded = ((counts + tm - 1) // tm) * tm
    pad_end = jnp.cumsum(padded)
    pad_start = pad_end - padded
    pos = jnp.sum((csum - onehot + pad_start[None, :]) * onehot, axis=1)
    tile_start = jnp.arange(n_tiles) * tm
    tile_expert = jnp.minimum(jnp.sum(tile_start[:, None] >= pad_end[None, :], axis=1), N_EXPERTS - 1)
    n_used = pad_end[-1:] // tm
    return (pos.astype(jnp.int32), pad_end.astype(jnp.int32), tile_expert.astype(jnp.int32),
            n_used.astype(jnp.int32))


def _trunk(xs, seqs, mix_norm, ffn_norm, na_w_qkv, na_q_gain, na_k_gain, na_rpb, na_w_o,
           mla_w_dqkv, mla_q_lora_gain, mla_kv_lora_gain, mla_w_uq, mla_w_ukv,
           mla_qn_gain, mla_qr_gain, mla_kn_gain, mla_kr_gain, mla_w_o,
           ffn_w_gate, ffn_w_up, ffn_w_down, moe_w_router, moe_w_gate, moe_w_up, moe_w_down):
    m = sum(x.shape[0] for x in xs)
    row2 = lambda a: a.reshape(1, -1).astype(F32)
    na_dim = NA_HEADS * NA_HEAD_DIM

    head_gain = jnp.concatenate([
        jnp.tile(na_q_gain[0] * NA_HEAD_DIM ** -0.5, NA_HEADS),
        jnp.tile(na_k_gain[0], NA_HEADS),
        jnp.ones((na_dim,), F32)]).reshape(1, -1)
    qkv = _norm_matmul(xs, row2(mix_norm[0]), na_w_qkv[0].astype(BF16), head_gain, 2 * na_dim,
                       tm=min(1024, min(x.shape[0] for x in xs)), tn=1024)
    bias = _na_bias_table(na_rpb[0])
    amax = lambda a: jnp.max(jnp.abs(a.astype(F32)))
    na_bound = (NA_HEAD_DIM ** 0.5 * amax(na_q_gain[0]) * amax(na_k_gain[0]) + amax(na_rpb[0]))
    na_safe = _safe_flag(na_bound)
    ffn_w = [ffn_w_gate[0], ffn_w_up[0], ffn_w_down[0]]
    host = max(range(len(seqs)), key=lambda gi: seqs[gi][1])
    ffn_w_bf16 = None
    attn = None
    off = 0
    for gi, (seq, nb) in enumerate(seqs):
        attn, casts = _na_attention(na_safe, qkv, bias, attn, ffn_w if gi == host else [],
                                    seq=seq, n_batch=nb, row_block_off=off // seq)
        ffn_w_bf16 = casts if casts is not None else ffn_w_bf16
        off += seq * nb
    if ffn_w_bf16 is None:
        ffn_w_bf16 = [w.astype(BF16) for w in ffn_w]
    x = _matmul_res(attn, na_w_o[0].astype(BF16), xs, tm=min(512, m))

    moe_w = (moe_w_gate[0], moe_w_up[0], moe_w_down[0])
    x, moe_w_bf16 = _ffn(x, row2(ffn_norm[0]), *ffn_w_bf16, [w.reshape(-1, w.shape[-1]) for w in moe_w],
                         tm=min(512, m), tf=512)
    moe_w_bf16 = [wb.reshape(w.shape) for wb, w in zip(moe_w_bf16, moe_w)]

    scale = (MLA_NOPE + MLA_ROPE) ** -0.5
    lat_w = MLA_Q_LORA + MLA_KV_LORA
    wd = jnp.concatenate((mla_w_dqkv[0][:, :lat_w],) + _rope_pieces(mla_w_dqkv[0][:, lat_w:]), axis=1).astype(BF16)
    wuq = mla_w_uq[0].reshape(MLA_Q_LORA, MLA_HEADS, MLA_NOPE + MLA_ROPE)
    wuq = jnp.concatenate(
        [wuq[:, :, :MLA_NOPE].reshape(MLA_Q_LORA, -1)]
        + [p.reshape(MLA_Q_LORA, -1) for p in _rope_pieces(wuq[:, :, MLA_NOPE:])], axis=1).astype(BF16)
    wukv = mla_w_ukv[0].reshape(MLA_KV_LORA, MLA_HEADS, MLA_NOPE + MLA_V)
    wukv = jnp.concatenate([wukv[:, :, :MLA_NOPE].reshape(MLA_KV_LORA, -1),
                            wukv[:, :, MLA_NOPE:].reshape(MLA_KV_LORA, -1)], axis=1).astype(BF16)
    gain_pieces = lambda gg: [p.reshape(1, -1).astype(F32) for p in _rope_pieces(gg)]
    max_seq = max(s for s, _ in seqs)
    cos, sin = _rope_tables(max_seq)
    tm_p = min(512, m)
    bounds = []
    off = 0
    for seq, nb in seqs:
        bounds.append((off // tm_p, seq // tm_p))
        off += seq * nb

    def table_block(i):
        blk = i
        for start, per in bounds:
            blk = jnp.where(i >= start, (i - start) % per, blk)
        return blk

    qn, qr, kn, kr, v = _mla_proj(
        x, row2(mix_norm[1]), wd, row2(mla_q_lora_gain[0]), row2(mla_kv_lora_gain[0]),
        *gain_pieces(mla_kr_gain[0]), wuq, row2(mla_qn_gain[0] * scale), *gain_pieces(mla_qr_gain[0] * scale),
        wukv, row2(mla_kn_gain[0]), cos, sin, table_block, tm=tm_p)
    mla_bound = scale * (MLA_NOPE * amax(mla_qn_gain[0]) * amax(mla_kn_gain[0])
                         + MLA_ROPE * amax(mla_qr_gain[0]) * amax(mla_kr_gain[0]))
    mla_safe = _safe_flag(mla_bound)
    attn = None
    off = 0
    for seq, nb in seqs:
        tq = min(FLASH_TQ, seq)
        attn = _flash(mla_safe, qn, qr, kn, kr, v, attn, seq=seq, n_batch=nb, row_off=off,
                      tq=tq, tk=min(FLASH_TK, seq), hb=FLASH_HEADS, sub_q=min(FLASH_SUB_Q, tq))
        off += seq * nb
    x = _matmul_res(attn, mla_w_o[0].astype(BF16), [x], tm=min(512, m))

    wr = jnp.pad(moe_w_router[0], ((0, 0), (0, LANES - N_EXPERTS))).astype(F32)
    idx, gates = _router(x, row2(ffn_norm[1]), wr, tm=min(512, m))
    tm_e = min(512, m)
    pos, pad_end, tile_expert, n_used = _route_plan(idx, m, tm_e)
    r = min(512, min(xx.shape[0] for xx in xs))
    x_sorted = _scatter_rows(pad_end, pos, x, tile_expert.shape[0] * tm_e, r=r, tm=tm_e)
    y_sorted = _expert_ffn(tile_expert, n_used, x_sorted, row2(ffn_norm[1]), *moe_w_bf16, tm=tm_e, tf=512)
    return _combine(pos, x, gates, y_sorted, [xx.shape[0] for xx in xs], r=r)


def kernel(x_prompt, x_sample, mix_norm, ffn_norm, na_w_qkv, na_q_gain, na_k_gain, na_rpb, na_w_o, mla_w_dqkv, mla_q_lora_gain, mla_kv_lora_gain, mla_w_uq, mla_w_ukv, mla_qn_gain, mla_qr_gain, mla_kn_gain, mla_kr_gain, mla_w_o, ffn_w_gate, ffn_w_up, ffn_w_down, moe_w_router, moe_w_gate, moe_w_up, moe_w_down):
    bp, sp, d = x_prompt.shape
    bs, ss, _ = x_sample.shape
    yp, ys = _trunk([x_prompt.reshape(bp * sp, d), x_sample.reshape(bs * ss, d)], [(sp, bp), (ss, bs)],
                    mix_norm, ffn_norm, na_w_qkv, na_q_gain, na_k_gain, na_rpb, na_w_o,
                    mla_w_dqkv, mla_q_lora_gain, mla_kv_lora_gain, mla_w_uq, mla_w_ukv,
                    mla_qn_gain, mla_qr_gain, mla_kn_gain, mla_kr_gain, mla_w_o,
                    ffn_w_gate, ffn_w_up, ffn_w_down, moe_w_router, moe_w_gate, moe_w_up, moe_w_down)
    return (yp.reshape(bp, sp, d), ys.reshape(bs, ss, d))
```
